```python
import jax, jax.numpy as jnp
from jax import lax
import numpy as np

D_MODEL = 2048
BATCH = 4
SEQ = 4096
DEPTH = 1

MIX_WIDTH = D_MODEL
HEAD_DIM = 64
NSA_WIDTH = MIX_WIDTH // 2
NSA_HEADS = NSA_WIDTH // HEAD_DIM
NSA_KV_HEADS = 4
NSA_GROUP = NSA_HEADS // NSA_KV_HEADS
NSA_KV_WIDTH = NSA_KV_HEADS * HEAD_DIM
CMP_BLOCK = 32
CMP_STRIDE = 16
SEL_BLOCK = 64
N_SELECT = 16
WINDOW = 512
N_GATES = 3
Q_BLOCK = 64
RWKV_WIDTH = MIX_WIDTH - NSA_WIDTH
RWKV_HEADS = RWKV_WIDTH // HEAD_DIM
DECAY_LORA = 96
ICLR_LORA = 96
GATE_LORA = 256
GN_EPS = 64e-5
N_EXPERTS = 32
TOP_K = 4
D_EXPERT = D_MODEL
SWIGLU_ALPHA = 1.702
SWIGLU_LIMIT = 7.0
MOE_BLOCK = 256
PLE_DIM = 256
LN_EPS = 1e-5
DEEPNORM_ALPHA = (2 * DEPTH) ** 0.25
DEEPNORM_BETA = (8 * DEPTH) ** -0.25
NEG_INF = -1e30

NSA_COLS = NSA_WIDTH + 6 * NSA_KV_WIDTH + NSA_HEADS * N_GATES
RWKV_COLS = 3 * RWKV_WIDTH + DECAY_LORA + ICLR_LORA + GATE_LORA
IN_COLS = NSA_COLS + RWKV_COLS
NSA_SPLIT = tuple(NSA_WIDTH + j * NSA_KV_WIDTH for j in range(7))
RWKV_SPLIT = (RWKV_WIDTH, 2 * RWKV_WIDTH, 3 * RWKV_WIDTH, 3 * RWKV_WIDTH + DECAY_LORA, 3 * RWKV_WIDTH + DECAY_LORA + ICLR_LORA)

kernel_name = "hymba_nsa_rwkv7_moe_deepnorm"


def layer_norm(x, g, b):
    xf = x.astype(jnp.float32)
    mean = xf.mean(-1, keepdims=True)
    var = jnp.square(xf - mean).mean(-1, keepdims=True)
    return ((xf - mean) * lax.rsqrt(var + LN_EPS) * g + b).astype(x.dtype)


def masked_softmax(s, mask):
    s = jnp.where(mask, s, NEG_INF)
    e = jnp.where(mask, jnp.exp(s - jnp.max(s, axis=-1, keepdims=True)), 0.0)
    return e / jnp.maximum(e.sum(-1, keepdims=True), 1e-30)


def alibi_slopes(n):
    return jnp.exp2(-8.0 * jnp.arange(1, n + 1, dtype=jnp.float32) / n)


def compress_kv(kv, pe, w1, w2):
    b, t_len, g, d = kv.shape
    ratio = CMP_BLOCK // CMP_STRIDE
    n_chunks = t_len // CMP_STRIDE
    n_cmp = n_chunks - ratio + 1
    chunks = kv.reshape(b, n_chunks, CMP_STRIDE, g, d)
    blocks = jnp.concatenate([chunks[:, r:r + n_cmp] for r in range(ratio)], axis=2)
    blocks = (blocks + pe[:, None, :]).transpose(0, 1, 3, 2, 4).reshape(b, n_cmp, g, CMP_BLOCK * d)
    return jax.nn.gelu(blocks @ w1) @ w2


def nsa_attention(q, k_cmp, v_cmp, k_slc, v_slc, k_win, v_win, gate_logits):
    b, t_len = q.shape[:2]
    n_cmp = k_cmp.shape[1]
    n_slc = t_len // SEL_BLOCK
    n_sel = min(N_SELECT, n_slc)
    scale = HEAD_DIM ** -0.5
    slopes = alibi_slopes(NSA_HEADS).reshape(NSA_KV_HEADS, NSA_GROUP)[:, :, None, None]
    cmp_end = jnp.arange(n_cmp) * CMP_STRIDE + CMP_BLOCK - 1
    cmp_start = cmp_end - CMP_BLOCK + 1
    slc_start = jnp.arange(n_slc) * SEL_BLOCK
    overlap = jnp.clip(jnp.minimum(cmp_end[:, None], slc_start[None, :] + SEL_BLOCK - 1)
                       - jnp.maximum(cmp_start[:, None], slc_start[None, :]) + 1, 0).astype(jnp.float32)

    def to_blocks(a):
        return a.reshape(b, n_slc, SEL_BLOCK, NSA_KV_HEADS, HEAD_DIM).transpose(0, 3, 1, 2, 4).reshape(
            b, NSA_KV_HEADS, n_slc, SEL_BLOCK * HEAD_DIM)

    k_sb, v_sb = to_blocks(k_slc), to_blocks(v_slc)
    pad = ((0, 0), (WINDOW, 0), (0, 0), (0, 0))
    k_wp, v_wp = jnp.pad(k_win, pad), jnp.pad(v_win, pad)
    b_idx = jnp.arange(b)[:, None, None, None]
    g_idx = jnp.arange(NSA_KV_HEADS)[None, :, None, None]
    blk = jnp.arange(n_slc)
    sel_off = jnp.arange(SEL_BLOCK)
    win_off = jnp.arange(WINDOW + Q_BLOCK) - WINDOW

    def attend(s, mask, dist, v, spec):
        s = s.astype(jnp.float32) * scale - slopes * dist
        probs = masked_softmax(s, mask)
        return probs, jnp.einsum(spec, probs.astype(v.dtype), v)

    def block(qi):
        q0 = qi * Q_BLOCK
        tq = q0 + jnp.arange(Q_BLOCK)
        qb = lax.dynamic_slice_in_dim(q, q0, Q_BLOCK, axis=1)
        qb = qb.reshape(b, Q_BLOCK, NSA_KV_HEADS, NSA_GROUP, HEAD_DIM).transpose(0, 2, 3, 1, 4)
        gb = jax.nn.sigmoid(lax.dynamic_slice_in_dim(gate_logits, q0, Q_BLOCK, axis=1).astype(jnp.float32))
        gb = gb.reshape(b, Q_BLOCK, NSA_KV_HEADS, NSA_GROUP, N_GATES).transpose(0, 2, 3, 1, 4)
        d_cmp = tq[:, None] - cmp_end[None, :]
        p_cmp, o_cmp = attend(jnp.einsum('bgrqd,bcgd->bgrqc', qb, k_cmp), d_cmp >= 0, d_cmp, v_cmp,
                              'bgrqc,bcgd->bgrqd')
        imp = jnp.einsum('bgrqc,cj->bgqj', p_cmp, overlap)
        cur = (tq // SEL_BLOCK)[:, None]
        forced = (blk == 0) | (blk == cur) | (blk == cur - 1)
        imp = jnp.where(forced, jnp.inf, jnp.where(blk > cur, -jnp.inf, imp))
        sel = lax.top_k(imp, n_sel)[1]
        k_g = k_sb[b_idx, g_idx, sel].reshape(b, NSA_KV_HEADS, Q_BLOCK, n_sel * SEL_BLOCK, HEAD_DIM)
        v_g = v_sb[b_idx, g_idx, sel].reshape(b, NSA_KV_HEADS, Q_BLOCK, n_sel * SEL_BLOCK, HEAD_DIM)
        pos = (sel[..., None] * SEL_BLOCK + sel_off).reshape(b, NSA_KV_HEADS, Q_BLOCK, n_sel * SEL_BLOCK)
        d_slc = (tq[:, None] - pos)[:, :, None]
        _, o_slc = attend(jnp.einsum('bgrqd,bgqkd->bgrqk', qb, k_g), d_slc >= 0, d_slc, v_g,
                          'bgrqk,bgqkd->bgrqd')
        kw = lax.dynamic_slice_in_dim(k_wp, q0, WINDOW + Q_BLOCK, axis=1)
        vw = lax.dynamic_slice_in_dim(v_wp, q0, WINDOW + Q_BLOCK, axis=1)
        kpos = q0 + win_off
        d_win = tq[:, None] - kpos[None, :]
        m_win = (d_win >= 0) & (d_win < WINDOW) & (kpos >= 0)[None, :]
        _, o_win = attend(jnp.einsum('bgrqd,bkgd->bgrqk', qb, kw), m_win, d_win, vw, 'bgrqk,bkgd->bgrqd')
        o = gb[..., 0:1] * o_cmp + gb[..., 1:2] * o_slc + gb[..., 2:3] * o_win
        return o.transpose(0, 3, 1, 2, 4).reshape(b, Q_BLOCK, NSA_WIDTH).astype(q.dtype)

    out = lax.map(block, jnp.arange(t_len // Q_BLOCK))
    return out.transpose(1, 0, 2, 3).reshape(b, t_len, NSA_WIDTH)


def rwkv_step(state, inp):
    r, w, k, v, a_vec, b_vec = inp
    state = (state * w[:, :, None, :]
             + jnp.einsum('bhvk,bhk->bhv', state, a_vec)[..., None] * b_vec[:, :, None, :]
             + v[..., None] * k[:, :, None, :])
    return state, jnp.einsum('bhvk,bhk->bhv', state, r)


def rwkv7_time_mix(z, mu, w0, w_up, a0, a_up, g_up, k_k, k_a, r_k, ln_g, ln_b):
    b, t_len, _ = z.shape
    f32 = jnp.float32
    z = z + (jnp.pad(z, ((0, 0), (1, 0), (0, 0)))[:, :-1] - z) * mu
    r, k, v, w_lo, a_lo, g_lo = jnp.split(z, RWKV_SPLIT, axis=-1)
    heads = lambda u: u.astype(f32).reshape(b, t_len, RWKV_HEADS, HEAD_DIM)
    w_raw = -jax.nn.softplus(-(w0 + jnp.tanh(w_lo) @ w_up).astype(f32)) - 0.5
    decay = jnp.exp(-jnp.exp(w_raw))
    a = jax.nn.sigmoid((a0 + a_lo @ a_up).astype(f32))
    g = jax.nn.sigmoid(g_lo) @ g_up
    kk = heads(k * k_k)
    kk = kk / jnp.maximum(jnp.linalg.norm(kk, axis=-1, keepdims=True), 1e-12)
    k = k.astype(f32) * (1.0 + (a - 1.0) * k_a)
    r_h, w_h, k_h, v_h, a_h = heads(r), heads(decay), heads(k), heads(v), heads(a)
    seq_major = lambda u: jnp.swapaxes(u, 0, 1)
    state0 = jnp.zeros((b, RWKV_HEADS, HEAD_DIM, HEAD_DIM), f32)
    _, y = lax.scan(rwkv_step, state0, (seq_major(r_h), seq_major(w_h), seq_major(k_h), seq_major(v_h),
                                         seq_major(-kk), seq_major(kk * a_h)))
    y = jnp.swapaxes(y, 0, 1)
    mean = y.mean(-1, keepdims=True)
    var = jnp.square(y - mean).mean(-1, keepdims=True)
    y = ((y - mean) * lax.rsqrt(var + GN_EPS)).reshape(b, t_len, RWKV_WIDTH) * ln_g + ln_b
    bonus = (jnp.sum(r_h * k_h * r_k, axis=-1, keepdims=True) * v_h).reshape(b, t_len, RWKV_WIDTH)
    return ((y + bonus) * g).astype(z.dtype)


def clamped_swiglu(gate, up):
    gate = jnp.minimum(gate, SWIGLU_LIMIT)
    up = jnp.clip(up, -SWIGLU_LIMIT, SWIGLU_LIMIT)
    return gate * jax.nn.sigmoid(SWIGLU_ALPHA * gate) * (up + 1.0)


def moe_ffn(x, router_w, router_b, w_gate, b_gate, w_up, b_up, w_down, b_down):
    b, t_len, d = x.shape
    xt = x.reshape(-1, d)
    n_tok = xt.shape[0]
    logits = (xt @ router_w + router_b).astype(jnp.float32)
    top_logit, top_idx = lax.top_k(logits, TOP_K)
    top_w = jax.nn.softmax(top_logit, axis=-1)
    n_assign = n_tok * TOP_K
    flat_e = top_idx.reshape(-1)
    order = jnp.argsort(flat_e)
    sorted_e = flat_e[order]
    sorted_tok = order // TOP_K
    sorted_w = top_w.reshape(-1)[order]
    counts = jnp.bincount(flat_e, length=N_EXPERTS)
    padded = (counts + MOE_BLOCK - 1) // MOE_BLOCK * MOE_BLOCK
    start_sorted = jnp.cumsum(counts) - counts
    end_padded = jnp.cumsum(padded)
    start_padded = end_padded - padded
    dest = start_padded[sorted_e] + jnp.arange(n_assign) - start_sorted[sorted_e]
    n_blocks = -(-(n_assign + N_EXPERTS * (MOE_BLOCK - 1)) // MOE_BLOCK)
    n_rows = n_blocks * MOE_BLOCK
    row_tok = jnp.zeros((n_rows,), jnp.int32).at[dest].set(sorted_tok)
    row_w = jnp.zeros((n_rows,), jnp.float32).at[dest].set(sorted_w)
    block_e = jnp.minimum(jnp.searchsorted(end_padded, jnp.arange(n_blocks) * MOE_BLOCK, side='right'),
                          N_EXPERTS - 1)

    def expert_block(args):
        tok, e = args
        xb = xt[tok]
        h = clamped_swiglu(xb @ w_gate[e] + b_gate[e], xb @ w_up[e] + b_up[e])
        return h @ w_down[e] + b_down[e]

    out = lax.map(expert_block, (row_tok.reshape(n_blocks, MOE_BLOCK), block_e))
    y = jnp.zeros((n_tok, d), jnp.float32).at[row_tok].add(out.reshape(n_rows, d).astype(jnp.float32) * row_w[:, None])
    return y.astype(x.dtype).reshape(b, t_len, d)


def setup_inputs(seed: int = 0) -> dict:
    key = jax.random.key(seed)
    keys = iter(jax.random.split(key, 48))
    f32 = jnp.float32
    nrm = lambda shape, s: jax.random.normal(next(keys), shape, f32) * s
    L, D, F, E = DEPTH, D_MODEL, D_EXPERT, N_EXPERTS
    kin = CMP_BLOCK * HEAD_DIM
    return {
        "x": nrm((BATCH, SEQ, D), 1.0),
        "p": nrm((L, BATCH, SEQ, PLE_DIM), 1.0),
        "w_in": nrm((L, D, IN_COLS), D ** -0.5),
        "cmp_pe_k": nrm((L, CMP_BLOCK, HEAD_DIM), 0.5),
        "cmp_w1_k": nrm((L, kin, HEAD_DIM), kin ** -0.5),
        "cmp_w2_k": nrm((L, HEAD_DIM, HEAD_DIM), HEAD_DIM ** -0.5),
        "cmp_pe_v": nrm((L, CMP_BLOCK, HEAD_DIM), 0.5),
        "cmp_w1_v": nrm((L, kin, HEAD_DIM), kin ** -0.5),
        "cmp_w2_v": nrm((L, HEAD_DIM, HEAD_DIM), HEAD_DIM ** -0.5),
        "rwkv_mu": jax.random.uniform(next(keys), (L, RWKV_COLS), f32),
        "rwkv_w0": -1.0 + nrm((L, RWKV_WIDTH), 0.5),
        "rwkv_w_up": nrm((L, DECAY_LORA, RWKV_WIDTH), 0.5 * DECAY_LORA ** -0.5),
        "rwkv_a0": nrm((L, RWKV_WIDTH), 0.5),
        "rwkv_a_up": nrm((L, ICLR_LORA, RWKV_WIDTH), 0.5 * ICLR_LORA ** -0.5),
        "rwkv_g_up": nrm((L, GATE_LORA, RWKV_WIDTH), GATE_LORA ** -0.5),
        "rwkv_k_k": 0.85 + nrm((L, RWKV_WIDTH), 0.05),
        "rwkv_k_a": 1.0 + nrm((L, RWKV_WIDTH), 0.05),
        "rwkv_r_k": nrm((L, RWKV_HEADS, HEAD_DIM), 0.1),
        "rwkv_ln_g": 1.0 + nrm((L, RWKV_WIDTH), 0.05),
        "rwkv_ln_b": nrm((L, RWKV_WIDTH), 0.01),
        "w_out": nrm((L, MIX_WIDTH, D), MIX_WIDTH ** -0.5 * DEEPNORM_BETA),
        "ln1_g": 1.0 + nrm((L, D), 0.05),
        "ln1_b": nrm((L, D), 0.01),
        "router_w": nrm((L, D, E), D ** -0.5),
        "router_b": nrm((L, E), 0.01),
        "exp_w_gate": nrm((L, E, D, F), D ** -0.5),
        "exp_b_gate": nrm((L, E, F), 0.01),
        "exp_w_up": nrm((L, E, D, F), D ** -0.5),
        "exp_b_up": nrm((L, E, F), 0.01),
        "exp_w_down": nrm((L, E, F, D), F ** -0.5 * DEEPNORM_BETA),
        "exp_b_down": nrm((L, E, D), 0.01),
        "ln2_g": 1.0 + nrm((L, D), 0.05),
        "ln2_b": nrm((L, D), 0.01),
        "ple_w": nrm((L, PLE_DIM, D), PLE_DIM ** -0.5),
        "ple_gate_w": nrm((L, D, D), D ** -0.5),
    }


def reference(x, p, w_in, cmp_pe_k, cmp_w1_k, cmp_w2_k, cmp_pe_v, cmp_w1_v, cmp_w2_v,
              rwkv_mu, rwkv_w0, rwkv_w_up, rwkv_a0, rwkv_a_up, rwkv_g_up, rwkv_k_k, rwkv_k_a, rwkv_r_k,
              rwkv_ln_g, rwkv_ln_b, w_out, ln1_g, ln1_b, router_w, router_b,
              exp_w_gate, exp_b_gate, exp_w_up, exp_b_up, exp_w_down, exp_b_down,
              ln2_g, ln2_b, ple_w, ple_gate_w):
    b, t_len, _ = x.shape
    kv_heads = lambda u: u.reshape(b, t_len, NSA_KV_HEADS, HEAD_DIM)
    h = x
    for i in range(DEPTH):
        z = h @ w_in[i]
        z_nsa, z_rwkv = z[..., :NSA_COLS], z[..., NSA_COLS:]
        q, kc, vc, ks, vs, kw, vw, gl = jnp.split(z_nsa, NSA_SPLIT, axis=-1)
        k_cmp = compress_kv(kv_heads(kc), cmp_pe_k[i], cmp_w1_k[i], cmp_w2_k[i])
        v_cmp = compress_kv(kv_heads(vc), cmp_pe_v[i], cmp_w1_v[i], cmp_w2_v[i])
        o_nsa = nsa_attention(q.reshape(b, t_len, NSA_HEADS, HEAD_DIM), k_cmp, v_cmp,
                              kv_heads(ks), kv_heads(vs), kv_heads(kw), kv_heads(vw),
                              gl.reshape(b, t_len, NSA_HEADS, N_GATES))
        o_rwkv = rwkv7_time_mix(z_rwkv, rwkv_mu[i], rwkv_w0[i], rwkv_w_up[i], rwkv_a0[i], rwkv_a_up[i],
                                rwkv_g_up[i], rwkv_k_k[i], rwkv_k_a[i], rwkv_r_k[i], rwkv_ln_g[i], rwkv_ln_b[i])
        mix = jnp.concatenate([o_nsa, o_rwkv], axis=-1) @ w_out[i]
        h = layer_norm(DEEPNORM_ALPHA * h + mix, ln1_g[i], ln1_b[i])
        ffn = moe_ffn(h, router_w[i], router_b[i], exp_w_gate[i], exp_b_gate[i], exp_w_up[i], exp_b_up[i],
                      exp_w_down[i], exp_b_down[i])
        h = layer_norm(DEEPNORM_ALPHA * h + ffn, ln2_g[i], ln2_b[i])
        h = h + jax.nn.sigmoid(h @ ple_gate_w[i]) * (p[i] @ ple_w[i])
    return h
```

```python
import functools

import numpy as np
import jax
import jax.numpy as jnp
from jax import lax
from jax.experimental import pallas as pl
from jax.experimental.pallas import tpu as pltpu

F32 = jnp.float32
BF16 = jnp.bfloat16
HIGHEST = lax.Precision.HIGHEST

LANE = 128
D_MODEL = 2048
HEAD_DIM = 64
NSA_HEADS = 16
NSA_KV_HEADS = 4
NSA_GROUP = NSA_HEADS // NSA_KV_HEADS
NSA_WIDTH = NSA_HEADS * HEAD_DIM
NSA_KV_WIDTH = NSA_KV_HEADS * HEAD_DIM
CMP_BLOCK = 32
CMP_STRIDE = 16
SEL_BLOCK = 64
N_SELECT = 16
WINDOW = 512
N_GATES = 3
Q_BLOCK = 64
RWKV_HEADS = 16
RWKV_WIDTH = RWKV_HEADS * HEAD_DIM
DECAY_LORA = 96
ICLR_LORA = 96
GATE_LORA = 256
GN_EPS = 64e-5
N_EXPERTS = 32
TOP_K = 4
SWIGLU_ALPHA = 1.702
SWIGLU_LIMIT = 7.0
PLE_DIM = 256
LN_EPS = 1e-5
NEG_INF = -1e30

NSA_COLS = NSA_WIDTH + 6 * NSA_KV_WIDTH + NSA_HEADS * N_GATES
RWKV_COLS = 3 * RWKV_WIDTH + DECAY_LORA + ICLR_LORA + GATE_LORA

RB_R = 0
RB_K = RB_R + RWKV_WIDTH // LANE
RB_V = RB_K + RWKV_WIDTH // LANE
RB_WLO = RB_V + RWKV_WIDTH // LANE
RB_ALO = RB_WLO + 1
RB_GLO = RB_ALO + 1
RWKV_BLOCKS = RB_GLO + GATE_LORA // LANE
ZB_RWKV = 0
ZB_Q = ZB_RWKV + RWKV_BLOCKS
ZB_KVC = ZB_Q + NSA_WIDTH // LANE
ZB_KVS = ZB_KVC + NSA_KV_HEADS
ZB_KVW = ZB_KVS + NSA_KV_HEADS
ZB_GATE = ZB_KVW + NSA_KV_HEADS
Z_BLOCKS = ZB_GATE + NSA_KV_HEADS
Z_COLS = Z_BLOCKS * LANE
assert (ZB_Q * LANE) % (NSA_GROUP * HEAD_DIM) == 0

VMEM_LIMIT = 56 * 1024 * 1024


def _z_source_columns():
    src = np.full((Z_COLS,), -1, np.int64)
    src[ZB_Q * LANE:ZB_Q * LANE + NSA_WIDTH] = np.arange(NSA_WIDTH)
    for branch in range(3):
        k0 = NSA_WIDTH + 2 * branch * NSA_KV_WIDTH
        v0 = k0 + NSA_KV_WIDTH
        for g in range(NSA_KV_HEADS):
            base = (ZB_KVC + branch * NSA_KV_HEADS + g) * LANE
            src[base:base + HEAD_DIM] = k0 + g * HEAD_DIM + np.arange(HEAD_DIM)
            src[base + HEAD_DIM:base + 2 * HEAD_DIM] = v0 + g * HEAD_DIM + np.arange(HEAD_DIM)
    g0 = NSA_WIDTH + 6 * NSA_KV_WIDTH
    for g in range(NSA_KV_HEADS):
        base = (ZB_GATE + g) * LANE
        for i in range(N_GATES):
            for r in range(NSA_GROUP):
                src[base + i * NSA_GROUP + r] = g0 + (g * NSA_GROUP + r) * N_GATES + i
    rwkv = _rwkv_source_columns()
    src[ZB_RWKV * LANE:(ZB_RWKV + RWKV_BLOCKS) * LANE] = np.where(rwkv >= 0, NSA_COLS + rwkv, -1)
    return src


def _rwkv_source_columns():
    src = np.full((RWKV_BLOCKS * LANE,), -1, np.int64)
    src[:3 * RWKV_WIDTH] = np.arange(3 * RWKV_WIDTH)
    src[RB_WLO * LANE:RB_WLO * LANE + DECAY_LORA] = 3 * RWKV_WIDTH + np.arange(DECAY_LORA)
    src[RB_ALO * LANE:RB_ALO * LANE + ICLR_LORA] = 3 * RWKV_WIDTH + DECAY_LORA + np.arange(ICLR_LORA)
    src[RB_GLO * LANE:] = 3 * RWKV_WIDTH + DECAY_LORA + ICLR_LORA + np.arange(GATE_LORA)
    return src


def _take_columns(w, src):
    cols = jnp.take(w, jnp.asarray(np.maximum(src, 0), jnp.int32), axis=-1)
    return jnp.where(jnp.asarray(src >= 0), cols, jnp.zeros((), w.dtype))


def _pad_rows(w, rows):
    return jnp.pad(w, ((0, rows - w.shape[0]), (0, 0)))


def _in_proj_kernel(x_ref, w_ref, z_ref, xb_ref):
    @pl.when(pl.program_id(1) == 0)
    def _():
        xb_ref[...] = x_ref[...].astype(BF16)

    z_ref[...] = jnp.dot(xb_ref[...], w_ref[...], preferred_element_type=F32).astype(z_ref.dtype)


def _in_proj(x2, w_bf, tm=1024, tn=512):
    n_tok, d = x2.shape
    n_cols = w_bf.shape[1]
    return pl.pallas_call(
        _in_proj_kernel,
        out_shape=jax.ShapeDtypeStruct((n_tok, n_cols), BF16),
        grid=(n_tok // tm, n_cols // tn),
        in_specs=[pl.BlockSpec((tm, d), lambda i, j: (i, 0)),
                  pl.BlockSpec((d, tn), lambda i, j: (0, j))],
        out_specs=pl.BlockSpec((tm, tn), lambda i, j: (i, j)),
        scratch_shapes=[pltpu.VMEM((tm, d), BF16)],
        compiler_params=pltpu.CompilerParams(dimension_semantics=("parallel", "arbitrary"),
                                             vmem_limit_bytes=VMEM_LIMIT),
        name="in_proj",
    )(x2, w_bf)


def _compress_kernel(kv_ref, pe_ref, w1_ref, w2_ref, out_ref, kv32_ref):
    n_chunks = kv_ref.shape[1] // CMP_STRIDE
    kv32_ref[...] = kv_ref[0].astype(F32)
    acc_lo = jnp.zeros((n_chunks, LANE), F32)
    acc_hi = jnp.zeros((n_chunks, LANE), F32)
    for i in range(CMP_STRIDE):
        rows = kv32_ref[pl.ds(i, n_chunks, stride=CMP_STRIDE), :]
        lo = (rows + pe_ref[i:i + 1, :]).astype(BF16)
        hi = (rows + pe_ref[CMP_STRIDE + i:CMP_STRIDE + i + 1, :]).astype(BF16)
        acc_lo += jnp.dot(lo, w1_ref[i], preferred_element_type=F32)
        acc_hi += jnp.dot(hi, w1_ref[CMP_STRIDE + i], preferred_element_type=F32)
    shifted = jnp.concatenate([acc_hi[1:], jnp.zeros((1, LANE), F32)], axis=0)
    hid = jax.nn.gelu(acc_lo + shifted)
    out = jnp.dot(hid.astype(BF16), w2_ref[...], preferred_element_type=F32)
    row = lax.broadcasted_iota(jnp.int32, out.shape, 0)
    out_ref[0, 0] = jnp.where(row < n_chunks - 1, out, 0.0).astype(out_ref.dtype)


def _compress(z3, pe, w1, w2):
    b, t_len, _ = z3.shape
    n_chunks = t_len // CMP_STRIDE
    return pl.pallas_call(
        _compress_kernel,
        out_shape=jax.ShapeDtypeStruct((b, NSA_KV_HEADS, n_chunks, LANE), BF16),
        grid=(b, NSA_KV_HEADS),
        in_specs=[pl.BlockSpec((1, t_len, LANE), lambda bi, g: (bi, 0, ZB_KVC + g)),
                  pl.BlockSpec((CMP_BLOCK, LANE), lambda bi, g: (0, 0)),
                  pl.BlockSpec((CMP_BLOCK, LANE, LANE), lambda bi, g: (0, 0, 0)),
                  pl.BlockSpec((LANE, LANE), lambda bi, g: (0, 0))],
        out_specs=pl.BlockSpec((1, 1, n_chunks, LANE), lambda bi, g: (bi, g, 0, 0)),
        scratch_shapes=[pltpu.VMEM((t_len, LANE), F32)],
        compiler_params=pltpu.CompilerParams(dimension_semantics=("parallel", "parallel"),
                                             vmem_limit_bytes=VMEM_LIMIT),
        name="kv_compress",
    )(z3, pe, w1, w2)


def _block_diag2(a, b):
    za = jnp.zeros(a.shape[:-1] + (b.shape[-1],), a.dtype)
    zb = jnp.zeros(b.shape[:-1] + (a.shape[-1],), b.dtype)
    return jnp.concatenate([jnp.concatenate([a, za], axis=-1), jnp.concatenate([zb, b], axis=-1)], axis=-2)


def _stack_heads(x):
    return jnp.concatenate([x[:, r * HEAD_DIM:(r + 1) * HEAD_DIM] for r in range(NSA_GROUP)], axis=0)


def _dot_nt(a, b):
    return lax.dot_general(a, b, (((1,), (1,)), ((), ())), preferred_element_type=F32)


def _nsa_kernel(slopes_ref, q_ref, kvc_ref, kvs_ref, kvw_ref, gate_ref, ovl_ref, exp_ref, o_ref, sel_ref,
                *, n_sel):
    g = pl.program_id(1)
    qi = pl.program_id(2)
    q0 = qi * Q_BLOCK
    rows = NSA_GROUP * Q_BLOCK
    n_cmp_pad = kvc_ref.shape[2]
    n_slc = exp_ref.shape[0]
    t_len = exp_ref.shape[1]

    q = q_ref[0]
    qs = _stack_heads(q) * jnp.asarray(HEAD_DIM ** -0.5, BF16)
    row_i = lax.broadcasted_iota(jnp.int32, (rows, 1), 0)
    tq_col = q0 + row_i % Q_BLOCK
    slope_col = jnp.concatenate(
        [jnp.full((Q_BLOCK, 1), slopes_ref[g * NSA_GROUP + r], F32) for r in range(NSA_GROUP)], axis=0)

    kc = kvc_ref[0, 0, :, 0:HEAD_DIM]
    vc = kvc_ref[0, 0, :, HEAD_DIM:2 * HEAD_DIM]
    s = _dot_nt(qs, kc)
    cmp_end = lax.broadcasted_iota(jnp.int32, (1, n_cmp_pad), 1) * CMP_STRIDE + (CMP_BLOCK - 1)
    d_cmp = tq_col - cmp_end
    m_cmp = d_cmp >= 0
    s = jnp.where(m_cmp, s - slope_col * d_cmp.astype(F32), NEG_INF)
    e = jnp.where(m_cmp, jnp.exp(s - jnp.max(s, axis=-1, keepdims=True)), 0.0)
    p_cmp = e / jnp.maximum(jnp.sum(e, axis=-1, keepdims=True), 1e-30)
    o_cmp = jnp.dot(p_cmp.astype(BF16), vc, preferred_element_type=F32)

    p_sum = p_cmp[0:Q_BLOCK]
    for r in range(1, NSA_GROUP):
        p_sum = p_sum + p_cmp[r * Q_BLOCK:(r + 1) * Q_BLOCK]
    imp = jnp.dot(p_sum, ovl_ref[...], preferred_element_type=F32, precision=HIGHEST)
    blk = lax.broadcasted_iota(jnp.int32, (Q_BLOCK, n_slc), 1)
    forced = (blk == 0) | (blk == qi) | (blk == qi - 1)
    imp = jnp.where(forced, jnp.inf, jnp.where(blk > qi, -jnp.inf, imp))
    rank = jnp.zeros((Q_BLOCK, n_slc), F32)
    for j in range(n_slc):
        col = imp[:, j:j + 1]
        ahead = (col > imp) | ((col == imp) & (blk > j))
        rank = rank + jnp.where(ahead, 1.0, 0.0)
    sel = jnp.where((rank < n_sel) & (blk <= qi), 1.0, 0.0).astype(BF16)
    sel_keys = jnp.dot(sel, exp_ref[...], preferred_element_type=F32)
    kpos_all = lax.broadcasted_iota(jnp.int32, (Q_BLOCK, t_len), 1)
    tq_q = q0 + lax.broadcasted_iota(jnp.int32, (Q_BLOCK, 1), 0)
    sel_keys = jnp.where(kpos_all <= tq_q, sel_keys, 0.0)
    for c in range(t_len // LANE):
        sel_ref[c] = sel_keys[:, c * LANE:(c + 1) * LANE]

    lane_i = lax.broadcasted_iota(jnp.int32, (rows, LANE), 1)

    def attend_chunk(c, carry, kv_ref, mask):
        m_prev, l_prev, acc = carry
        k = kv_ref[0, pl.ds(c * LANE, LANE), 0:HEAD_DIM]
        v = kv_ref[0, pl.ds(c * LANE, LANE), HEAD_DIM:2 * HEAD_DIM]
        dist = (tq_col - c * LANE - lane_i).astype(F32)
        sc = jnp.where(mask, _dot_nt(qs, k) - slope_col * dist, NEG_INF)
        m_new = jnp.maximum(m_prev, jnp.max(sc, axis=-1, keepdims=True))
        p = jnp.where(mask, jnp.exp(sc - m_new), 0.0)
        alpha = jnp.exp(m_prev - m_new)
        l_new = alpha * l_prev + jnp.sum(p, axis=-1, keepdims=True)
        acc = alpha * acc + jnp.dot(p.astype(BF16), v, preferred_element_type=F32)
        return m_new, l_new, acc

    init = (jnp.full((rows, 1), NEG_INF, F32), jnp.zeros((rows, 1), F32), jnp.zeros((rows, HEAD_DIM), F32))

    def slc_body(c, carry):
        m1 = sel_ref[c] > 0.5
        mask = jnp.concatenate([m1] * NSA_GROUP, axis=0)
        return attend_chunk(c, carry, kvs_ref, mask)

    _, l_slc, acc_slc = lax.fori_loop(0, (qi + 2) // 2, slc_body, init)
    o_slc = acc_slc / jnp.maximum(l_slc, 1e-30)

    def win_body(c, carry):
        kpos = c * LANE + lax.broadcasted_iota(jnp.int32, (Q_BLOCK, LANE), 1)
        d1 = tq_q - kpos
        m1 = (d1 >= 0) & (d1 < WINDOW)
        mask = jnp.concatenate([m1] * NSA_GROUP, axis=0)
        return attend_chunk(c, carry, kvw_ref, mask)

    c_hi = (q0 + Q_BLOCK - 1) // LANE
    c_lo = jnp.maximum(c_hi - WINDOW // LANE, 0)
    _, l_win, acc_win = lax.fori_loop(c_lo, c_hi + 1, win_body, init)
    o_win = acc_win / jnp.maximum(l_win, 1e-30)

    gates = jax.nn.sigmoid(gate_ref[0].astype(F32))

    def gate_col(i):
        return jnp.concatenate([gates[:, i * NSA_GROUP + r:i * NSA_GROUP + r + 1] for r in range(NSA_GROUP)],
                               axis=0)

    o = gate_col(0) * o_cmp + gate_col(1) * o_slc + gate_col(2) * o_win
    o_ref[0] = jnp.concatenate([o[r * Q_BLOCK:(r + 1) * Q_BLOCK] for r in range(NSA_GROUP)],
                               axis=1).astype(o_ref.dtype)


def _nsa_attention(z3, kv_cmp):
    b, t_len, _ = z3.shape
    n_q = t_len // Q_BLOCK
    n_slc = t_len // SEL_BLOCK
    n_cmp_pad = kv_cmp.shape[2]
    n_sel = min(N_SELECT, n_slc)
    slopes = jnp.exp2(-8.0 * jnp.arange(1, NSA_HEADS + 1, dtype=F32) / NSA_HEADS)
    c_idx = np.arange(n_cmp_pad)
    cmp_start = c_idx * CMP_STRIDE
    cmp_end = cmp_start + CMP_BLOCK - 1
    slc_start = np.arange(n_slc) * SEL_BLOCK
    overlap = np.clip(np.minimum(cmp_end[:, None], slc_start[None, :] + SEL_BLOCK - 1)
                      - np.maximum(cmp_start[:, None], slc_start[None, :]) + 1, 0, None).astype(np.float32)
    overlap[c_idx >= t_len // CMP_STRIDE - CMP_BLOCK // CMP_STRIDE + 1] = 0.0
    expand = (np.arange(t_len)[None, :] // SEL_BLOCK == np.arange(n_slc)[:, None]).astype(np.float32)
    kernel = functools.partial(_nsa_kernel, n_sel=n_sel)
    return pl.pallas_call(
        kernel,
        out_shape=jax.ShapeDtypeStruct((b, t_len, NSA_WIDTH), BF16),
        grid=(b, NSA_KV_HEADS, n_q),
        in_specs=[
            pl.BlockSpec(memory_space=pltpu.SMEM),
            pl.BlockSpec((1, Q_BLOCK, NSA_GROUP * HEAD_DIM),
                         lambda bi, g, qi: (bi, qi, ZB_Q * LANE // (NSA_GROUP * HEAD_DIM) + g)),
            pl.BlockSpec((1, 1, n_cmp_pad, LANE), lambda bi, g, qi: (bi, g, 0, 0)),
            pl.BlockSpec((1, t_len, LANE), lambda bi, g, qi: (bi, 0, ZB_KVS + g)),
            pl.BlockSpec((1, t_len, LANE), lambda bi, g, qi: (bi, 0, ZB_KVW + g)),
            pl.BlockSpec((1, Q_BLOCK, LANE), lambda bi, g, qi: (bi, qi, ZB_GATE + g)),
            pl.BlockSpec((n_cmp_pad, n_slc), lambda bi, g, qi: (0, 0)),
            pl.BlockSpec((n_slc, t_len), lambda bi, g, qi: (0, 0)),
        ],
        out_specs=pl.BlockSpec((1, Q_BLOCK, NSA_GROUP * HEAD_DIM), lambda bi, g, qi: (bi, qi, g)),
        scratch_shapes=[pltpu.VMEM((t_len // LANE, Q_BLOCK, LANE), F32)],
        compiler_params=pltpu.CompilerParams(dimension_semantics=("parallel", "parallel", "arbitrary"),
                                             vmem_limit_bytes=VMEM_LIMIT),
        name="nsa_attention",
    )(slopes, z3, kv_cmp, z3, z3, z3, jnp.asarray(overlap), jnp.asarray(expand, BF16))


def _nsa_prepare(cmp_pe_k, cmp_w1_k, cmp_w2_k, cmp_pe_v, cmp_w1_v, cmp_w2_v):
    pe = jnp.concatenate([cmp_pe_k, cmp_pe_v], axis=-1)
    w1 = _block_diag2(cmp_w1_k.reshape(CMP_BLOCK, HEAD_DIM, HEAD_DIM),
                      cmp_w1_v.reshape(CMP_BLOCK, HEAD_DIM, HEAD_DIM)).astype(BF16)
    w2 = _block_diag2(cmp_w2_k, cmp_w2_v).astype(BF16)
    return pe, w1, w2


RWKV_CHUNK = 64


def _rwkv_prep_kernel(z_ref, mu_ref, w0_ref, wup_ref, a0_ref, aup_ref, gup_ref, kk_ref, ka_ref,
                      r_ref, k_ref, v_ref, kkr_ref, a_ref, lw_ref, g_ref, carry_ref):
    w = RWKV_WIDTH

    @pl.when(pl.program_id(1) == 0)
    def _():
        carry_ref[...] = jnp.zeros_like(carry_ref)

    z = z_ref[0].astype(F32)
    tc = z.shape[0]
    row = lax.broadcasted_iota(jnp.int32, z.shape, 0)
    prev = jnp.where(row == 0, carry_ref[0:1, :], pltpu.roll(z, 1, 0))
    carry_ref[0:1, :] = z[tc - 1:tc, :]
    zs = z + (prev - z) * mu_ref[...]
    r = zs[:, RB_R * LANE:RB_R * LANE + w]
    k = zs[:, RB_K * LANE:RB_K * LANE + w]
    v = zs[:, RB_V * LANE:RB_V * LANE + w]
    w_lo = zs[:, RB_WLO * LANE:(RB_WLO + 1) * LANE]
    a_lo = zs[:, RB_ALO * LANE:(RB_ALO + 1) * LANE]
    g_lo = zs[:, RB_GLO * LANE:RB_GLO * LANE + GATE_LORA]
    d = w0_ref[...] + jnp.dot(jnp.tanh(w_lo).astype(BF16), wup_ref[...], preferred_element_type=F32)
    w_raw = -jax.nn.softplus(-d) - 0.5
    lw_ref[0] = -jnp.exp(w_raw)
    a = jax.nn.sigmoid(a0_ref[...] + jnp.dot(a_lo.astype(BF16), aup_ref[...], preferred_element_type=F32))
    g = jnp.dot(jax.nn.sigmoid(g_lo).astype(BF16), gup_ref[...], preferred_element_type=F32)
    r_ref[0] = r.astype(r_ref.dtype)
    v_ref[0] = v.astype(v_ref.dtype)
    kkr_ref[0] = (k * kk_ref[...]).astype(kkr_ref.dtype)
    k_ref[0] = (k * (1.0 + (a - 1.0) * ka_ref[...])).astype(k_ref.dtype)
    a_ref[0] = a.astype(a_ref.dtype)
    g_ref[0] = g.astype(g_ref.dtype)


def _rwkv_prep(z3, mu, w0, w_up, a0, a_up, g_up, k_k, k_a, tc=256):
    b, t_len, _ = z3.shape
    w = RWKV_WIDTH
    ncol = RWKV_BLOCKS * LANE
    vec = lambda n: pl.BlockSpec((1, n), lambda bi, ti: (0, 0))
    mat = lambda m, n: pl.BlockSpec((m, n), lambda bi, ti: (0, 0))
    out_bf = jax.ShapeDtypeStruct((b, t_len, w), BF16)
    out_f32 = jax.ShapeDtypeStruct((b, t_len, w), F32)
    out_spec = pl.BlockSpec((1, tc, w), lambda bi, ti: (bi, ti, 0))
    return pl.pallas_call(
        _rwkv_prep_kernel,
        out_shape=(out_bf, out_bf, out_bf, out_bf, out_bf, out_f32, out_bf),
        grid=(b, t_len // tc),
        in_specs=[pl.BlockSpec((1, tc, ncol), lambda bi, ti: (bi, ti, ZB_RWKV)),
                  vec(ncol), vec(w), mat(LANE, w), vec(w), mat(LANE, w), mat(GATE_LORA, w), vec(w), vec(w)],
        out_specs=(out_spec,) * 7,
        scratch_shapes=[pltpu.VMEM((8, ncol), F32)],
        compiler_params=pltpu.CompilerParams(dimension_semantics=("parallel", "arbitrary"),
                                             vmem_limit_bytes=VMEM_LIMIT),
        name="rwkv_prep",
    )(z3, mu, w0, w_up, a0, a_up, g_up, k_k, k_a)


def _pair_blocks(x):
    lane = lax.broadcasted_iota(jnp.int32, x.shape, 1)
    zero = jnp.zeros((), x.dtype)
    return jnp.concatenate([jnp.where(lane < HEAD_DIM, x, zero), jnp.where(lane >= HEAD_DIM, x, zero)], axis=0)


def _fold_pair(x):
    n = x.shape[0] // 2
    return x[:n] + x[n:]


def _rwkv_scan_kernel(r_ref, k_ref, v_ref, kkr_ref, a_ref, lw_ref, g_ref, rk_ref, lng_ref, lnb_ref,
                      o_ref, s_ref, *, n_chunks):
    L = RWKV_CHUNK
    L2 = 2 * L

    @pl.when(pl.program_id(2) == 0)
    def _():
        s_ref[...] = jnp.zeros_like(s_ref)

    ri = lax.broadcasted_iota(jnp.int32, (L, L), 0)
    ci = lax.broadcasted_iota(jnp.int32, (L, L), 1)
    tri_incl = jnp.where(ri >= ci, 1.0, 0.0).astype(F32)
    r2 = lax.broadcasted_iota(jnp.int32, (L2, L2), 0)
    c2 = lax.broadcasted_iota(jnp.int32, (L2, L2), 1)
    same_head = (r2 // L) == (c2 // L)
    strict2 = same_head & (r2 > c2)
    incl2 = same_head & (r2 >= c2)
    eye2 = jnp.where(r2 == c2, 1.0, 0.0).astype(F32)
    h_r = lax.broadcasted_iota(jnp.int32, (LANE, LANE), 0) // HEAD_DIM
    h_c = lax.broadcasted_iota(jnp.int32, (LANE, LANE), 1) // HEAD_DIM
    head_mask = h_r == h_c
    head_ones = jnp.where(head_mask, 1.0, 0.0).astype(F32)

    def head_sum(x):
        return jnp.dot(x, head_ones, preferred_element_type=F32, precision=HIGHEST)

    def mm(a, b):
        return jnp.dot(a.astype(BF16), b.astype(BF16), preferred_element_type=F32)

    for c in range(n_chunks):
        sl = pl.ds(c * L, L)
        r = r_ref[0, sl, :].astype(F32)
        k = k_ref[0, sl, :].astype(F32)
        v = v_ref[0, sl, :].astype(F32)
        kkr = kkr_ref[0, sl, :].astype(F32)
        a = a_ref[0, sl, :].astype(F32)
        lw = lw_ref[0, sl, :]
        kk = kkr / jnp.maximum(jnp.sqrt(head_sum(kkr * kkr)), 1e-12)
        bvec = kk * a
        cl = jnp.dot(tri_incl, lw, preferred_element_type=F32, precision=HIGHEST)
        p_incl = jnp.exp(cl)
        p_inv = jnp.exp(-cl)
        rt = r * p_incl
        at = -kk * jnp.exp(cl - lw)
        kt = k * p_inv
        bt = bvec * p_inv
        lhs = jnp.concatenate([_pair_blocks(at), _pair_blocks(rt)], axis=0).astype(BF16)
        rhs = jnp.concatenate([_pair_blocks(bt), _pair_blocks(kt)], axis=0).astype(BF16)
        gram = _dot_nt(lhs, rhs)
        a_ab = jnp.where(strict2, gram[:L2, :L2], 0.0)
        a_ak = jnp.where(strict2, gram[:L2, L2:], 0.0)
        a_rb = jnp.where(incl2, gram[L2:, :L2], 0.0)
        a_rk = jnp.where(incl2, gram[L2:, L2:], 0.0)
        tinv = eye2 + a_ab
        apow = a_ab
        for _ in range(int(np.log2(L)) - 1):
            apow = mm(apow, apow)
            tinv = tinv + mm(tinv, apow)
        v2 = _pair_blocks(v)
        s = s_ref[...]
        m = _dot_nt(at.astype(BF16), s.astype(BF16)) + _fold_pair(mm(a_ak, v2))
        u = _fold_pair(mm(tinv, _pair_blocks(m)))
        y = (_dot_nt(rt.astype(BF16), s.astype(BF16)) + _fold_pair(mm(a_rb, _pair_blocks(u)))
             + _fold_pair(mm(a_rk, v2)))
        p_last = p_incl[L - 1:L, :]
        upd = lax.dot_general(jnp.concatenate([u, v], axis=0).astype(BF16),
                              jnp.concatenate([bt * p_last, kt * p_last], axis=0).astype(BF16),
                              (((0,), (0,)), ((), ())), preferred_element_type=F32)
        s_ref[...] = s * p_last + jnp.where(head_mask, upd, 0.0)
        mean = head_sum(y) * (1.0 / HEAD_DIM)
        yc = y - mean
        var = head_sum(yc * yc) * (1.0 / HEAD_DIM)
        yn = yc * lax.rsqrt(var + GN_EPS) * lng_ref[...] + lnb_ref[...]
        bonus = head_sum(r * k * rk_ref[...]) * v
        o_ref[0, sl, :] = ((yn + bonus) * g_ref[0, sl, :].astype(F32)).astype(o_ref.dtype)


def _rwkv_scan(r, k, v, kkr, a, lw, g, r_k, ln_g, ln_b, tt=256):
    b, t_len, w = r.shape
    n_pairs = w // LANE
    tile = pl.BlockSpec((1, tt, LANE), lambda bi, hp, ti: (bi, ti, hp))
    vec = pl.BlockSpec((1, LANE), lambda bi, hp, ti: (0, hp))
    kernel = functools.partial(_rwkv_scan_kernel, n_chunks=tt // RWKV_CHUNK)
    return pl.pallas_call(
        kernel,
        out_shape=jax.ShapeDtypeStruct((b, t_len, w), BF16),
        grid=(b, n_pairs, t_len // tt),
        in_specs=[tile] * 7 + [vec] * 3,
        out_specs=tile,
        scratch_shapes=[pltpu.VMEM((LANE, LANE), F32)],
        compiler_params=pltpu.CompilerParams(dimension_semantics=("parallel", "parallel", "arbitrary"),
                                             vmem_limit_bytes=VMEM_LIMIT),
        name="rwkv_scan",
    )(r, k, v, kkr, a, lw, g, r_k, ln_g, ln_b)


def _rwkv_time_mix(z3, mu, w0, w_up, a0, a_up, g_up, k_k, k_a, r_k, ln_g, ln_b):
    row = lambda u: u.reshape(1, -1)
    mu_p = row(_take_columns(mu, _rwkv_source_columns()))
    r, k, v, kkr, a, lw, g = _rwkv_prep(z3, mu_p, row(w0), _pad_rows(w_up, LANE).astype(BF16), row(a0),
                                        _pad_rows(a_up, LANE).astype(BF16), g_up.astype(BF16), row(k_k), row(k_a))
    return _rwkv_scan(r, k, v, kkr, a, lw, g, row(r_k), row(ln_g), row(ln_b))


def _layer_norm(x, g, b):
    mean = jnp.mean(x, axis=-1, keepdims=True)
    xc = x - mean
    var = jnp.mean(xc * xc, axis=-1, keepdims=True)
    return xc * lax.rsqrt(var + LN_EPS) * g + b


def _out_proj_kernel(on_ref, or_ref, x_ref, wa_ref, wb_ref, g_ref, b_ref, rw_ref, rb_ref,
                     h_ref, idx_ref, wgt_ref, *, alpha):
    mix = (jnp.dot(on_ref[...], wa_ref[...], preferred_element_type=F32)
           + jnp.dot(or_ref[...], wb_ref[...], preferred_element_type=F32))
    h = _layer_norm(alpha * x_ref[...] + mix, g_ref[...], b_ref[...])
    h_ref[...] = h
    logits = jnp.dot(h, rw_ref[...], preferred_element_type=F32, precision=HIGHEST) + rb_ref[...]
    lane = lax.broadcasted_iota(jnp.int32, logits.shape, 1)
    logits = jnp.where(lane < N_EXPERTS, logits, -jnp.inf)
    idx_out = jnp.zeros(logits.shape, jnp.int32)
    val_out = jnp.full(logits.shape, -jnp.inf, F32)
    for k in range(TOP_K):
        best = jnp.max(logits, axis=-1, keepdims=True)
        first = jnp.min(jnp.where(logits == best, lane, LANE), axis=-1, keepdims=True)
        idx_out = jnp.where(lane == k, first, idx_out)
        val_out = jnp.where(lane == k, best, val_out)
        logits = jnp.where(lane == first, -jnp.inf, logits)
    e = jnp.exp(val_out - jnp.max(val_out, axis=-1, keepdims=True))
    idx_ref[...] = idx_out
    wgt_ref[...] = e / jnp.sum(e, axis=-1, keepdims=True)


def _out_proj_router(o_nsa, o_rwkv, x2, w_out, ln_g, ln_b, router_w, router_b, alpha, tm=512):
    n_tok, d = x2.shape
    half = o_nsa.shape[1]
    rw = jnp.pad(router_w, ((0, 0), (0, LANE - N_EXPERTS)))
    rb = jnp.pad(router_b, (0, LANE - N_EXPERTS)).reshape(1, LANE)
    row_blk = lambda n: pl.BlockSpec((tm, n), lambda i: (i, 0))
    full = lambda m, n: pl.BlockSpec((m, n), lambda i: (0, 0))
    return pl.pallas_call(
        functools.partial(_out_proj_kernel, alpha=alpha),
        out_shape=(jax.ShapeDtypeStruct((n_tok, d), F32), jax.ShapeDtypeStruct((n_tok, LANE), jnp.int32),
                   jax.ShapeDtypeStruct((n_tok, LANE), F32)),
        grid=(n_tok // tm,),
        in_specs=[row_blk(half), row_blk(half), row_blk(d), full(half, d), full(half, d), full(1, d), full(1, d),
                  full(d, LANE), full(1, LANE)],
        out_specs=(row_blk(d), row_blk(LANE), row_blk(LANE)),
        compiler_params=pltpu.CompilerParams(dimension_semantics=("parallel",), vmem_limit_bytes=VMEM_LIMIT),
        name="out_proj_router",
    )(o_nsa, o_rwkv, x2, w_out[:half].astype(BF16), w_out[half:].astype(BF16), ln_g.reshape(1, d),
      ln_b.reshape(1, d), rw, rb)


MOE_ITEM_ROWS = 1024
MOE_SUB_ROWS = 256
MOE_F_TILE = 256


def _moe_tables(top_idx, n_items):
    n_tok = top_idx.shape[0]
    flat_e = top_idx.reshape(-1)
    n_assign = flat_e.shape[0]
    onehot = (flat_e[:, None] == jnp.arange(N_EXPERTS, dtype=jnp.int32)[None, :]).astype(jnp.int32)
    csum = jnp.cumsum(onehot, axis=0)
    rank = jnp.take_along_axis(csum, flat_e[:, None], axis=1)[:, 0] - 1
    counts = csum[-1]
    items_e = (counts + MOE_ITEM_ROWS - 1) // MOE_ITEM_ROWS
    items_end = jnp.cumsum(items_e)
    item_start_e = items_end - items_e
    dest = item_start_e[flat_e] * MOE_ITEM_ROWS + rank
    row_tok = jnp.zeros((n_items * MOE_ITEM_ROWS,), jnp.int32).at[dest].set(
        jnp.arange(n_assign, dtype=jnp.int32) // TOP_K)
    item = jnp.arange(n_items, dtype=jnp.int32)
    valid = item < items_end[-1]
    last_e = jnp.max(jnp.where(counts > 0, jnp.arange(N_EXPERTS, dtype=jnp.int32), 0))
    item_e = jnp.minimum(jnp.searchsorted(items_end, item, side="right").astype(jnp.int32), N_EXPERTS - 1)
    item_e = jnp.where(valid, item_e, last_e)
    item_nv = jnp.where(valid, jnp.clip(counts[item_e] - (item - item_start_e[item_e]) * MOE_ITEM_ROWS,
                                        0, MOE_ITEM_ROWS), 0).astype(jnp.int32)
    return dest.reshape(n_tok, TOP_K), row_tok, item_e, item_nv


def _moe_kernel(item_e_ref, item_nv_ref, x_ref, wg_ref, bg_ref, wu_ref, bu_ref, wd_ref, bd_ref, o_ref):
    i = pl.program_id(0)
    f = pl.program_id(1)
    nv = item_nv_ref[i]
    wg = wg_ref[0].astype(BF16)
    wu = wu_ref[0].astype(BF16)
    wd = wd_ref[0].astype(BF16)
    for sb in range(MOE_ITEM_ROWS // MOE_SUB_ROWS):
        rows = pl.ds(sb * MOE_SUB_ROWS, MOE_SUB_ROWS)

        @pl.when(sb * MOE_SUB_ROWS < nv)
        def _():
            xb = x_ref[rows, :]
            gate = jnp.dot(xb, wg, preferred_element_type=F32) + bg_ref[0]
            up = jnp.dot(xb, wu, preferred_element_type=F32) + bu_ref[0]
            gate = jnp.minimum(gate, SWIGLU_LIMIT)
            up = jnp.clip(up, -SWIGLU_LIMIT, SWIGLU_LIMIT)
            h = gate * jax.nn.sigmoid(SWIGLU_ALPHA * gate) * (up + 1.0)
            y = jnp.dot(h.astype(BF16), wd, preferred_element_type=F32)

            @pl.when(f == 0)
            def _():
                o_ref[rows, :] = y + bd_ref[0]

            @pl.when(f > 0)
            def _():
                o_ref[rows, :] += y

        @pl.when((sb * MOE_SUB_ROWS >= nv) & (f == 0))
        def _():
            o_ref[rows, :] = jnp.zeros((MOE_SUB_ROWS, o_ref.shape[1]), o_ref.dtype)


def _moe_experts(xg, item_e, item_nv, w_gate, b_gate, w_up, b_up, w_down, b_down):
    n_rows, d = xg.shape
    n_items = n_rows // MOE_ITEM_ROWS
    n_e, _, d_ff = w_gate.shape
    n_f = d_ff // MOE_F_TILE

    def f_idx(i, f, nv):
        return jnp.where(nv[i] > 0, f, n_f - 1)

    grid_spec = pltpu.PrefetchScalarGridSpec(
        num_scalar_prefetch=2,
        grid=(n_items, n_f),
        in_specs=[
            pl.BlockSpec((MOE_ITEM_ROWS, d), lambda i, f, e, nv: (i, 0)),
            pl.BlockSpec((1, d, MOE_F_TILE), lambda i, f, e, nv: (e[i], 0, f_idx(i, f, nv))),
            pl.BlockSpec((1, 1, MOE_F_TILE), lambda i, f, e, nv: (e[i], 0, f_idx(i, f, nv))),
            pl.BlockSpec((1, d, MOE_F_TILE), lambda i, f, e, nv: (e[i], 0, f_idx(i, f, nv))),
            pl.BlockSpec((1, 1, MOE_F_TILE), lambda i, f, e, nv: (e[i], 0, f_idx(i, f, nv))),
            pl.BlockSpec((1, MOE_F_TILE, d), lambda i, f, e, nv: (e[i], f_idx(i, f, nv), 0)),
            pl.BlockSpec((1, 1, d), lambda i, f, e, nv: (e[i], 0, 0)),
        ],
        out_specs=pl.BlockSpec((MOE_ITEM_ROWS, d), lambda i, f, e, nv: (i, 0)),
    )
    return pl.pallas_call(
        _moe_kernel,
        out_shape=jax.ShapeDtypeStruct((n_rows, d), F32),
        grid_spec=grid_spec,
        compiler_params=pltpu.CompilerParams(dimension_semantics=("parallel", "arbitrary"),
                                             vmem_limit_bytes=VMEM_LIMIT),
        name="moe_experts",
    )(item_e, item_nv, xg, w_gate, b_gate.reshape(n_e, 1, d_ff), w_up, b_up.reshape(n_e, 1, d_ff), w_down,
      b_down.reshape(n_e, 1, d))


def _final_kernel(h_ref, y_ref, wgt_ref, p_ref, g_ref, b_ref, wgate_ref, wple_ref, o_ref, *, alpha):
    d = h_ref.shape[1]
    wgt = wgt_ref[...]
    ffn = wgt[:, 0:1] * y_ref[:, 0:d]
    for k in range(1, TOP_K):
        ffn = ffn + wgt[:, k:k + 1] * y_ref[:, k * d:(k + 1) * d]
    h = _layer_norm(alpha * h_ref[...] + ffn, g_ref[...], b_ref[...])
    gate = jax.nn.sigmoid(jnp.dot(h.astype(BF16), wgate_ref[...], preferred_element_type=F32))
    ple = jnp.dot(p_ref[...].astype(BF16), wple_ref[...], preferred_element_type=F32)
    o_ref[...] = h + gate * ple


def _final(h1, yg, top_w, p2, ln_g, ln_b, ple_gate_w, ple_w, alpha, tm=256):
    n_tok, d = h1.shape
    row_blk = lambda n: pl.BlockSpec((tm, n), lambda i: (i, 0))
    full = lambda m, n: pl.BlockSpec((m, n), lambda i: (0, 0))
    return pl.pallas_call(
        functools.partial(_final_kernel, alpha=alpha),
        out_shape=jax.ShapeDtypeStruct((n_tok, d), F32),
        grid=(n_tok // tm,),
        in_specs=[row_blk(d), row_blk(TOP_K * d), row_blk(LANE), row_blk(p2.shape[1]), full(1, d), full(1, d),
                  full(d, d), full(p2.shape[1], d)],
        out_specs=row_blk(d),
        compiler_params=pltpu.CompilerParams(dimension_semantics=("parallel",), vmem_limit_bytes=VMEM_LIMIT),
        name="combine_ln_ple",
    )(h1, yg, top_w, p2, ln_g.reshape(1, d), ln_b.reshape(1, d), ple_gate_w.astype(BF16), ple_w.astype(BF16))


def kernel(x, p, w_in, cmp_pe_k, cmp_w1_k, cmp_w2_k, cmp_pe_v, cmp_w1_v, cmp_w2_v, rwkv_mu, rwkv_w0, rwkv_w_up, rwkv_a0, rwkv_a_up, rwkv_g_up, rwkv_k_k, rwkv_k_a, rwkv_r_k, rwkv_ln_g, rwkv_ln_b, w_out, ln1_g, ln1_b, router_w, router_b, exp_w_gate, exp_b_gate, exp_w_up, exp_b_up, exp_w_down, exp_b_down, ln2_g, ln2_b, ple_w, ple_gate_w):
    b, t_len, d = x.shape
    depth = w_in.shape[0]
    alpha = float((2 * depth) ** 0.25)
    n_tok = b * t_len
    n_items = (n_tok * TOP_K) // MOE_ITEM_ROWS + N_EXPERTS
    h = x.reshape(n_tok, d)
    for i in range(depth):
        w_bf = _take_columns(w_in[i], _z_source_columns()).astype(BF16)
        z3 = _in_proj(h, w_bf).reshape(b, t_len, Z_COLS)
        pe, w1, w2 = _nsa_prepare(cmp_pe_k[i], cmp_w1_k[i], cmp_w2_k[i], cmp_pe_v[i], cmp_w1_v[i], cmp_w2_v[i])
        kv_cmp = _compress(z3, pe, w1, w2)
        o_nsa = _nsa_attention(z3, kv_cmp)
        o_rwkv = _rwkv_time_mix(z3, rwkv_mu[i], rwkv_w0[i], rwkv_w_up[i], rwkv_a0[i], rwkv_a_up[i], rwkv_g_up[i],
                                rwkv_k_k[i], rwkv_k_a[i], rwkv_r_k[i].reshape(-1), rwkv_ln_g[i], rwkv_ln_b[i])
        h1, top_idx, top_w = _out_proj_router(o_nsa.reshape(n_tok, -1), o_rwkv.reshape(n_tok, -1), h, w_out[i],
                                              ln1_g[i], ln1_b[i], router_w[i], router_b[i], alpha)
        dest, row_tok, item_e, item_nv = _moe_tables(top_idx[:, :TOP_K], n_items)
        xg = jnp.take(h1.astype(BF16), row_tok, axis=0)
        eo = _moe_experts(xg, item_e, item_nv, exp_w_gate[i], exp_b_gate[i], exp_w_up[i], exp_b_up[i],
                          exp_w_down[i], exp_b_down[i])
        yg = jnp.take(eo, dest.reshape(-1), axis=0).reshape(n_tok, TOP_K * d)
        h = _final(h1, yg, top_w, p[i].reshape(n_tok, -1), ln2_g[i], ln2_b[i], ple_gate_w[i], ple_w[i], alpha)
    return h.reshape(b, t_len, d)
```

```python
import functools

import numpy as np
import jax
import jax.numpy as jnp
from jax import lax
from jax.experimental import pallas as pl
from jax.experimental.pallas import tpu as pltpu

F32 = jnp.float32
BF16 = jnp.bfloat16
HIGHEST = lax.Precision.HIGHEST

LANE = 128
D_MODEL = 2048
HEAD_DIM = 64
NSA_HEADS = 16
NSA_KV_HEADS = 4
NSA_GROUP = NSA_HEADS // NSA_KV_HEADS
NSA_WIDTH = NSA_HEADS * HEAD_DIM
NSA_KV_WIDTH = NSA_KV_HEADS * HEAD_DIM
CMP_BLOCK = 32
CMP_STRIDE = 16
SEL_BLOCK = 64
N_SELECT = 16
WINDOW = 512
N_GATES = 3
Q_BLOCK = 64
RWKV_HEADS = 16
RWKV_WIDTH = RWKV_HEADS * HEAD_DIM
DECAY_LORA = 96
ICLR_LORA = 96
GATE_LORA = 256
GN_EPS = 64e-5
N_EXPERTS = 32
TOP_K = 4
SWIGLU_ALPHA = 1.702
SWIGLU_LIMIT = 7.0
PLE_DIM = 256
LN_EPS = 1e-5
NEG_INF = -1e30

NSA_COLS = NSA_WIDTH + 6 * NSA_KV_WIDTH + NSA_HEADS * N_GATES
RWKV_COLS = 3 * RWKV_WIDTH + DECAY_LORA + ICLR_LORA + GATE_LORA

RB_R = 0
RB_K = RB_R + RWKV_WIDTH // LANE
RB_V = RB_K + RWKV_WIDTH // LANE
RB_WLO = RB_V + RWKV_WIDTH // LANE
RB_ALO = RB_WLO + 1
RB_GLO = RB_ALO + 1
RWKV_BLOCKS = RB_GLO + GATE_LORA // LANE
ZB_RWKV = 0
ZB_Q = ZB_RWKV + RWKV_BLOCKS
ZB_KVC = ZB_Q + NSA_WIDTH // LANE
ZB_KVS = ZB_KVC + NSA_KV_HEADS
ZB_KVW = ZB_KVS + NSA_KV_HEADS
ZB_GATE = ZB_KVW + NSA_KV_HEADS
Z_BLOCKS = ZB_GATE + NSA_KV_HEADS
Z_COLS = Z_BLOCKS * LANE
assert (ZB_Q * LANE) % (NSA_GROUP * HEAD_DIM) == 0

VMEM_LIMIT = 56 * 1024 * 1024


def _z_source_columns():
    src = np.full((Z_COLS,), -1, np.int64)
    src[ZB_Q * LANE:ZB_Q * LANE + NSA_WIDTH] = np.arange(NSA_WIDTH)
    for branch in range(3):
        k0 = NSA_WIDTH + 2 * branch * NSA_KV_WIDTH
        v0 = k0 + NSA_KV_WIDTH
        for g in range(NSA_KV_HEADS):
            base = (ZB_KVC + branch * NSA_KV_HEADS + g) * LANE
            src[base:base + HEAD_DIM] = k0 + g * HEAD_DIM + np.arange(HEAD_DIM)
            src[base + HEAD_DIM:base + 2 * HEAD_DIM] = v0 + g * HEAD_DIM + np.arange(HEAD_DIM)
    g0 = NSA_WIDTH + 6 * NSA_KV_WIDTH
    for g in range(NSA_KV_HEADS):
        base = (ZB_GATE + g) * LANE
        for i in range(N_GATES):
            for r in range(NSA_GROUP):
                src[base + i * NSA_GROUP + r] = g0 + (g * NSA_GROUP + r) * N_GATES + i
    rwkv = _rwkv_source_columns()
    src[ZB_RWKV * LANE:(ZB_RWKV + RWKV_BLOCKS) * LANE] = np.where(rwkv >= 0, NSA_COLS + rwkv, -1)
    return src


def _rwkv_source_columns():
    src = np.full((RWKV_BLOCKS * LANE,), -1, np.int64)
    src[:3 * RWKV_WIDTH] = np.arange(3 * RWKV_WIDTH)
    src[RB_WLO * LANE:RB_WLO * LANE + DECAY_LORA] = 3 * RWKV_WIDTH + np.arange(DECAY_LORA)
    src[RB_ALO * LANE:RB_ALO * LANE + ICLR_LORA] = 3 * RWKV_WIDTH + DECAY_LORA + np.arange(ICLR_LORA)
    src[RB_GLO * LANE:] = 3 * RWKV_WIDTH + DECAY_LORA + ICLR_LORA + np.arange(GATE_LORA)
    return src


def _take_columns(w, src):
    pieces, i, n = [], 0, len(src)
    while i < n:
        j = i + 1
        if src[i] < 0:
            while j < n and src[j] < 0:
                j += 1
            pieces.append(jnp.zeros(w.shape[:-1] + (j - i,), w.dtype))
        else:
            while j < n and src[j] == src[i] + (j - i):
                j += 1
            pieces.append(w[..., int(src[i]):int(src[i]) + j - i])
        i = j
    return jnp.concatenate(pieces, axis=-1)


def _pad_rows(w, rows):
    return jnp.pad(w, ((0, rows - w.shape[0]), (0, 0)))


def _in_proj_kernel(x_ref, w_ref, z_ref, xb_ref):
    @pl.when(pl.program_id(1) == 0)
    def _():
        xb_ref[...] = x_ref[...].astype(BF16)

    z_ref[...] = jnp.dot(xb_ref[...], w_ref[...], preferred_element_type=F32).astype(z_ref.dtype)


def _in_proj(x2, w_bf, tm=1024, tn=512):
    n_tok, d = x2.shape
    n_cols = w_bf.shape[1]
    return pl.pallas_call(
        _in_proj_kernel,
        out_shape=jax.ShapeDtypeStruct((n_tok, n_cols), BF16),
        grid=(n_tok // tm, n_cols // tn),
        in_specs=[pl.BlockSpec((tm, d), lambda i, j: (i, 0)),
                  pl.BlockSpec((d, tn), lambda i, j: (0, j))],
        out_specs=pl.BlockSpec((tm, tn), lambda i, j: (i, j)),
        scratch_shapes=[pltpu.VMEM((tm, d), BF16)],
        compiler_params=pltpu.CompilerParams(dimension_semantics=("parallel", "arbitrary"),
                                             vmem_limit_bytes=VMEM_LIMIT),
        name="in_proj",
    )(x2, w_bf)


def _compress_kernel(kv_ref, pe_ref, w1_ref, w2_ref, out_ref, kv32_ref):
    n_chunks = kv_ref.shape[1] // CMP_STRIDE
    kv32_ref[...] = kv_ref[0].astype(F32)
    acc_lo = jnp.zeros((n_chunks, LANE), F32)
    acc_hi = jnp.zeros((n_chunks, LANE), F32)
    for i in range(CMP_STRIDE):
        rows = kv32_ref[pl.ds(i, n_chunks, stride=CMP_STRIDE), :]
        lo = (rows + pe_ref[i:i + 1, :]).astype(BF16)
        hi = (rows + pe_ref[CMP_STRIDE + i:CMP_STRIDE + i + 1, :]).astype(BF16)
        acc_lo += jnp.dot(lo, w1_ref[i], preferred_element_type=F32)
        acc_hi += jnp.dot(hi, w1_ref[CMP_STRIDE + i], preferred_element_type=F32)
    shifted = jnp.concatenate([acc_hi[1:], jnp.zeros((1, LANE), F32)], axis=0)
    hid = jax.nn.gelu(acc_lo + shifted)
    out = jnp.dot(hid.astype(BF16), w2_ref[...], preferred_element_type=F32)
    row = lax.broadcasted_iota(jnp.int32, out.shape, 0)
    out_ref[0, 0] = jnp.where(row < n_chunks - 1, out, 0.0).astype(out_ref.dtype)


def _compress(z3, pe, w1, w2):
    b, t_len, _ = z3.shape
    n_chunks = t_len // CMP_STRIDE
    return pl.pallas_call(
        _compress_kernel,
        out_shape=jax.ShapeDtypeStruct((b, NSA_KV_HEADS, n_chunks, LANE), BF16),
        grid=(b, NSA_KV_HEADS),
        in_specs=[pl.BlockSpec((1, t_len, LANE), lambda bi, g: (bi, 0, ZB_KVC + g)),
                  pl.BlockSpec((CMP_BLOCK, LANE), lambda bi, g: (0, 0)),
                  pl.BlockSpec((CMP_BLOCK, LANE, LANE), lambda bi, g: (0, 0, 0)),
                  pl.BlockSpec((LANE, LANE), lambda bi, g: (0, 0))],
        out_specs=pl.BlockSpec((1, 1, n_chunks, LANE), lambda bi, g: (bi, g, 0, 0)),
        scratch_shapes=[pltpu.VMEM((t_len, LANE), F32)],
        compiler_params=pltpu.CompilerParams(dimension_semantics=("parallel", "parallel"),
                                             vmem_limit_bytes=VMEM_LIMIT),
        name="kv_compress",
    )(z3, pe, w1, w2)


def _block_diag2(a, b):
    za = jnp.zeros(a.shape[:-1] + (b.shape[-1],), a.dtype)
    zb = jnp.zeros(b.shape[:-1] + (a.shape[-1],), b.dtype)
    return jnp.concatenate([jnp.concatenate([a, za], axis=-1), jnp.concatenate([zb, b], axis=-1)], axis=-2)


def _stack_heads(x):
    return jnp.concatenate([x[:, r * HEAD_DIM:(r + 1) * HEAD_DIM] for r in range(NSA_GROUP)], axis=0)


def _dot_nt(a, b):
    return lax.dot_general(a, b, (((1,), (1,)), ((), ())), preferred_element_type=F32)


def _eye(n, dtype):
    return jnp.where(lax.broadcasted_iota(jnp.int32, (n, n), 0) == lax.broadcasted_iota(jnp.int32, (n, n), 1),
                     1.0, 0.0).astype(dtype)


NSA_KEY_CHUNK = 256
NSA_V_ROWS = HEAD_DIM + 16


def _nsa_kernel(slopes_ref, *refs, n_sel, n_g):
    q_refs = refs[:n_g]
    (kvc_ref, kvs_ref, kvw_ref, gate_ref, ovl_ref, o_ref, vst_ref, vwt_ref, vct_ref, sel_ref,
     sc0_ref, sc1_ref, p0_ref, p1_ref) = refs[n_g:]
    g_base = pl.program_id(1) * n_g
    qi = pl.program_id(2)
    q0 = qi * Q_BLOCK
    cols = NSA_GROUP * Q_BLOCK
    kc_len = NSA_KEY_CHUNK
    blocks_per_chunk = kc_len // SEL_BLOCK
    n_chunks = kvs_ref.shape[1] // kc_len
    n_cmp_pad = kvc_ref.shape[2]
    n_slc = ovl_ref.shape[0]
    v_rows = vst_ref.shape[2]
    eye_dh = _eye(HEAD_DIM, BF16)
    k_lanes = lambda gg: slice(gg * LANE, gg * LANE + HEAD_DIM)
    v_lanes = lambda gg: slice(gg * LANE + HEAD_DIM, (gg + 1) * LANE)

    @pl.when(qi == 0)
    def _():
        ones_row = jnp.where(lax.broadcasted_iota(jnp.int32, (v_rows - HEAD_DIM, kc_len), 0) == 0,
                             1.0, 0.0).astype(BF16)

        def body(c, carry):
            rows = pl.ds(pl.multiple_of(c * kc_len, kc_len), kc_len)
            for gg in range(n_g):
                vst_ref[gg, c, 0:HEAD_DIM] = _dot_nt(eye_dh, kvs_ref[0, rows, v_lanes(gg)]).astype(BF16)
                vwt_ref[gg, c, 0:HEAD_DIM] = _dot_nt(eye_dh, kvw_ref[0, rows, v_lanes(gg)]).astype(BF16)
                vst_ref[gg, c, HEAD_DIM:v_rows] = ones_row
                vwt_ref[gg, c, HEAD_DIM:v_rows] = ones_row
            return carry

        lax.fori_loop(0, n_chunks, body, 0)
        for gg in range(n_g):
            vct_ref[gg] = _dot_nt(eye_dh, kvc_ref[0, gg, :, HEAD_DIM:2 * HEAD_DIM]).astype(BF16)

    log2e = float(np.log2(np.e))
    lane = lax.broadcasted_iota(jnp.int32, (1, cols), 1)
    head = lane // Q_BLOCK
    tq = q0 + lane % Q_BLOCK
    sub = lax.broadcasted_iota(jnp.int32, (kc_len, cols), 0)
    half = lax.broadcasted_iota(jnp.int32, (1, LANE), 1) // Q_BLOCK
    blk = lax.broadcasted_iota(jnp.int32, (n_slc, LANE), 0)
    blk8 = lax.broadcasted_iota(jnp.int32, (8, LANE), 0)
    cmp_end = lax.broadcasted_iota(jnp.int32, (n_cmp_pad, 1), 0) * CMP_STRIDE + (CMP_BLOCK - 1)
    d_cmp = tq - cmp_end
    m_cmp = d_cmp >= 0
    d_cmp_f = d_cmp.astype(F32)
    forced = (blk == 0) | (blk == qi) | (blk == qi - 1)
    ovl = ovl_ref[...]

    def front(gg):
        q = _stack_heads(q_refs[gg][0]).astype(F32)
        qs = (q * (HEAD_DIM ** -0.5 * log2e)).astype(BF16)
        slope = jnp.zeros((1, cols), F32)
        for r in range(NSA_GROUP):
            slope = jnp.where(head == r, slopes_ref[(g_base + gg) * NSA_GROUP + r] * log2e, slope)
        st = _dot_nt(kvc_ref[0, gg, :, 0:HEAD_DIM], qs)
        st = jnp.where(m_cmp, st - slope * d_cmp_f, NEG_INF)
        e = jnp.where(m_cmp, jnp.exp2(st - jnp.max(st, axis=0, keepdims=True)), 0.0)
        p_cmp = e / jnp.maximum(jnp.sum(e, axis=0, keepdims=True), 1e-30)
        o_cmp = jnp.dot(vct_ref[gg], p_cmp.astype(BF16), preferred_element_type=F32)
        y = p_cmp[:, 0:LANE] + p_cmp[:, LANE:2 * LANE]
        y = y + pltpu.roll(y, Q_BLOCK, 1)
        y_hi = y.astype(BF16)
        r1 = y - y_hi.astype(F32)
        y_mid = r1.astype(BF16)
        y_lo = (r1 - y_mid.astype(F32)).astype(BF16)
        imp = (jnp.dot(ovl, y_hi, preferred_element_type=F32) + jnp.dot(ovl, y_mid, preferred_element_type=F32)
               + jnp.dot(ovl, y_lo, preferred_element_type=F32))
        imp = jnp.where(forced, jnp.inf, jnp.where(blk > qi, -jnp.inf, imp))
        tiles = [imp[t:t + 8] for t in range(0, n_slc, 8)]
        ranks = [jnp.zeros((8, LANE), F32) for _ in tiles]
        for jp in range(0, n_slc, 2):
            row = jnp.where(half == 0, imp[jp:jp + 1, :], imp[jp + 1:jp + 2, :])
            for ti, tile in enumerate(tiles):
                t0 = ti * 8
                if t0 > jp + 1:
                    hit = jnp.where(row >= tile, 1.0, 0.0)
                elif t0 + 7 <= jp:
                    hit = jnp.where(row > tile, 1.0, 0.0)
                else:
                    hit = jnp.where(blk8 + t0 > jp + half, jnp.where(row >= tile, 1.0, 0.0),
                                    jnp.where(row > tile, 1.0, 0.0))
                ranks[ti] = ranks[ti] + hit
        rank = jnp.concatenate(ranks, axis=0)
        rank = rank + pltpu.roll(rank, Q_BLOCK, 1)
        neg = jnp.where((rank < n_sel) & (blk <= qi), 0.0, NEG_INF)
        neg2 = jnp.concatenate([neg, neg], axis=1)
        for j in range(n_slc):
            sel_ref[gg, j] = jnp.broadcast_to(neg2[j:j + 1, :], (8, cols))
        return qs, slope, slope * sub.astype(F32), o_cmp

    fronts = [front(gg) for gg in range(n_g)]

    init1 = (jnp.full((1, cols), NEG_INF, F32), jnp.zeros((v_rows, cols), F32))

    def block_mask(gg, c):
        tiles = [sel_ref[gg, c * blocks_per_chunk + i] for i in range(blocks_per_chunk)]
        return jnp.concatenate([t for t in tiles for _ in range(SEL_BLOCK // 8)], axis=0)

    streams = {"slc": (kvs_ref, vst_ref), "win": (kvw_ref, vwt_ref)}
    sc_slots = (sc0_ref, sc1_ref)
    p_slots = (p0_ref, p1_ref)

    def score(gg, pos, slot):
        stream, c, _ = pos
        rows = pl.ds(pl.multiple_of(c * kc_len, kc_len), kc_len)
        sc_slots[slot][gg] = _dot_nt(streams[stream][0][0, rows, k_lanes(gg)], fronts[gg][0])

    def softmax(gg, pos, slot, m_prev):
        _, c, mask_fn = pos
        _, slope, bias_local, _ = fronts[gg]
        sc = sc_slots[slot][gg] + bias_local + mask_fn(gg)
        shift = slope * (c * kc_len - q0).astype(F32)
        m_new = jnp.maximum(m_prev, jnp.max(sc, axis=0, keepdims=True) + shift)
        p_slots[slot][gg] = jnp.exp2(sc - (m_new - shift)).astype(BF16)
        return m_new, jnp.exp2(m_prev - m_new)

    def weigh(gg, pos, slot, alpha, acc):
        stream, c, _ = pos
        return alpha * acc + jnp.dot(streams[stream][1][gg, c], p_slots[slot][gg], preferred_element_type=F32)

    def step(pos, slot, nxt, prev, state, alphas):
        if nxt is not None:
            for gg in range(n_g):
                score(gg, nxt, 1 - slot)
        state = {s: list(v) for s, v in state.items()}
        new_alphas = []
        for gg in range(n_g):
            m_prev, acc = state[pos[0]][gg]
            m_new, alpha = softmax(gg, pos, slot, m_prev)
            state[pos[0]][gg] = (m_new, acc)
            new_alphas.append(alpha)
            if prev is not None:
                m_p, acc_p = state[prev[0]][gg]
                state[prev[0]][gg] = (m_p, weigh(gg, prev, 1 - slot, alphas[gg], acc_p))
        return state, new_alphas

    c_cur = qi // blocks_per_chunk
    n_pairs = c_cur // 2
    plain = lambda c: ("slc", c, lambda gg: block_mask(gg, c))

    for gg in range(n_g):
        p1_ref[gg] = jnp.zeros((kc_len, cols), BF16)
        score(gg, plain(0), 0)

    def pair_body(i, carry):
        slc_state, alphas = carry
        k0 = 2 * i
        state, alphas = step(plain(k0), 0, plain(k0 + 1), plain(jnp.maximum(k0 - 1, 0)),
                             {"slc": slc_state}, alphas)
        state, alphas = step(plain(k0 + 1), 1, plain(k0 + 2), plain(k0), state, alphas)
        return tuple(state["slc"]), tuple(alphas)

    slc_state, alphas = lax.fori_loop(0, n_pairs, pair_body,
                                      (tuple(init1 for _ in range(n_g)),
                                       tuple(jnp.ones((1, cols), F32) for _ in range(n_g))))
    k_t = 2 * n_pairs
    odd_neg = jnp.where(c_cur % 2 == 1, 0.0, NEG_INF)
    causal_neg = jnp.where((c_cur * kc_len + sub) <= tq, 0.0, NEG_INF)
    tail = [("slc", k_t, lambda gg: block_mask(gg, k_t) + odd_neg),
            ("slc", c_cur, lambda gg: block_mask(gg, c_cur) + causal_neg)]
    for i in range(WINDOW // kc_len, -1, -1):
        c_raw = c_cur - i
        c_win = jnp.maximum(c_raw, 0)
        dist = tq - (c_win * kc_len + sub)
        band_neg = jnp.where((dist >= 0) & (dist < WINDOW) & (c_raw >= 0), 0.0, NEG_INF)
        tail.append(("win", c_win, lambda gg, band_neg=band_neg: band_neg))
    state = {"slc": list(slc_state), "win": [init1 for _ in range(n_g)]}
    prev = plain(jnp.maximum(k_t - 1, 0))
    for idx, pos in enumerate(tail):
        nxt = tail[idx + 1] if idx + 1 < len(tail) else None
        state, alphas = step(pos, idx % 2, nxt, prev, state, alphas)
        prev = pos
    last_slot = (len(tail) - 1) % 2
    for gg in range(n_g):
        m_p, acc_p = state[prev[0]][gg]
        state[prev[0]][gg] = (m_p, weigh(gg, prev, last_slot, alphas[gg], acc_p))

    eye_q = _eye(Q_BLOCK, BF16)
    eye_lane = _eye(LANE, BF16)
    for gg in range(n_g):
        gates = jax.nn.sigmoid(_dot_nt(eye_lane, gate_ref[0, :, gg * LANE:(gg + 1) * LANE]))

        def gate_row(i):
            return jnp.concatenate([gates[i * NSA_GROUP + r:i * NSA_GROUP + r + 1, :] for r in range(NSA_GROUP)],
                                   axis=1)

        def normalized(acc):
            return acc[0:HEAD_DIM] / jnp.maximum(acc[HEAD_DIM:HEAD_DIM + 1], 1e-30)

        o = (gate_row(0) * fronts[gg][3] + gate_row(1) * normalized(state["slc"][gg][1])
             + gate_row(2) * normalized(state["win"][gg][1])).astype(BF16)
        o_rows = jnp.concatenate([o[:, r * Q_BLOCK:(r + 1) * Q_BLOCK] for r in range(NSA_GROUP)], axis=0)
        o_ref[0, :, gg * cols:(gg + 1) * cols] = _dot_nt(eye_q, o_rows).astype(o_ref.dtype)


NSA_GROUPS_PER_STEP = 4


def _nsa_attention(z3, kv_cmp):
    b, t_len, _ = z3.shape
    n_q = t_len // Q_BLOCK
    n_slc = t_len // SEL_BLOCK
    n_cmp_pad = kv_cmp.shape[2]
    n_sel = min(N_SELECT, n_slc)
    n_g = NSA_GROUPS_PER_STEP
    cols = NSA_GROUP * Q_BLOCK
    slopes = jnp.exp2(-8.0 * jnp.arange(1, NSA_HEADS + 1, dtype=F32) / NSA_HEADS)
    c_idx = np.arange(n_cmp_pad)
    cmp_start = c_idx * CMP_STRIDE
    cmp_end = cmp_start + CMP_BLOCK - 1
    slc_start = np.arange(n_slc) * SEL_BLOCK
    overlap = np.clip(np.minimum(cmp_end[None, :], slc_start[:, None] + SEL_BLOCK - 1)
                      - np.maximum(cmp_start[None, :], slc_start[:, None]) + 1, 0, None).astype(np.float32)
    overlap[:, c_idx >= t_len // CMP_STRIDE - CMP_BLOCK // CMP_STRIDE + 1] = 0.0
    kernel = functools.partial(_nsa_kernel, n_sel=n_sel, n_g=n_g)
    n_kc = t_len // NSA_KEY_CHUNK
    q_spec = lambda gg: pl.BlockSpec((1, Q_BLOCK, cols),
                                     lambda bi, g, qi: (bi, qi, ZB_Q * LANE // cols + g * n_g + gg))
    slab = lambda zb: pl.BlockSpec((1, t_len, n_g * LANE), lambda bi, g, qi: (bi, 0, zb // n_g + g))
    assert ZB_KVS % n_g == 0 and ZB_KVW % n_g == 0 and ZB_GATE % n_g == 0
    return pl.pallas_call(
        kernel,
        out_shape=jax.ShapeDtypeStruct((b, t_len, NSA_WIDTH), BF16),
        grid=(b, NSA_KV_HEADS // n_g, n_q),
        in_specs=[pl.BlockSpec(memory_space=pltpu.SMEM)] + [q_spec(gg) for gg in range(n_g)] + [
            pl.BlockSpec((1, n_g, n_cmp_pad, LANE), lambda bi, g, qi: (bi, g, 0, 0)),
            slab(ZB_KVS), slab(ZB_KVW),
            pl.BlockSpec((1, Q_BLOCK, n_g * LANE), lambda bi, g, qi: (bi, qi, ZB_GATE // n_g + g)),
            pl.BlockSpec((n_slc, n_cmp_pad), lambda bi, g, qi: (0, 0)),
        ],
        out_specs=pl.BlockSpec((1, Q_BLOCK, n_g * cols), lambda bi, g, qi: (bi, qi, g)),
        scratch_shapes=[pltpu.VMEM((n_g, n_kc, NSA_V_ROWS, NSA_KEY_CHUNK), BF16),
                        pltpu.VMEM((n_g, n_kc, NSA_V_ROWS, NSA_KEY_CHUNK), BF16),
                        pltpu.VMEM((n_g, HEAD_DIM, n_cmp_pad), BF16),
                        pltpu.VMEM((n_g, n_slc, 8, cols), F32),
                        pltpu.VMEM((n_g, NSA_KEY_CHUNK, cols), F32),
                        pltpu.VMEM((n_g, NSA_KEY_CHUNK, cols), F32),
                        pltpu.VMEM((n_g, NSA_KEY_CHUNK, cols), BF16),
                        pltpu.VMEM((n_g, NSA_KEY_CHUNK, cols), BF16)],
        compiler_params=pltpu.CompilerParams(dimension_semantics=("parallel", "parallel", "arbitrary"),
                                             vmem_limit_bytes=VMEM_LIMIT),
        name="nsa_attention",
    )(slopes, *([z3] * n_g), kv_cmp, z3, z3, z3, jnp.asarray(overlap, BF16))


def _nsa_prepare(cmp_pe_k, cmp_w1_k, cmp_w2_k, cmp_pe_v, cmp_w1_v, cmp_w2_v):
    pe = jnp.concatenate([cmp_pe_k, cmp_pe_v], axis=-1)
    w1 = _block_diag2(cmp_w1_k.reshape(CMP_BLOCK, HEAD_DIM, HEAD_DIM),
                      cmp_w1_v.reshape(CMP_BLOCK, HEAD_DIM, HEAD_DIM)).astype(BF16)
    w2 = _block_diag2(cmp_w2_k, cmp_w2_v).astype(BF16)
    return pe, w1, w2


RWKV_CHUNK = 64


def _rwkv_prep_kernel(z_ref, mu_ref, w0_ref, wup_ref, a0_ref, aup_ref, gup_ref, kk_ref, ka_ref,
                      r_ref, k_ref, v_ref, kkr_ref, a_ref, lw_ref, g_ref, carry_ref):
    w = RWKV_WIDTH

    @pl.when(pl.program_id(1) == 0)
    def _():
        carry_ref[...] = jnp.zeros_like(carry_ref)

    z = z_ref[0].astype(F32)
    tc = z.shape[0]
    row = lax.broadcasted_iota(jnp.int32, z.shape, 0)
    prev = jnp.where(row == 0, carry_ref[0:1, :], pltpu.roll(z, 1, 0))
    carry_ref[0:1, :] = z[tc - 1:tc, :]
    zs = z + (prev - z) * mu_ref[...]
    r = zs[:, RB_R * LANE:RB_R * LANE + w]
    k = zs[:, RB_K * LANE:RB_K * LANE + w]
    v = zs[:, RB_V * LANE:RB_V * LANE + w]
    w_lo = zs[:, RB_WLO * LANE:(RB_WLO + 1) * LANE]
    a_lo = zs[:, RB_ALO * LANE:(RB_ALO + 1) * LANE]
    g_lo = zs[:, RB_GLO * LANE:RB_GLO * LANE + GATE_LORA]
    d = w0_ref[...] + jnp.dot(jnp.tanh(w_lo).astype(BF16), wup_ref[...], preferred_element_type=F32)
    w_raw = -jax.nn.softplus(-d) - 0.5
    lw_ref[0] = -jnp.exp(w_raw)
    a = jax.nn.sigmoid(a0_ref[...] + jnp.dot(a_lo.astype(BF16), aup_ref[...], preferred_element_type=F32))
    g = jnp.dot(jax.nn.sigmoid(g_lo).astype(BF16), gup_ref[...], preferred_element_type=F32)
    r_ref[0] = r.astype(r_ref.dtype)
    v_ref[0] = v.astype(v_ref.dtype)
    kkr_ref[0] = (k * kk_ref[...]).astype(kkr_ref.dtype)
    k_ref[0] = (k * (1.0 + (a - 1.0) * ka_ref[...])).astype(k_ref.dtype)
    a_ref[0] = a.astype(a_ref.dtype)
    g_ref[0] = g.astype(g_ref.dtype)


def _rwkv_prep(z3, mu, w0, w_up, a0, a_up, g_up, k_k, k_a, tc=256):
    b, t_len, _ = z3.shape
    w = RWKV_WIDTH
    ncol = RWKV_BLOCKS * LANE
    vec = lambda n: pl.BlockSpec((1, n), lambda bi, ti: (0, 0))
    mat = lambda m, n: pl.BlockSpec((m, n), lambda bi, ti: (0, 0))
    out_bf = jax.ShapeDtypeStruct((b, t_len, w), BF16)
    out_f32 = jax.ShapeDtypeStruct((b, t_len, w), F32)
    out_spec = pl.BlockSpec((1, tc, w), lambda bi, ti: (bi, ti, 0))
    return pl.pallas_call(
        _rwkv_prep_kernel,
        out_shape=(out_bf, out_bf, out_bf, out_bf, out_bf, out_f32, out_bf),
        grid=(b, t_len // tc),
        in_specs=[pl.BlockSpec((1, tc, ncol), lambda bi, ti: (bi, ti, ZB_RWKV)),
                  vec(ncol), vec(w), mat(LANE, w), vec(w), mat(LANE, w), mat(GATE_LORA, w), vec(w), vec(w)],
        out_specs=(out_spec,) * 7,
        scratch_shapes=[pltpu.VMEM((8, ncol), F32)],
        compiler_params=pltpu.CompilerParams(dimension_semantics=("parallel", "arbitrary"),
                                             vmem_limit_bytes=VMEM_LIMIT),
        name="rwkv_prep",
    )(z3, mu, w0, w_up, a0, a_up, g_up, k_k, k_a)


def _pair_blocks(x):
    lane = lax.broadcasted_iota(jnp.int32, x.shape, 1)
    zero = jnp.zeros((), x.dtype)
    return jnp.concatenate([jnp.where(lane < HEAD_DIM, x, zero), jnp.where(lane >= HEAD_DIM, x, zero)], axis=0)


def _fold_pair(x):
    n = x.shape[0] // 2
    return x[:n] + x[n:]


def _rwkv_scan_kernel(r_ref, k_ref, v_ref, kkr_ref, a_ref, lw_ref, g_ref, rk_ref, lng_ref, lnb_ref,
                      o_ref, s_ref, *, n_chunks):
    L = RWKV_CHUNK
    L2 = 2 * L

    @pl.when(pl.program_id(2) == 0)
    def _():
        s_ref[...] = jnp.zeros_like(s_ref)

    ri = lax.broadcasted_iota(jnp.int32, (L, L), 0)
    ci = lax.broadcasted_iota(jnp.int32, (L, L), 1)
    tri_incl = jnp.where(ri >= ci, 1.0, 0.0).astype(F32)
    r2 = lax.broadcasted_iota(jnp.int32, (L2, L2), 0)
    c2 = lax.broadcasted_iota(jnp.int32, (L2, L2), 1)
    same_head = (r2 // L) == (c2 // L)
    strict2 = same_head & (r2 > c2)
    incl2 = same_head & (r2 >= c2)
    eye2 = jnp.where(r2 == c2, 1.0, 0.0).astype(F32)
    h_r = lax.broadcasted_iota(jnp.int32, (LANE, LANE), 0) // HEAD_DIM
    h_c = lax.broadcasted_iota(jnp.int32, (LANE, LANE), 1) // HEAD_DIM
    head_mask = h_r == h_c
    head_ones = jnp.where(head_mask, 1.0, 0.0).astype(F32)

    def head_sum(x):
        return jnp.dot(x, head_ones, preferred_element_type=F32, precision=HIGHEST)

    def mm(a, b):
        return jnp.dot(a.astype(BF16), b.astype(BF16), preferred_element_type=F32)

    for c in range(n_chunks):
        sl = pl.ds(c * L, L)
        r = r_ref[0, sl, :].astype(F32)
        k = k_ref[0, sl, :].astype(F32)
        v = v_ref[0, sl, :].astype(F32)
        kkr = kkr_ref[0, sl, :].astype(F32)
        a = a_ref[0, sl, :].astype(F32)
        lw = lw_ref[0, sl, :]
        kk = kkr / jnp.maximum(jnp.sqrt(head_sum(kkr * kkr)), 1e-12)
        bvec = kk * a
        cl = jnp.dot(tri_incl, lw, preferred_element_type=F32, precision=HIGHEST)
        p_incl = jnp.exp(cl)
        p_inv = jnp.exp(-cl)
        rt = r * p_incl
        at = -kk * jnp.exp(cl - lw)
        kt = k * p_inv
        bt = bvec * p_inv
        lhs = jnp.concatenate([_pair_blocks(at), _pair_blocks(rt)], axis=0).astype(BF16)
        rhs = jnp.concatenate([_pair_blocks(bt), _pair_blocks(kt)], axis=0).astype(BF16)
        gram = _dot_nt(lhs, rhs)
        a_ab = jnp.where(strict2, gram[:L2, :L2], 0.0)
        a_ak = jnp.where(strict2, gram[:L2, L2:], 0.0)
        a_rb = jnp.where(incl2, gram[L2:, :L2], 0.0)
        a_rk = jnp.where(incl2, gram[L2:, L2:], 0.0)
        tinv = eye2 + a_ab
        apow = a_ab
        for _ in range(int(np.log2(L)) - 1):
            apow = mm(apow, apow)
            tinv = tinv + mm(tinv, apow)
        v2 = _pair_blocks(v)
        s = s_ref[...]
        m = _dot_nt(at.astype(BF16), s.astype(BF16)) + _fold_pair(mm(a_ak, v2))
        u = _fold_pair(mm(tinv, _pair_blocks(m)))
        y = (_dot_nt(rt.astype(BF16), s.astype(BF16)) + _fold_pair(mm(a_rb, _pair_blocks(u)))
             + _fold_pair(mm(a_rk, v2)))
        p_last = p_incl[L - 1:L, :]
        upd = lax.dot_general(jnp.concatenate([u, v], axis=0).astype(BF16),
                              jnp.concatenate([bt * p_last, kt * p_last], axis=0).astype(BF16),
                              (((0,), (0,)), ((), ())), preferred_element_type=F32)
        s_ref[...] = s * p_last + jnp.where(head_mask, upd, 0.0)
        mean = head_sum(y) * (1.0 / HEAD_DIM)
        yc = y - mean
        var = head_sum(yc * yc) * (1.0 / HEAD_DIM)
        yn = yc * lax.rsqrt(var + GN_EPS) * lng_ref[...] + lnb_ref[...]
        bonus = head_sum(r * k * rk_ref[...]) * v
        o_ref[0, sl, :] = ((yn + bonus) * g_ref[0, sl, :].astype(F32)).astype(o_ref.dtype)


def _rwkv_scan(r, k, v, kkr, a, lw, g, r_k, ln_g, ln_b, tt=256):
    b, t_len, w = r.shape
    n_pairs = w // LANE
    tile = pl.BlockSpec((1, tt, LANE), lambda bi, hp, ti: (bi, ti, hp))
    vec = pl.BlockSpec((1, LANE), lambda bi, hp, ti: (0, hp))
    kernel = functools.partial(_rwkv_scan_kernel, n_chunks=tt // RWKV_CHUNK)
    return pl.pallas_call(
        kernel,
        out_shape=jax.ShapeDtypeStruct((b, t_len, w), BF16),
        grid=(b, n_pairs, t_len // tt),
        in_specs=[tile] * 7 + [vec] * 3,
        out_specs=tile,
        scratch_shapes=[pltpu.VMEM((LANE, LANE), F32)],
        compiler_params=pltpu.CompilerParams(dimension_semantics=("parallel", "parallel", "arbitrary"),
                                             vmem_limit_bytes=VMEM_LIMIT),
        name="rwkv_scan",
    )(r, k, v, kkr, a, lw, g, r_k, ln_g, ln_b)


def _rwkv_time_mix(z3, mu, w0, w_up, a0, a_up, g_up, k_k, k_a, r_k, ln_g, ln_b):
    row = lambda u: u.reshape(1, -1)
    mu_p = row(_take_columns(mu, _rwkv_source_columns()))
    r, k, v, kkr, a, lw, g = _rwkv_prep(z3, mu_p, row(w0), _pad_rows(w_up, LANE).astype(BF16), row(a0),
                                        _pad_rows(a_up, LANE).astype(BF16), g_up.astype(BF16), row(k_k), row(k_a))
    return _rwkv_scan(r, k, v, kkr, a, lw, g, row(r_k), row(ln_g), row(ln_b))


def _layer_norm(x, g, b):
    mean = jnp.mean(x, axis=-1, keepdims=True)
    xc = x - mean
    var = jnp.mean(xc * xc, axis=-1, keepdims=True)
    return xc * lax.rsqrt(var + LN_EPS) * g + b


def _out_proj_kernel(on_ref, or_ref, x_ref, wa_ref, wb_ref, g_ref, b_ref, rw_ref, rb_ref,
                     h_ref, idx_ref, wgt_ref, *, alpha):
    mix = (jnp.dot(on_ref[...], wa_ref[...], preferred_element_type=F32)
           + jnp.dot(or_ref[...], wb_ref[...], preferred_element_type=F32))
    h = _layer_norm(alpha * x_ref[...] + mix, g_ref[...], b_ref[...])
    h_ref[...] = h
    logits = jnp.dot(h, rw_ref[...], preferred_element_type=F32, precision=HIGHEST) + rb_ref[...]
    lane = lax.broadcasted_iota(jnp.int32, logits.shape, 1)
    logits = jnp.where(lane < N_EXPERTS, logits, -jnp.inf)
    idx_out = jnp.zeros(logits.shape, jnp.int32)
    val_out = jnp.full(logits.shape, -jnp.inf, F32)
    for k in range(TOP_K):
        best = jnp.max(logits, axis=-1, keepdims=True)
        first = jnp.min(jnp.where(logits == best, lane, LANE), axis=-1, keepdims=True)
        idx_out = jnp.where(lane == k, first, idx_out)
        val_out = jnp.where(lane == k, best, val_out)
        logits = jnp.where(lane == first, -jnp.inf, logits)
    e = jnp.exp(val_out - jnp.max(val_out, axis=-1, keepdims=True))
    idx_ref[...] = idx_out
    wgt_ref[...] = e / jnp.sum(e, axis=-1, keepdims=True)


def _out_proj_router(o_nsa, o_rwkv, x2, w_out, ln_g, ln_b, router_w, router_b, alpha, tm=512):
    n_tok, d = x2.shape
    half = o_nsa.shape[1]
    rw = jnp.pad(router_w, ((0, 0), (0, LANE - N_EXPERTS)))
    rb = jnp.pad(router_b, (0, LANE - N_EXPERTS)).reshape(1, LANE)
    row_blk = lambda n: pl.BlockSpec((tm, n), lambda i: (i, 0))
    full = lambda m, n: pl.BlockSpec((m, n), lambda i: (0, 0))
    return pl.pallas_call(
        functools.partial(_out_proj_kernel, alpha=alpha),
        out_shape=(jax.ShapeDtypeStruct((n_tok, d), F32), jax.ShapeDtypeStruct((n_tok, LANE), jnp.int32),
                   jax.ShapeDtypeStruct((n_tok, LANE), F32)),
        grid=(n_tok // tm,),
        in_specs=[row_blk(half), row_blk(half), row_blk(d), full(half, d), full(half, d), full(1, d), full(1, d),
                  full(d, LANE), full(1, LANE)],
        out_specs=(row_blk(d), row_blk(LANE), row_blk(LANE)),
        compiler_params=pltpu.CompilerParams(dimension_semantics=("parallel",), vmem_limit_bytes=VMEM_LIMIT),
        name="out_proj_router",
    )(o_nsa, o_rwkv, x2, w_out[:half].astype(BF16), w_out[half:].astype(BF16), ln_g.reshape(1, d),
      ln_b.reshape(1, d), rw, rb)


MOE_ITEM_ROWS = 1024
MOE_SUB_ROWS = 256
MOE_F_TILE = 256


def _moe_tables(top_idx, n_items):
    n_tok = top_idx.shape[0]
    flat_e = top_idx.reshape(-1)
    n_assign = flat_e.shape[0]
    onehot = (flat_e[:, None] == jnp.arange(N_EXPERTS, dtype=jnp.int32)[None, :]).astype(jnp.int32)
    csum = jnp.cumsum(onehot, axis=0)
    rank = jnp.take_along_axis(csum, flat_e[:, None], axis=1)[:, 0] - 1
    counts = csum[-1]
    items_e = (counts + MOE_ITEM_ROWS - 1) // MOE_ITEM_ROWS
    items_end = jnp.cumsum(items_e)
    item_start_e = items_end - items_e
    dest = item_start_e[flat_e] * MOE_ITEM_ROWS + rank
    row_tok = jnp.zeros((n_items * MOE_ITEM_ROWS,), jnp.int32).at[dest].set(
        jnp.arange(n_assign, dtype=jnp.int32) // TOP_K)
    item = jnp.arange(n_items, dtype=jnp.int32)
    valid = item < items_end[-1]
    last_e = jnp.max(jnp.where(counts > 0, jnp.arange(N_EXPERTS, dtype=jnp.int32), 0))
    item_e = jnp.minimum(jnp.searchsorted(items_end, item, side="right").astype(jnp.int32), N_EXPERTS - 1)
    item_e = jnp.where(valid, item_e, last_e)
    item_nv = jnp.where(valid, jnp.clip(counts[item_e] - (item - item_start_e[item_e]) * MOE_ITEM_ROWS,
                                        0, MOE_ITEM_ROWS), 0).astype(jnp.int32)
    return dest.reshape(n_tok, TOP_K), row_tok, item_e, item_nv


def _moe_kernel(item_e_ref, item_nv_ref, x_ref, wg_ref, bg_ref, wu_ref, bu_ref, wd_ref, bd_ref, o_ref):
    i = pl.program_id(0)
    f = pl.program_id(1)
    nv = item_nv_ref[i]
    wg = wg_ref[0].astype(BF16)
    wu = wu_ref[0].astype(BF16)
    wd = wd_ref[0].astype(BF16)
    for sb in range(MOE_ITEM_ROWS // MOE_SUB_ROWS):
        rows = pl.ds(sb * MOE_SUB_ROWS, MOE_SUB_ROWS)

        @pl.when(sb * MOE_SUB_ROWS < nv)
        def _():
            xb = x_ref[rows, :]
            gate = jnp.dot(xb, wg, preferred_element_type=F32) + bg_ref[0]
            up = jnp.dot(xb, wu, preferred_element_type=F32) + bu_ref[0]
            gate = jnp.minimum(gate, SWIGLU_LIMIT)
            up = jnp.clip(up, -SWIGLU_LIMIT, SWIGLU_LIMIT)
            h = gate * jax.nn.sigmoid(SWIGLU_ALPHA * gate) * (up + 1.0)
            y = jnp.dot(h.astype(BF16), wd, preferred_element_type=F32)

            @pl.when(f == 0)
            def _():
                o_ref[rows, :] = y + bd_ref[0]

            @pl.when(f > 0)
            def _():
                o_ref[rows, :] += y

        @pl.when((sb * MOE_SUB_ROWS >= nv) & (f == 0))
        def _():
            o_ref[rows, :] = jnp.zeros((MOE_SUB_ROWS, o_ref.shape[1]), o_ref.dtype)


def _moe_experts(xg, item_e, item_nv, w_gate, b_gate, w_up, b_up, w_down, b_down):
    n_rows, d = xg.shape
    n_items = n_rows // MOE_ITEM_ROWS
    n_e, _, d_ff = w_gate.shape
    n_f = d_ff // MOE_F_TILE

    def f_idx(i, f, nv):
        return jnp.where(nv[i] > 0, f, n_f - 1)

    grid_spec = pltpu.PrefetchScalarGridSpec(
        num_scalar_prefetch=2,
        grid=(n_items, n_f),
        in_specs=[
            pl.BlockSpec((MOE_ITEM_ROWS, d), lambda i, f, e, nv: (i, 0)),
            pl.BlockSpec((1, d, MOE_F_TILE), lambda i, f, e, nv: (e[i], 0, f_idx(i, f, nv))),
            pl.BlockSpec((1, 1, MOE_F_TILE), lambda i, f, e, nv: (e[i], 0, f_idx(i, f, nv))),
            pl.BlockSpec((1, d, MOE_F_TILE), lambda i, f, e, nv: (e[i], 0, f_idx(i, f, nv))),
            pl.BlockSpec((1, 1, MOE_F_TILE), lambda i, f, e, nv: (e[i], 0, f_idx(i, f, nv))),
            pl.BlockSpec((1, MOE_F_TILE, d), lambda i, f, e, nv: (e[i], f_idx(i, f, nv), 0)),
            pl.BlockSpec((1, 1, d), lambda i, f, e, nv: (e[i], 0, 0)),
        ],
        out_specs=pl.BlockSpec((MOE_ITEM_ROWS, d), lambda i, f, e, nv: (i, 0)),
    )
    return pl.pallas_call(
        _moe_kernel,
        out_shape=jax.ShapeDtypeStruct((n_rows, d), F32),
        grid_spec=grid_spec,
        compiler_params=pltpu.CompilerParams(dimension_semantics=("parallel", "arbitrary"),
                                             vmem_limit_bytes=VMEM_LIMIT),
        name="moe_experts",
    )(item_e, item_nv, xg, w_gate, b_gate.reshape(n_e, 1, d_ff), w_up, b_up.reshape(n_e, 1, d_ff), w_down,
      b_down.reshape(n_e, 1, d))


def _final_kernel(h_ref, y_ref, wgt_ref, p_ref, g_ref, b_ref, wgate_ref, wple_ref, o_ref, *, alpha):
    d = h_ref.shape[1]
    wgt = wgt_ref[...]
    ffn = wgt[:, 0:1] * y_ref[:, 0:d]
    for k in range(1, TOP_K):
        ffn = ffn + wgt[:, k:k + 1] * y_ref[:, k * d:(k + 1) * d]
    h = _layer_norm(alpha * h_ref[...] + ffn, g_ref[...], b_ref[...])
    gate = jax.nn.sigmoid(jnp.dot(h.astype(BF16), wgate_ref[...], preferred_element_type=F32))
    ple = jnp.dot(p_ref[...].astype(BF16), wple_ref[...], preferred_element_type=F32)
    o_ref[...] = h + gate * ple


def _final(h1, yg, top_w, p2, ln_g, ln_b, ple_gate_w, ple_w, alpha, tm=256):
    n_tok, d = h1.shape
    row_blk = lambda n: pl.BlockSpec((tm, n), lambda i: (i, 0))
    full = lambda m, n: pl.BlockSpec((m, n), lambda i: (0, 0))
    return pl.pallas_call(
        functools.partial(_final_kernel, alpha=alpha),
        out_shape=jax.ShapeDtypeStruct((n_tok, d), F32),
        grid=(n_tok // tm,),
        in_specs=[row_blk(d), row_blk(TOP_K * d), row_blk(LANE), row_blk(p2.shape[1]), full(1, d), full(1, d),
                  full(d, d), full(p2.shape[1], d)],
        out_specs=row_blk(d),
        compiler_params=pltpu.CompilerParams(dimension_semantics=("parallel",), vmem_limit_bytes=VMEM_LIMIT),
        name="combine_ln_ple",
    )(h1, yg, top_w, p2, ln_g.reshape(1, d), ln_b.reshape(1, d), ple_gate_w.astype(BF16), ple_w.astype(BF16))


def kernel(x, p, w_in, cmp_pe_k, cmp_w1_k, cmp_w2_k, cmp_pe_v, cmp_w1_v, cmp_w2_v, rwkv_mu, rwkv_w0, rwkv_w_up, rwkv_a0, rwkv_a_up, rwkv_g_up, rwkv_k_k, rwkv_k_a, rwkv_r_k, rwkv_ln_g, rwkv_ln_b, w_out, ln1_g, ln1_b, router_w, router_b, exp_w_gate, exp_b_gate, exp_w_up, exp_b_up, exp_w_down, exp_b_down, ln2_g, ln2_b, ple_w, ple_gate_w):
    b, t_len, d = x.shape
    depth = w_in.shape[0]
    alpha = float((2 * depth) ** 0.25)
    n_tok = b * t_len
    n_items = (n_tok * TOP_K) // MOE_ITEM_ROWS + N_EXPERTS
    h = x.reshape(n_tok, d)
    for i in range(depth):
        w_bf = _take_columns(w_in[i].astype(BF16), _z_source_columns())
        z3 = _in_proj(h, w_bf).reshape(b, t_len, Z_COLS)
        pe, w1, w2 = _nsa_prepare(cmp_pe_k[i], cmp_w1_k[i], cmp_w2_k[i], cmp_pe_v[i], cmp_w1_v[i], cmp_w2_v[i])
        kv_cmp = _compress(z3, pe, w1, w2)
        o_nsa = _nsa_attention(z3, kv_cmp)
        o_rwkv = _rwkv_time_mix(z3, rwkv_mu[i], rwkv_w0[i], rwkv_w_up[i], rwkv_a0[i], rwkv_a_up[i], rwkv_g_up[i],
                                rwkv_k_k[i], rwkv_k_a[i], rwkv_r_k[i].reshape(-1), rwkv_ln_g[i], rwkv_ln_b[i])
        h1, top_idx, top_w = _out_proj_router(o_nsa.reshape(n_tok, -1), o_rwkv.reshape(n_tok, -1), h, w_out[i],
                                              ln1_g[i], ln1_b[i], router_w[i], router_b[i], alpha)
        dest, row_tok, item_e, item_nv = _moe_tables(top_idx[:, :TOP_K], n_items)
        xg = jnp.take(h1.astype(BF16), row_tok, axis=0)
        eo = _moe_experts(xg, item_e, item_nv, exp_w_gate[i], exp_b_gate[i], exp_w_up[i], exp_b_up[i],
                          exp_w_down[i], exp_b_down[i])
        yg = jnp.take(eo, dest.reshape(-1), axis=0).reshape(n_tok, TOP_K * d)
        h = _final(h1, yg, top_w, p[i].reshape(n_tok, -1), ln2_g[i], ln2_b[i], ple_gate_w[i], ple_w[i], alpha)
    return h.reshape(b, t_len, d)
```

```python
import functools

import numpy as np
import jax
import jax.numpy as jnp
from jax import lax
from jax.experimental import pallas as pl
from jax.experimental.pallas import tpu as pltpu

F32 = jnp.float32
BF16 = jnp.bfloat16
HIGHEST = lax.Precision.HIGHEST

LANE = 128
D_MODEL = 2048
HEAD_DIM = 64
NSA_HEADS = 16
NSA_KV_HEADS = 4
NSA_GROUP = NSA_HEADS // NSA_KV_HEADS
NSA_WIDTH = NSA_HEADS * HEAD_DIM
NSA_KV_WIDTH = NSA_KV_HEADS * HEAD_DIM
CMP_BLOCK = 32
CMP_STRIDE = 16
SEL_BLOCK = 64
N_SELECT = 16
WINDOW = 512
N_GATES = 3
Q_BLOCK = 64
RWKV_HEADS = 16
RWKV_WIDTH = RWKV_HEADS * HEAD_DIM
DECAY_LORA = 96
ICLR_LORA = 96
GATE_LORA = 256
GN_EPS = 64e-5
N_EXPERTS = 32
TOP_K = 4
SWIGLU_ALPHA = 1.702
SWIGLU_LIMIT = 7.0
PLE_DIM = 256
LN_EPS = 1e-5
NEG_INF = -1e30

NSA_COLS = NSA_WIDTH + 6 * NSA_KV_WIDTH + NSA_HEADS * N_GATES
RWKV_COLS = 3 * RWKV_WIDTH + DECAY_LORA + ICLR_LORA + GATE_LORA

RB_R = 0
RB_K = RB_R + RWKV_WIDTH // LANE
RB_V = RB_K + RWKV_WIDTH // LANE
RB_WLO = RB_V + RWKV_WIDTH // LANE
RB_ALO = RB_WLO + 1
RB_GLO = RB_ALO + 1
RWKV_BLOCKS = RB_GLO + GATE_LORA // LANE
ZB_RWKV = 0
ZB_Q = ZB_RWKV + RWKV_BLOCKS
ZB_KVC = ZB_Q + NSA_WIDTH // LANE
ZB_KVS = ZB_KVC + NSA_KV_HEADS
ZB_KVW = ZB_KVS + NSA_KV_HEADS
ZB_GATE = ZB_KVW + NSA_KV_HEADS
Z_BLOCKS = ZB_GATE + NSA_KV_HEADS
Z_COLS = Z_BLOCKS * LANE
assert (ZB_Q * LANE) % (NSA_GROUP * HEAD_DIM) == 0

VMEM_LIMIT = 56 * 1024 * 1024


def _z_source_columns():
    src = np.full((Z_COLS,), -1, np.int64)
    src[ZB_Q * LANE:ZB_Q * LANE + NSA_WIDTH] = np.arange(NSA_WIDTH)
    for branch in range(3):
        k0 = NSA_WIDTH + 2 * branch * NSA_KV_WIDTH
        v0 = k0 + NSA_KV_WIDTH
        for g in range(NSA_KV_HEADS):
            base = (ZB_KVC + branch * NSA_KV_HEADS + g) * LANE
            src[base:base + HEAD_DIM] = k0 + g * HEAD_DIM + np.arange(HEAD_DIM)
            src[base + HEAD_DIM:base + 2 * HEAD_DIM] = v0 + g * HEAD_DIM + np.arange(HEAD_DIM)
    g0 = NSA_WIDTH + 6 * NSA_KV_WIDTH
    for g in range(NSA_KV_HEADS):
        base = (ZB_GATE + g) * LANE
        for i in range(N_GATES):
            for r in range(NSA_GROUP):
                src[base + i * NSA_GROUP + r] = g0 + (g * NSA_GROUP + r) * N_GATES + i
    rwkv = _rwkv_source_columns()
    src[ZB_RWKV * LANE:(ZB_RWKV + RWKV_BLOCKS) * LANE] = np.where(rwkv >= 0, NSA_COLS + rwkv, -1)
    return src


def _rwkv_source_columns():
    src = np.full((RWKV_BLOCKS * LANE,), -1, np.int64)
    src[:3 * RWKV_WIDTH] = np.arange(3 * RWKV_WIDTH)
    src[RB_WLO * LANE:RB_WLO * LANE + DECAY_LORA] = 3 * RWKV_WIDTH + np.arange(DECAY_LORA)
    src[RB_ALO * LANE:RB_ALO * LANE + ICLR_LORA] = 3 * RWKV_WIDTH + DECAY_LORA + np.arange(ICLR_LORA)
    src[RB_GLO * LANE:] = 3 * RWKV_WIDTH + DECAY_LORA + ICLR_LORA + np.arange(GATE_LORA)
    return src


def _take_columns(w, src):
    pieces, i, n = [], 0, len(src)
    while i < n:
        j = i + 1
        if src[i] < 0:
            while j < n and src[j] < 0:
                j += 1
            pieces.append(jnp.zeros(w.shape[:-1] + (j - i,), w.dtype))
        else:
            while j < n and src[j] == src[i] + (j - i):
                j += 1
            pieces.append(w[..., int(src[i]):int(src[i]) + j - i])
        i = j
    return jnp.concatenate(pieces, axis=-1)


def _pad_rows(w, rows):
    return jnp.pad(w, ((0, rows - w.shape[0]), (0, 0)))


def _in_proj_kernel(x_ref, w_ref, z_ref, xb_ref):
    @pl.when(pl.program_id(1) == 0)
    def _():
        xb_ref[...] = x_ref[...].astype(BF16)

    z_ref[...] = jnp.dot(xb_ref[...], w_ref[...], preferred_element_type=F32).astype(z_ref.dtype)


def _in_proj(x2, w_bf, tm=1024, tn=512):
    n_tok, d = x2.shape
    n_cols = w_bf.shape[1]
    return pl.pallas_call(
        _in_proj_kernel,
        out_shape=jax.ShapeDtypeStruct((n_tok, n_cols), BF16),
        grid=(n_tok // tm, n_cols // tn),
        in_specs=[pl.BlockSpec((tm, d), lambda i, j: (i, 0)),
                  pl.BlockSpec((d, tn), lambda i, j: (0, j))],
        out_specs=pl.BlockSpec((tm, tn), lambda i, j: (i, j)),
        scratch_shapes=[pltpu.VMEM((tm, d), BF16)],
        compiler_params=pltpu.CompilerParams(dimension_semantics=("parallel", "arbitrary"),
                                             vmem_limit_bytes=VMEM_LIMIT),
        name="in_proj",
    )(x2, w_bf)


def _compress_kernel(kv_ref, pe_ref, w1_ref, w2_ref, out_ref, kv32_ref):
    n_chunks = kv_ref.shape[1] // CMP_STRIDE
    kv32_ref[...] = kv_ref[0].astype(F32)
    acc_lo = jnp.zeros((n_chunks, LANE), F32)
    acc_hi = jnp.zeros((n_chunks, LANE), F32)
    for i in range(CMP_STRIDE):
        rows = kv32_ref[pl.ds(i, n_chunks, stride=CMP_STRIDE), :]
        lo = (rows + pe_ref[i:i + 1, :]).astype(BF16)
        hi = (rows + pe_ref[CMP_STRIDE + i:CMP_STRIDE + i + 1, :]).astype(BF16)
        acc_lo += jnp.dot(lo, w1_ref[i], preferred_element_type=F32)
        acc_hi += jnp.dot(hi, w1_ref[CMP_STRIDE + i], preferred_element_type=F32)
    shifted = jnp.concatenate([acc_hi[1:], jnp.zeros((1, LANE), F32)], axis=0)
    hid = jax.nn.gelu(acc_lo + shifted)
    out = jnp.dot(hid.astype(BF16), w2_ref[...], preferred_element_type=F32)
    row = lax.broadcasted_iota(jnp.int32, out.shape, 0)
    out_ref[0, 0] = jnp.where(row < n_chunks - 1, out, 0.0).astype(out_ref.dtype)


def _compress(z3, pe, w1, w2):
    b, t_len, _ = z3.shape
    n_chunks = t_len // CMP_STRIDE
    return pl.pallas_call(
        _compress_kernel,
        out_shape=jax.ShapeDtypeStruct((b, NSA_KV_HEADS, n_chunks, LANE), BF16),
        grid=(b, NSA_KV_HEADS),
        in_specs=[pl.BlockSpec((1, t_len, LANE), lambda bi, g: (bi, 0, ZB_KVC + g)),
                  pl.BlockSpec((CMP_BLOCK, LANE), lambda bi, g: (0, 0)),
                  pl.BlockSpec((CMP_BLOCK, LANE, LANE), lambda bi, g: (0, 0, 0)),
                  pl.BlockSpec((LANE, LANE), lambda bi, g: (0, 0))],
        out_specs=pl.BlockSpec((1, 1, n_chunks, LANE), lambda bi, g: (bi, g, 0, 0)),
        scratch_shapes=[pltpu.VMEM((t_len, LANE), F32)],
        compiler_params=pltpu.CompilerParams(dimension_semantics=("parallel", "parallel"),
                                             vmem_limit_bytes=VMEM_LIMIT),
        name="kv_compress",
    )(z3, pe, w1, w2)


def _block_diag2(a, b):
    za = jnp.zeros(a.shape[:-1] + (b.shape[-1],), a.dtype)
    zb = jnp.zeros(b.shape[:-1] + (a.shape[-1],), b.dtype)
    return jnp.concatenate([jnp.concatenate([a, za], axis=-1), jnp.concatenate([zb, b], axis=-1)], axis=-2)


def _stack_heads(x):
    return jnp.concatenate([x[:, r * HEAD_DIM:(r + 1) * HEAD_DIM] for r in range(NSA_GROUP)], axis=0)


def _dot_nt(a, b):
    return lax.dot_general(a, b, (((1,), (1,)), ((), ())), preferred_element_type=F32)


def _eye(n, dtype):
    return jnp.where(lax.broadcasted_iota(jnp.int32, (n, n), 0) == lax.broadcasted_iota(jnp.int32, (n, n), 1),
                     1.0, 0.0).astype(dtype)


NSA_KEY_CHUNK = 256
NSA_V_ROWS = HEAD_DIM + 16


def _nsa_kernel(slopes_ref, *refs, n_sel, n_g):
    q_refs = refs[:n_g]
    (kvc_ref, kvs_ref, kvw_ref, gate_ref, ovl_ref, o_ref, vst_ref, vwt_ref, vct_ref, sel_ref,
     sc0_ref, sc1_ref, p0_ref, p1_ref) = refs[n_g:]
    g_base = pl.program_id(1) * n_g
    qi = pl.program_id(2)
    q0 = qi * Q_BLOCK
    cols = NSA_GROUP * Q_BLOCK
    kc_len = NSA_KEY_CHUNK
    blocks_per_chunk = kc_len // SEL_BLOCK
    n_chunks = kvs_ref.shape[1] // kc_len
    n_cmp_pad = kvc_ref.shape[2]
    n_slc = ovl_ref.shape[0]
    v_rows = vst_ref.shape[2]
    eye_dh = _eye(HEAD_DIM, BF16)
    k_lanes = lambda gg: slice(gg * LANE, gg * LANE + HEAD_DIM)
    v_lanes = lambda gg: slice(gg * LANE + HEAD_DIM, (gg + 1) * LANE)

    @pl.when(qi == 0)
    def _():
        ones_row = jnp.where(lax.broadcasted_iota(jnp.int32, (v_rows - HEAD_DIM, kc_len), 0) == 0,
                             1.0, 0.0).astype(BF16)

        def body(c, carry):
            rows = pl.ds(pl.multiple_of(c * kc_len, kc_len), kc_len)
            for gg in range(n_g):
                vst_ref[gg, c, 0:HEAD_DIM] = _dot_nt(eye_dh, kvs_ref[0, rows, v_lanes(gg)]).astype(BF16)
                vwt_ref[gg, c, 0:HEAD_DIM] = _dot_nt(eye_dh, kvw_ref[0, rows, v_lanes(gg)]).astype(BF16)
                vst_ref[gg, c, HEAD_DIM:v_rows] = ones_row
                vwt_ref[gg, c, HEAD_DIM:v_rows] = ones_row
            return carry

        lax.fori_loop(0, n_chunks, body, 0)
        for gg in range(n_g):
            vct_ref[gg] = _dot_nt(eye_dh, kvc_ref[0, gg, :, HEAD_DIM:2 * HEAD_DIM]).astype(BF16)

    log2e = float(np.log2(np.e))
    lane = lax.broadcasted_iota(jnp.int32, (1, cols), 1)
    head = lane // Q_BLOCK
    tq = q0 + lane % Q_BLOCK
    sub = lax.broadcasted_iota(jnp.int32, (kc_len, cols), 0)
    half = lax.broadcasted_iota(jnp.int32, (1, LANE), 1) // Q_BLOCK
    blk = lax.broadcasted_iota(jnp.int32, (n_slc, LANE), 0)
    blk8 = lax.broadcasted_iota(jnp.int32, (8, LANE), 0)
    cmp_end = lax.broadcasted_iota(jnp.int32, (n_cmp_pad, 1), 0) * CMP_STRIDE + (CMP_BLOCK - 1)
    d_cmp = tq - cmp_end
    m_cmp = d_cmp >= 0
    d_cmp_f = d_cmp.astype(F32)
    forced = (blk == 0) | (blk == qi) | (blk == qi - 1)
    ovl = ovl_ref[...]

    groups = range(n_g)
    qs, slopes = [], []
    for gg in groups:
        q = _stack_heads(q_refs[gg][0]).astype(F32)
        qs.append((q * (HEAD_DIM ** -0.5 * log2e)).astype(BF16))
        slope = jnp.zeros((1, cols), F32)
        for r in range(NSA_GROUP):
            slope = jnp.where(head == r, slopes_ref[(g_base + gg) * NSA_GROUP + r] * log2e, slope)
        slopes.append(slope)
    st = [_dot_nt(kvc_ref[0, gg, :, 0:HEAD_DIM], qs[gg]) for gg in groups]
    p_cmp = []
    for gg in groups:
        s_m = jnp.where(m_cmp, st[gg] - slopes[gg] * d_cmp_f, NEG_INF)
        e = jnp.where(m_cmp, jnp.exp2(s_m - jnp.max(s_m, axis=0, keepdims=True)), 0.0)
        p_cmp.append(e / jnp.maximum(jnp.sum(e, axis=0, keepdims=True), 1e-30))
    o_cmp = [jnp.dot(vct_ref[gg], p_cmp[gg].astype(BF16), preferred_element_type=F32) for gg in groups]
    parts = []
    for gg in groups:
        y = p_cmp[gg][:, 0:LANE] + p_cmp[gg][:, LANE:2 * LANE]
        parts.append(_split3(y + pltpu.roll(y, Q_BLOCK, 1)))
    pooled = [[jnp.dot(ovl, parts[gg][i], preferred_element_type=F32) for gg in groups] for i in range(3)]
    for gg in groups:
        imp = pooled[0][gg] + pooled[1][gg] + pooled[2][gg]
        imp = jnp.where(forced, jnp.inf, jnp.where(blk > qi, -jnp.inf, imp))
        tiles = [imp[t:t + 8] for t in range(0, n_slc, 8)]
        ranks = [jnp.zeros((8, LANE), F32) for _ in tiles]
        for jp in range(0, n_slc, 2):
            row = jnp.where(half == 0, imp[jp:jp + 1, :], imp[jp + 1:jp + 2, :])
            for ti, tile in enumerate(tiles):
                t0 = ti * 8
                if t0 > jp + 1:
                    hit = jnp.where(row >= tile, 1.0, 0.0)
                elif t0 + 7 <= jp:
                    hit = jnp.where(row > tile, 1.0, 0.0)
                else:
                    hit = jnp.where(blk8 + t0 > jp + half, jnp.where(row >= tile, 1.0, 0.0),
                                    jnp.where(row > tile, 1.0, 0.0))
                ranks[ti] = ranks[ti] + hit
        rank = jnp.concatenate(ranks, axis=0)
        rank = rank + pltpu.roll(rank, Q_BLOCK, 1)
        neg = jnp.where((rank < n_sel) & (blk <= qi), 0.0, NEG_INF)
        neg2 = jnp.concatenate([neg, neg], axis=1)
        for j in range(n_slc):
            sel_ref[gg, j] = jnp.broadcast_to(neg2[j:j + 1, :], (8, cols))
    sub_f = sub.astype(F32)
    fronts = [(qs[gg], slopes[gg], slopes[gg] * sub_f, o_cmp[gg]) for gg in groups]

    init1 = (jnp.full((1, cols), NEG_INF, F32), jnp.zeros((v_rows, cols), F32))

    def block_mask(gg, c):
        tiles = [sel_ref[gg, c * blocks_per_chunk + i] for i in range(blocks_per_chunk)]
        return jnp.concatenate([t for t in tiles for _ in range(SEL_BLOCK // 8)], axis=0)

    streams = {"slc": (kvs_ref, vst_ref), "win": (kvw_ref, vwt_ref)}
    sc_slots = (sc0_ref, sc1_ref)
    p_slots = (p0_ref, p1_ref)

    def score(gg, pos, slot):
        stream, c, _ = pos
        rows = pl.ds(pl.multiple_of(c * kc_len, kc_len), kc_len)
        sc_slots[slot][gg] = _dot_nt(streams[stream][0][0, rows, k_lanes(gg)], fronts[gg][0])

    def softmax(gg, pos, slot, m_prev):
        _, c, mask_fn = pos
        _, slope, bias_local, _ = fronts[gg]
        sc = sc_slots[slot][gg] + bias_local + mask_fn(gg)
        shift = slope * (c * kc_len - q0).astype(F32)
        m_new = jnp.maximum(m_prev, jnp.max(sc, axis=0, keepdims=True) + shift)
        p_slots[slot][gg] = jnp.exp2(sc - (m_new - shift)).astype(BF16)
        return m_new, jnp.exp2(m_prev - m_new)

    def weigh(gg, pos, slot, alpha, acc):
        stream, c, _ = pos
        return alpha * acc + jnp.dot(streams[stream][1][gg, c], p_slots[slot][gg], preferred_element_type=F32)

    def step(pos, slot, nxt, prev, state, alphas):
        if nxt is not None:
            for gg in range(n_g):
                score(gg, nxt, 1 - slot)
        state = {s: list(v) for s, v in state.items()}
        new_alphas = []
        for gg in range(n_g):
            m_prev, acc = state[pos[0]][gg]
            m_new, alpha = softmax(gg, pos, slot, m_prev)
            state[pos[0]][gg] = (m_new, acc)
            new_alphas.append(alpha)
            if prev is not None:
                m_p, acc_p = state[prev[0]][gg]
                state[prev[0]][gg] = (m_p, weigh(gg, prev, 1 - slot, alphas[gg], acc_p))
        return state, new_alphas

    c_cur = qi // blocks_per_chunk
    n_pairs = c_cur // 2
    plain = lambda c: ("slc", c, lambda gg: block_mask(gg, c))

    for gg in range(n_g):
        p1_ref[gg] = jnp.zeros((kc_len, cols), BF16)
        score(gg, plain(0), 0)

    def pair_body(i, carry):
        slc_state, alphas = carry
        k0 = 2 * i
        state, alphas = step(plain(k0), 0, plain(k0 + 1), plain(jnp.maximum(k0 - 1, 0)),
                             {"slc": slc_state}, alphas)
        state, alphas = step(plain(k0 + 1), 1, plain(k0 + 2), plain(k0), state, alphas)
        return tuple(state["slc"]), tuple(alphas)

    slc_state, alphas = lax.fori_loop(0, n_pairs, pair_body,
                                      (tuple(init1 for _ in range(n_g)),
                                       tuple(jnp.ones((1, cols), F32) for _ in range(n_g))))
    k_t = 2 * n_pairs
    odd_neg = jnp.where(c_cur % 2 == 1, 0.0, NEG_INF)
    causal_neg = jnp.where((c_cur * kc_len + sub) <= tq, 0.0, NEG_INF)
    tail = [("slc", k_t, lambda gg: block_mask(gg, k_t) + odd_neg),
            ("slc", c_cur, lambda gg: block_mask(gg, c_cur) + causal_neg)]
    for i in range(WINDOW // kc_len, -1, -1):
        c_raw = c_cur - i
        c_win = jnp.maximum(c_raw, 0)
        dist = tq - (c_win * kc_len + sub)
        band_neg = jnp.where((dist >= 0) & (dist < WINDOW) & (c_raw >= 0), 0.0, NEG_INF)
        tail.append(("win", c_win, lambda gg, band_neg=band_neg: band_neg))
    state = {"slc": list(slc_state), "win": [init1 for _ in range(n_g)]}
    prev = plain(jnp.maximum(k_t - 1, 0))
    for idx, pos in enumerate(tail):
        nxt = tail[idx + 1] if idx + 1 < len(tail) else None
        state, alphas = step(pos, idx % 2, nxt, prev, state, alphas)
        prev = pos
    last_slot = (len(tail) - 1) % 2
    for gg in range(n_g):
        m_p, acc_p = state[prev[0]][gg]
        state[prev[0]][gg] = (m_p, weigh(gg, prev, last_slot, alphas[gg], acc_p))

    eye_q = _eye(Q_BLOCK, BF16)
    eye_lane = _eye(LANE, BF16)
    gates = [jax.nn.sigmoid(_dot_nt(eye_lane, gate_ref[0, :, gg * LANE:(gg + 1) * LANE])) for gg in groups]

    def gate_row(gg, i):
        return jnp.concatenate([gates[gg][i * NSA_GROUP + r:i * NSA_GROUP + r + 1, :] for r in range(NSA_GROUP)],
                               axis=1)

    def normalized(acc):
        return acc[0:HEAD_DIM] / jnp.maximum(acc[HEAD_DIM:HEAD_DIM + 1], 1e-30)

    o_rows = []
    for gg in groups:
        o = (gate_row(gg, 0) * fronts[gg][3] + gate_row(gg, 1) * normalized(state["slc"][gg][1])
             + gate_row(gg, 2) * normalized(state["win"][gg][1])).astype(BF16)
        o_rows.append(jnp.concatenate([o[:, r * Q_BLOCK:(r + 1) * Q_BLOCK] for r in range(NSA_GROUP)], axis=0))
    outs = [_dot_nt(eye_q, o_rows[gg]) for gg in groups]
    for gg in groups:
        o_ref[0, :, gg * cols:(gg + 1) * cols] = outs[gg].astype(o_ref.dtype)


NSA_GROUPS_PER_STEP = 4


def _nsa_attention(z3, kv_cmp):
    b, t_len, _ = z3.shape
    n_q = t_len // Q_BLOCK
    n_slc = t_len // SEL_BLOCK
    n_cmp_pad = kv_cmp.shape[2]
    n_sel = min(N_SELECT, n_slc)
    n_g = NSA_GROUPS_PER_STEP
    cols = NSA_GROUP * Q_BLOCK
    slopes = jnp.exp2(-8.0 * jnp.arange(1, NSA_HEADS + 1, dtype=F32) / NSA_HEADS)
    c_idx = np.arange(n_cmp_pad)
    cmp_start = c_idx * CMP_STRIDE
    cmp_end = cmp_start + CMP_BLOCK - 1
    slc_start = np.arange(n_slc) * SEL_BLOCK
    overlap = np.clip(np.minimum(cmp_end[None, :], slc_start[:, None] + SEL_BLOCK - 1)
                      - np.maximum(cmp_start[None, :], slc_start[:, None]) + 1, 0, None).astype(np.float32)
    overlap[:, c_idx >= t_len // CMP_STRIDE - CMP_BLOCK // CMP_STRIDE + 1] = 0.0
    kernel = functools.partial(_nsa_kernel, n_sel=n_sel, n_g=n_g)
    n_kc = t_len // NSA_KEY_CHUNK
    q_spec = lambda gg: pl.BlockSpec((1, Q_BLOCK, cols),
                                     lambda bi, g, qi: (bi, qi, ZB_Q * LANE // cols + g * n_g + gg))
    slab = lambda zb: pl.BlockSpec((1, t_len, n_g * LANE), lambda bi, g, qi: (bi, 0, zb // n_g + g))
    assert ZB_KVS % n_g == 0 and ZB_KVW % n_g == 0 and ZB_GATE % n_g == 0
    return pl.pallas_call(
        kernel,
        out_shape=jax.ShapeDtypeStruct((b, t_len, NSA_WIDTH), BF16),
        grid=(b, NSA_KV_HEADS // n_g, n_q),
        in_specs=[pl.BlockSpec(memory_space=pltpu.SMEM)] + [q_spec(gg) for gg in range(n_g)] + [
            pl.BlockSpec((1, n_g, n_cmp_pad, LANE), lambda bi, g, qi: (bi, g, 0, 0)),
            slab(ZB_KVS), slab(ZB_KVW),
            pl.BlockSpec((1, Q_BLOCK, n_g * LANE), lambda bi, g, qi: (bi, qi, ZB_GATE // n_g + g)),
            pl.BlockSpec((n_slc, n_cmp_pad), lambda bi, g, qi: (0, 0)),
        ],
        out_specs=pl.BlockSpec((1, Q_BLOCK, n_g * cols), lambda bi, g, qi: (bi, qi, g)),
        scratch_shapes=[pltpu.VMEM((n_g, n_kc, NSA_V_ROWS, NSA_KEY_CHUNK), BF16),
                        pltpu.VMEM((n_g, n_kc, NSA_V_ROWS, NSA_KEY_CHUNK), BF16),
                        pltpu.VMEM((n_g, HEAD_DIM, n_cmp_pad), BF16),
                        pltpu.VMEM((n_g, n_slc, 8, cols), F32),
                        pltpu.VMEM((n_g, NSA_KEY_CHUNK, cols), F32),
                        pltpu.VMEM((n_g, NSA_KEY_CHUNK, cols), F32),
                        pltpu.VMEM((n_g, NSA_KEY_CHUNK, cols), BF16),
                        pltpu.VMEM((n_g, NSA_KEY_CHUNK, cols), BF16)],
        compiler_params=pltpu.CompilerParams(dimension_semantics=("parallel", "parallel", "arbitrary"),
                                             vmem_limit_bytes=VMEM_LIMIT),
        name="nsa_attention",
    )(slopes, *([z3] * n_g), kv_cmp, z3, z3, z3, jnp.asarray(overlap, BF16))


def _nsa_prepare(cmp_pe_k, cmp_w1_k, cmp_w2_k, cmp_pe_v, cmp_w1_v, cmp_w2_v):
    pe = jnp.concatenate([cmp_pe_k, cmp_pe_v], axis=-1)
    w1 = _block_diag2(cmp_w1_k.reshape(CMP_BLOCK, HEAD_DIM, HEAD_DIM),
                      cmp_w1_v.reshape(CMP_BLOCK, HEAD_DIM, HEAD_DIM)).astype(BF16)
    w2 = _block_diag2(cmp_w2_k, cmp_w2_v).astype(BF16)
    return pe, w1, w2


RWKV_CHUNK = 64


def _rwkv_prep_kernel(z_ref, mu_ref, w0_ref, wup_ref, a0_ref, aup_ref, gup_ref, kk_ref, ka_ref,
                      r_ref, k_ref, v_ref, kkr_ref, a_ref, lw_ref, g_ref, carry_ref):
    w = RWKV_WIDTH

    @pl.when(pl.program_id(1) == 0)
    def _():
        carry_ref[...] = jnp.zeros_like(carry_ref)

    z = z_ref[0].astype(F32)
    tc = z.shape[0]
    row = lax.broadcasted_iota(jnp.int32, z.shape, 0)
    prev = jnp.where(row == 0, carry_ref[0:1, :], pltpu.roll(z, 1, 0))
    carry_ref[0:1, :] = z[tc - 1:tc, :]
    zs = z + (prev - z) * mu_ref[...]
    r = zs[:, RB_R * LANE:RB_R * LANE + w]
    k = zs[:, RB_K * LANE:RB_K * LANE + w]
    v = zs[:, RB_V * LANE:RB_V * LANE + w]
    w_lo = zs[:, RB_WLO * LANE:(RB_WLO + 1) * LANE]
    a_lo = zs[:, RB_ALO * LANE:(RB_ALO + 1) * LANE]
    g_lo = zs[:, RB_GLO * LANE:RB_GLO * LANE + GATE_LORA]
    d = w0_ref[...] + jnp.dot(jnp.tanh(w_lo).astype(BF16), wup_ref[...], preferred_element_type=F32)
    w_raw = -jax.nn.softplus(-d) - 0.5
    lw_ref[0] = -jnp.exp(w_raw)
    a = jax.nn.sigmoid(a0_ref[...] + jnp.dot(a_lo.astype(BF16), aup_ref[...], preferred_element_type=F32))
    g = jnp.dot(jax.nn.sigmoid(g_lo).astype(BF16), gup_ref[...], preferred_element_type=F32)
    r_ref[0] = r.astype(r_ref.dtype)
    v_ref[0] = v.astype(v_ref.dtype)
    kkr_ref[0] = (k * kk_ref[...]).astype(kkr_ref.dtype)
    k_ref[0] = (k * (1.0 + (a - 1.0) * ka_ref[...])).astype(k_ref.dtype)
    a_ref[0] = a.astype(a_ref.dtype)
    g_ref[0] = g.astype(g_ref.dtype)


def _rwkv_prep(z3, mu, w0, w_up, a0, a_up, g_up, k_k, k_a, tc=256):
    b, t_len, _ = z3.shape
    w = RWKV_WIDTH
    ncol = RWKV_BLOCKS * LANE
    vec = lambda n: pl.BlockSpec((1, n), lambda bi, ti: (0, 0))
    mat = lambda m, n: pl.BlockSpec((m, n), lambda bi, ti: (0, 0))
    out_bf = jax.ShapeDtypeStruct((b, t_len, w), BF16)
    out_f32 = jax.ShapeDtypeStruct((b, t_len, w), F32)
    out_spec = pl.BlockSpec((1, tc, w), lambda bi, ti: (bi, ti, 0))
    return pl.pallas_call(
        _rwkv_prep_kernel,
        out_shape=(out_bf, out_bf, out_bf, out_bf, out_bf, out_f32, out_bf),
        grid=(b, t_len // tc),
        in_specs=[pl.BlockSpec((1, tc, ncol), lambda bi, ti: (bi, ti, ZB_RWKV)),
                  vec(ncol), vec(w), mat(LANE, w), vec(w), mat(LANE, w), mat(GATE_LORA, w), vec(w), vec(w)],
        out_specs=(out_spec,) * 7,
        scratch_shapes=[pltpu.VMEM((8, ncol), F32)],
        compiler_params=pltpu.CompilerParams(dimension_semantics=("parallel", "arbitrary"),
                                             vmem_limit_bytes=VMEM_LIMIT),
        name="rwkv_prep",
    )(z3, mu, w0, w_up, a0, a_up, g_up, k_k, k_a)


def _pair_blocks(x):
    lane = lax.broadcasted_iota(jnp.int32, x.shape, 1)
    zero = jnp.zeros((), x.dtype)
    return jnp.concatenate([jnp.where(lane < HEAD_DIM, x, zero), jnp.where(lane >= HEAD_DIM, x, zero)], axis=0)


def _fold_pair(x):
    n = x.shape[0] // 2
    return x[:n] + x[n:]


def _split3(x):
    hi = x.astype(BF16)
    r1 = x - hi.astype(F32)
    mid = r1.astype(BF16)
    return hi, mid, (r1 - mid.astype(F32)).astype(BF16)


def _dot_split_rhs(a_bf, x):
    hi, mid, lo = _split3(x)
    return (jnp.dot(a_bf, hi, preferred_element_type=F32) + jnp.dot(a_bf, mid, preferred_element_type=F32)
            + jnp.dot(a_bf, lo, preferred_element_type=F32))


def _dot_split_lhs(x, b_bf):
    hi, mid, lo = _split3(x)
    return (jnp.dot(hi, b_bf, preferred_element_type=F32) + jnp.dot(mid, b_bf, preferred_element_type=F32)
            + jnp.dot(lo, b_bf, preferred_element_type=F32))


RWKV_PAIRS_PER_STEP = 4


def _rwkv_scan_kernel(r_ref, k_ref, v_ref, kkr_ref, a_ref, lw_ref, g_ref, rk_ref, lng_ref, lnb_ref,
                      o_ref, s_ref, *, n_chunks, n_pairs):
    L = RWKV_CHUNK
    L2 = 2 * L

    @pl.when(pl.program_id(2) == 0)
    def _():
        s_ref[...] = jnp.zeros_like(s_ref)

    ri = lax.broadcasted_iota(jnp.int32, (L, L), 0)
    ci = lax.broadcasted_iota(jnp.int32, (L, L), 1)
    tri_incl = jnp.where(ri >= ci, 1.0, 0.0).astype(BF16)
    r2 = lax.broadcasted_iota(jnp.int32, (L2, L2), 0)
    c2 = lax.broadcasted_iota(jnp.int32, (L2, L2), 1)
    same_head = (r2 // L) == (c2 // L)
    strict2 = same_head & (r2 > c2)
    incl2 = same_head & (r2 >= c2)
    eye2 = jnp.where(r2 == c2, 1.0, 0.0).astype(F32)
    h_r = lax.broadcasted_iota(jnp.int32, (LANE, LANE), 0) // HEAD_DIM
    h_c = lax.broadcasted_iota(jnp.int32, (LANE, LANE), 1) // HEAD_DIM
    head_mask = h_r == h_c
    head_ones = jnp.where(head_mask, 1.0, 0.0).astype(BF16)

    def head_sum(x):
        return _dot_split_lhs(x, head_ones)

    def mm(a, b):
        return jnp.dot(a.astype(BF16), b.astype(BF16), preferred_element_type=F32)

    chains = [(hp, c) for hp in range(n_pairs) for c in range(n_chunks)]
    lanes_of = lambda hp: slice(hp * LANE, (hp + 1) * LANE)
    rows_of = lambda c: pl.ds(c * L, L)
    each = lambda fn: {ch: fn(ch) for ch in chains}

    load = lambda ref: each(lambda ch: ref[0, rows_of(ch[1]), lanes_of(ch[0])])
    r, k, v, a = (each(lambda ch, d=d: d[ch].astype(F32)) for d in (load(r_ref), load(k_ref), load(v_ref), load(a_ref)))
    kkr = each(lambda ch, d=load(kkr_ref): d[ch].astype(F32))
    lw = load(lw_ref)
    kk_sq = each(lambda ch: head_sum(kkr[ch] * kkr[ch]))
    cl = each(lambda ch: _dot_split_rhs(tri_incl, lw[ch]))
    kk = each(lambda ch: kkr[ch] / jnp.maximum(jnp.sqrt(kk_sq[ch]), 1e-12))
    p_incl = each(lambda ch: jnp.exp(cl[ch]))
    p_inv = each(lambda ch: jnp.exp(-cl[ch]))
    rt = each(lambda ch: (r[ch] * p_incl[ch]).astype(BF16))
    at = each(lambda ch: (-kk[ch] * jnp.exp(cl[ch] - lw[ch])).astype(BF16))
    kt = each(lambda ch: k[ch] * p_inv[ch])
    bt = each(lambda ch: kk[ch] * a[ch] * p_inv[ch])
    gram = each(lambda ch: _dot_nt(
        jnp.concatenate([_pair_blocks(at[ch]), _pair_blocks(rt[ch])], axis=0),
        jnp.concatenate([_pair_blocks(bt[ch]), _pair_blocks(kt[ch])], axis=0).astype(BF16)))
    a_ab = each(lambda ch: jnp.where(strict2, gram[ch][:L2, :L2], 0.0))
    a_ak = each(lambda ch: jnp.where(strict2, gram[ch][:L2, L2:], 0.0).astype(BF16))
    a_rb = each(lambda ch: jnp.where(incl2, gram[ch][L2:, :L2], 0.0).astype(BF16))
    a_rk = each(lambda ch: jnp.where(incl2, gram[ch][L2:, L2:], 0.0).astype(BF16))
    tinv = each(lambda ch: eye2 + a_ab[ch])
    apow = a_ab
    for _ in range(int(np.log2(L)) - 1):
        apow = each(lambda ch: mm(apow[ch], apow[ch]))
        tinv = each(lambda ch: tinv[ch] + mm(tinv[ch], apow[ch]))
    tinv = each(lambda ch: tinv[ch].astype(BF16))
    v_bf = each(lambda ch: v[ch].astype(BF16))
    v2 = each(lambda ch: _pair_blocks(v_bf[ch]))
    akv = each(lambda ch: _fold_pair(jnp.dot(a_ak[ch], v2[ch], preferred_element_type=F32)))
    rkv = each(lambda ch: _fold_pair(jnp.dot(a_rk[ch], v2[ch], preferred_element_type=F32)))
    p_last = each(lambda ch: p_incl[ch][L - 1:L, :])
    wts = each(lambda ch: jnp.concatenate([bt[ch] * p_last[ch], kt[ch] * p_last[ch]], axis=0).astype(BF16))
    bonus = each(lambda ch: head_sum(r[ch] * k[ch] * rk_ref[:, lanes_of(ch[0])]) * v[ch])

    pairs = range(n_pairs)
    states = [s_ref[hp] for hp in pairs]
    for c in range(n_chunks):
        s_bf = [states[hp].astype(BF16) for hp in pairs]
        m = [_dot_nt(at[hp, c], s_bf[hp]) + akv[hp, c] for hp in pairs]
        y = [_dot_nt(rt[hp, c], s_bf[hp]) + rkv[hp, c] for hp in pairs]
        u = [_fold_pair(jnp.dot(tinv[hp, c], _pair_blocks(m[hp].astype(BF16)), preferred_element_type=F32))
             for hp in pairs]
        u_bf = [u[hp].astype(BF16) for hp in pairs]
        upd = [lax.dot_general(jnp.concatenate([u_bf[hp], v_bf[hp, c]], axis=0), wts[hp, c],
                               (((0,), (0,)), ((), ())), preferred_element_type=F32) for hp in pairs]
        states = [states[hp] * p_last[hp, c] + jnp.where(head_mask, upd[hp], 0.0) for hp in pairs]
        y = [y[hp] + _fold_pair(jnp.dot(a_rb[hp, c], _pair_blocks(u_bf[hp]), preferred_element_type=F32))
             for hp in pairs]
        mean = [head_sum(y[hp]) * (1.0 / HEAD_DIM) for hp in pairs]
        yc = [y[hp] - mean[hp] for hp in pairs]
        var = [head_sum(yc[hp] * yc[hp]) * (1.0 / HEAD_DIM) for hp in pairs]
        for hp in pairs:
            yn = yc[hp] * lax.rsqrt(var[hp] + GN_EPS) * lng_ref[:, lanes_of(hp)] + lnb_ref[:, lanes_of(hp)]
            o_ref[0, rows_of(c), lanes_of(hp)] = (
                (yn + bonus[hp, c]) * g_ref[0, rows_of(c), lanes_of(hp)].astype(F32)).astype(o_ref.dtype)
    for hp in pairs:
        s_ref[hp] = states[hp]


def _rwkv_scan(r, k, v, kkr, a, lw, g, r_k, ln_g, ln_b, tt=256):
    b, t_len, w = r.shape
    n_p = RWKV_PAIRS_PER_STEP
    width = n_p * LANE
    tile = pl.BlockSpec((1, tt, width), lambda bi, hp, ti: (bi, ti, hp))
    vec = pl.BlockSpec((1, width), lambda bi, hp, ti: (0, hp))
    kernel = functools.partial(_rwkv_scan_kernel, n_chunks=tt // RWKV_CHUNK, n_pairs=n_p)
    return pl.pallas_call(
        kernel,
        out_shape=jax.ShapeDtypeStruct((b, t_len, w), BF16),
        grid=(b, w // width, t_len // tt),
        in_specs=[tile] * 7 + [vec] * 3,
        out_specs=tile,
        scratch_shapes=[pltpu.VMEM((n_p, LANE, LANE), F32)],
        compiler_params=pltpu.CompilerParams(dimension_semantics=("parallel", "parallel", "arbitrary"),
                                             vmem_limit_bytes=VMEM_LIMIT),
        name="rwkv_scan",
    )(r, k, v, kkr, a, lw, g, r_k, ln_g, ln_b)


def _rwkv_time_mix(z3, mu, w0, w_up, a0, a_up, g_up, k_k, k_a, r_k, ln_g, ln_b):
    row = lambda u: u.reshape(1, -1)
    mu_p = row(_take_columns(mu, _rwkv_source_columns()))
    r, k, v, kkr, a, lw, g = _rwkv_prep(z3, mu_p, row(w0), _pad_rows(w_up, LANE).astype(BF16), row(a0),
                                        _pad_rows(a_up, LANE).astype(BF16), g_up.astype(BF16), row(k_k), row(k_a))
    return _rwkv_scan(r, k, v, kkr, a, lw, g, row(r_k), row(ln_g), row(ln_b))


def _layer_norm(x, g, b):
    mean = jnp.mean(x, axis=-1, keepdims=True)
    xc = x - mean
    var = jnp.mean(xc * xc, axis=-1, keepdims=True)
    return xc * lax.rsqrt(var + LN_EPS) * g + b


def _out_proj_kernel(on_ref, or_ref, x_ref, wa_ref, wb_ref, g_ref, b_ref, rw_ref, rb_ref,
                     h_ref, idx_ref, wgt_ref, *, alpha):
    mix = (jnp.dot(on_ref[...], wa_ref[...], preferred_element_type=F32)
           + jnp.dot(or_ref[...], wb_ref[...], preferred_element_type=F32))
    h = _layer_norm(alpha * x_ref[...] + mix, g_ref[...], b_ref[...])
    h_ref[...] = h
    logits = jnp.dot(h, rw_ref[...], preferred_element_type=F32, precision=HIGHEST) + rb_ref[...]
    lane = lax.broadcasted_iota(jnp.int32, logits.shape, 1)
    logits = jnp.where(lane < N_EXPERTS, logits, -jnp.inf)
    idx_out = jnp.zeros(logits.shape, jnp.int32)
    val_out = jnp.full(logits.shape, -jnp.inf, F32)
    for k in range(TOP_K):
        best = jnp.max(logits, axis=-1, keepdims=True)
        first = jnp.min(jnp.where(logits == best, lane, LANE), axis=-1, keepdims=True)
        idx_out = jnp.where(lane == k, first, idx_out)
        val_out = jnp.where(lane == k, best, val_out)
        logits = jnp.where(lane == first, -jnp.inf, logits)
    e = jnp.exp(val_out - jnp.max(val_out, axis=-1, keepdims=True))
    idx_ref[...] = idx_out
    wgt_ref[...] = e / jnp.sum(e, axis=-1, keepdims=True)


def _out_proj_router(o_nsa, o_rwkv, x2, w_out, ln_g, ln_b, router_w, router_b, alpha, tm=512):
    n_tok, d = x2.shape
    half = o_nsa.shape[1]
    rw = jnp.pad(router_w, ((0, 0), (0, LANE - N_EXPERTS)))
    rb = jnp.pad(router_b, (0, LANE - N_EXPERTS)).reshape(1, LANE)
    row_blk = lambda n: pl.BlockSpec((tm, n), lambda i: (i, 0))
    full = lambda m, n: pl.BlockSpec((m, n), lambda i: (0, 0))
    return pl.pallas_call(
        functools.partial(_out_proj_kernel, alpha=alpha),
        out_shape=(jax.ShapeDtypeStruct((n_tok, d), F32), jax.ShapeDtypeStruct((n_tok, LANE), jnp.int32),
                   jax.ShapeDtypeStruct((n_tok, LANE), F32)),
        grid=(n_tok // tm,),
        in_specs=[row_blk(half), row_blk(half), row_blk(d), full(half, d), full(half, d), full(1, d), full(1, d),
                  full(d, LANE), full(1, LANE)],
        out_specs=(row_blk(d), row_blk(LANE), row_blk(LANE)),
        compiler_params=pltpu.CompilerParams(dimension_semantics=("parallel",), vmem_limit_bytes=VMEM_LIMIT),
        name="out_proj_router",
    )(o_nsa, o_rwkv, x2, w_out[:half].astype(BF16), w_out[half:].astype(BF16), ln_g.reshape(1, d),
      ln_b.reshape(1, d), rw, rb)


MOE_ITEM_ROWS = 1024
MOE_SUB_ROWS = 256
MOE_F_TILE = 256


def _moe_tables(top_idx, n_items):
    n_tok = top_idx.shape[0]
    flat_e = top_idx.reshape(-1)
    n_assign = flat_e.shape[0]
    onehot = (flat_e[:, None] == jnp.arange(N_EXPERTS, dtype=jnp.int32)[None, :]).astype(jnp.int32)
    csum = jnp.cumsum(onehot, axis=0)
    rank = jnp.take_along_axis(csum, flat_e[:, None], axis=1)[:, 0] - 1
    counts = csum[-1]
    items_e = (counts + MOE_ITEM_ROWS - 1) // MOE_ITEM_ROWS
    items_end = jnp.cumsum(items_e)
    item_start_e = items_end - items_e
    dest = item_start_e[flat_e] * MOE_ITEM_ROWS + rank
    row_tok = jnp.zeros((n_items * MOE_ITEM_ROWS,), jnp.int32).at[dest].set(
        jnp.arange(n_assign, dtype=jnp.int32) // TOP_K)
    item = jnp.arange(n_items, dtype=jnp.int32)
    valid = item < items_end[-1]
    last_e = jnp.max(jnp.where(counts > 0, jnp.arange(N_EXPERTS, dtype=jnp.int32), 0))
    item_e = jnp.minimum(jnp.searchsorted(items_end, item, side="right").astype(jnp.int32), N_EXPERTS - 1)
    item_e = jnp.where(valid, item_e, last_e)
    item_nv = jnp.where(valid, jnp.clip(counts[item_e] - (item - item_start_e[item_e]) * MOE_ITEM_ROWS,
                                        0, MOE_ITEM_ROWS), 0).astype(jnp.int32)
    return dest.reshape(n_tok, TOP_K), row_tok, item_e, item_nv


def _moe_kernel(item_e_ref, item_nv_ref, x_ref, wg_ref, bg_ref, wu_ref, bu_ref, wd_ref, bd_ref, o_ref):
    i = pl.program_id(0)
    f = pl.program_id(1)
    nv = item_nv_ref[i]
    n_sub_max = MOE_ITEM_ROWS // MOE_SUB_ROWS
    n_sub = (nv + MOE_SUB_ROWS - 1) // MOE_SUB_ROWS
    rows = [pl.ds(sb * MOE_SUB_ROWS, MOE_SUB_ROWS) for sb in range(n_sub_max)]

    @pl.when(f == 0)
    def _():
        for sb in range(n_sub_max):
            bias = jnp.broadcast_to(bd_ref[0], (MOE_SUB_ROWS, o_ref.shape[1]))
            o_ref[rows[sb], :] = jnp.where(sb * MOE_SUB_ROWS < nv, bias, 0.0)

    for n in range(1, n_sub_max + 1):
        @pl.when(n_sub == n)
        def _(n=n):
            wg = wg_ref[0].astype(BF16)
            wu = wu_ref[0].astype(BF16)
            wd = wd_ref[0].astype(BF16)
            xs = [x_ref[rows[sb], :] for sb in range(n)]
            gates = [jnp.dot(xs[sb], wg, preferred_element_type=F32) for sb in range(n)]
            ups = [jnp.dot(xs[sb], wu, preferred_element_type=F32) for sb in range(n)]
            for sb in range(n):
                gate = jnp.minimum(gates[sb] + bg_ref[0], SWIGLU_LIMIT)
                up = jnp.clip(ups[sb] + bu_ref[0], -SWIGLU_LIMIT, SWIGLU_LIMIT)
                h = gate * jax.nn.sigmoid(SWIGLU_ALPHA * gate) * (up + 1.0)
                o_ref[rows[sb], :] += jnp.dot(h.astype(BF16), wd, preferred_element_type=F32)


def _moe_experts(xg, item_e, item_nv, w_gate, b_gate, w_up, b_up, w_down, b_down):
    n_rows, d = xg.shape
    n_items = n_rows // MOE_ITEM_ROWS
    n_e, _, d_ff = w_gate.shape
    n_f = d_ff // MOE_F_TILE

    def f_idx(i, f, nv):
        return jnp.where(nv[i] > 0, f, n_f - 1)

    grid_spec = pltpu.PrefetchScalarGridSpec(
        num_scalar_prefetch=2,
        grid=(n_items, n_f),
        in_specs=[
            pl.BlockSpec((MOE_ITEM_ROWS, d), lambda i, f, e, nv: (i, 0)),
            pl.BlockSpec((1, d, MOE_F_TILE), lambda i, f, e, nv: (e[i], 0, f_idx(i, f, nv))),
            pl.BlockSpec((1, 1, MOE_F_TILE), lambda i, f, e, nv: (e[i], 0, f_idx(i, f, nv))),
            pl.BlockSpec((1, d, MOE_F_TILE), lambda i, f, e, nv: (e[i], 0, f_idx(i, f, nv))),
            pl.BlockSpec((1, 1, MOE_F_TILE), lambda i, f, e, nv: (e[i], 0, f_idx(i, f, nv))),
            pl.BlockSpec((1, MOE_F_TILE, d), lambda i, f, e, nv: (e[i], f_idx(i, f, nv), 0)),
            pl.BlockSpec((1, 1, d), lambda i, f, e, nv: (e[i], 0, 0)),
        ],
        out_specs=pl.BlockSpec((MOE_ITEM_ROWS, d), lambda i, f, e, nv: (i, 0)),
    )
    return pl.pallas_call(
        _moe_kernel,
        out_shape=jax.ShapeDtypeStruct((n_rows, d), F32),
        grid_spec=grid_spec,
        compiler_params=pltpu.CompilerParams(dimension_semantics=("parallel", "arbitrary"),
                                             vmem_limit_bytes=VMEM_LIMIT),
        name="moe_experts",
    )(item_e, item_nv, xg, w_gate, b_gate.reshape(n_e, 1, d_ff), w_up, b_up.reshape(n_e, 1, d_ff), w_down,
      b_down.reshape(n_e, 1, d))


COMBINE_ISSUE_UNROLL = 8


def _final_kernel(dest_ref, dest_next_ref, h_ref, eo_ref, wgt_ref, p_ref, g_ref, b_ref, wgate_ref, wple_ref,
                  o_ref, rows_ref, sem_ref, *, alpha):
    tm, d = h_ref.shape
    n_rows = TOP_K * tm
    i = pl.program_id(0)
    slot = i % 2

    def row_copy(idx_ref, r, s):
        return pltpu.make_async_copy(eo_ref.at[pl.ds(idx_ref[0, 0, r], 1)], rows_ref.at[s, pl.ds(r, 1)],
                                     sem_ref.at[s])

    def issue(idx_ref, s):
        def body(j, carry):
            for u in range(COMBINE_ISSUE_UNROLL):
                row_copy(idx_ref, j * COMBINE_ISSUE_UNROLL + u, s).start()
            return carry

        lax.fori_loop(0, n_rows // COMBINE_ISSUE_UNROLL, body, 0)

    @pl.when(i == 0)
    def _():
        issue(dest_ref, 0)

    @pl.when(i + 1 < pl.num_programs(0))
    def _():
        issue(dest_next_ref, 1 - slot)

    pltpu.make_async_copy(eo_ref.at[pl.ds(0, n_rows)], rows_ref.at[slot], sem_ref.at[slot]).wait()
    wgt = wgt_ref[...]
    ffn = wgt[:, 0:1] * rows_ref[slot, 0:tm, :]
    for k in range(1, TOP_K):
        ffn = ffn + wgt[:, k:k + 1] * rows_ref[slot, k * tm:(k + 1) * tm, :]
    h = _layer_norm(alpha * h_ref[...] + ffn, g_ref[...], b_ref[...])
    gate = jax.nn.sigmoid(jnp.dot(h.astype(BF16), wgate_ref[...], preferred_element_type=F32))
    ple = jnp.dot(p_ref[...].astype(BF16), wple_ref[...], preferred_element_type=F32)
    o_ref[...] = h + gate * ple


def _final(h1, eo, dest, top_w, p2, ln_g, ln_b, ple_gate_w, ple_w, alpha, tm=256):
    n_tok, d = h1.shape
    n_tiles = n_tok // tm
    dest_tiles = dest.reshape(n_tiles, tm, TOP_K).transpose(0, 2, 1).reshape(n_tiles, 1, TOP_K * tm)
    row_blk = lambda n: pl.BlockSpec((tm, n), lambda i: (i, 0))
    full = lambda m, n: pl.BlockSpec((m, n), lambda i: (0, 0))
    idx_blk = lambda fn: pl.BlockSpec((1, 1, TOP_K * tm), fn, memory_space=pltpu.SMEM)
    return pl.pallas_call(
        functools.partial(_final_kernel, alpha=alpha),
        out_shape=jax.ShapeDtypeStruct((n_tok, d), F32),
        grid=(n_tiles,),
        in_specs=[idx_blk(lambda i: (i, 0, 0)), idx_blk(lambda i: (jnp.minimum(i + 1, n_tiles - 1), 0, 0)),
                  row_blk(d), pl.BlockSpec(memory_space=pl.ANY), row_blk(LANE), row_blk(p2.shape[1]),
                  full(1, d), full(1, d), full(d, d), full(p2.shape[1], d)],
        out_specs=row_blk(d),
        scratch_shapes=[pltpu.VMEM((2, TOP_K * tm, d), F32), pltpu.SemaphoreType.DMA((2,))],
        compiler_params=pltpu.CompilerParams(dimension_semantics=("arbitrary",), vmem_limit_bytes=VMEM_LIMIT),
        name="combine_ln_ple",
    )(dest_tiles, dest_tiles, h1, eo, top_w, p2, ln_g.reshape(1, d), ln_b.reshape(1, d),
      ple_gate_w.astype(BF16), ple_w.astype(BF16))


def kernel(x, p, w_in, cmp_pe_k, cmp_w1_k, cmp_w2_k, cmp_pe_v, cmp_w1_v, cmp_w2_v, rwkv_mu, rwkv_w0, rwkv_w_up, rwkv_a0, rwkv_a_up, rwkv_g_up, rwkv_k_k, rwkv_k_a, rwkv_r_k, rwkv_ln_g, rwkv_ln_b, w_out, ln1_g, ln1_b, router_w, router_b, exp_w_gate, exp_b_gate, exp_w_up, exp_b_up, exp_w_down, exp_b_down, ln2_g, ln2_b, ple_w, ple_gate_w):
    b, t_len, d = x.shape
    depth = w_in.shape[0]
    alpha = float((2 * depth) ** 0.25)
    n_tok = b * t_len
    n_items = (n_tok * TOP_K) // MOE_ITEM_ROWS + N_EXPERTS
    h = x.reshape(n_tok, d)
    for i in range(depth):
        w_bf = _take_columns(w_in[i].astype(BF16), _z_source_columns())
        z3 = _in_proj(h, w_bf).reshape(b, t_len, Z_COLS)
        pe, w1, w2 = _nsa_prepare(cmp_pe_k[i], cmp_w1_k[i], cmp_w2_k[i], cmp_pe_v[i], cmp_w1_v[i], cmp_w2_v[i])
        kv_cmp = _compress(z3, pe, w1, w2)
        o_nsa = _nsa_attention(z3, kv_cmp)
        o_rwkv = _rwkv_time_mix(z3, rwkv_mu[i], rwkv_w0[i], rwkv_w_up[i], rwkv_a0[i], rwkv_a_up[i], rwkv_g_up[i],
                                rwkv_k_k[i], rwkv_k_a[i], rwkv_r_k[i].reshape(-1), rwkv_ln_g[i], rwkv_ln_b[i])
        h1, top_idx, top_w = _out_proj_router(o_nsa.reshape(n_tok, -1), o_rwkv.reshape(n_tok, -1), h, w_out[i],
                                              ln1_g[i], ln1_b[i], router_w[i], router_b[i], alpha)
        dest, row_tok, item_e, item_nv = _moe_tables(top_idx[:, :TOP_K], n_items)
        xg = jnp.take(h1.astype(BF16), row_tok, axis=0)
        eo = _moe_experts(xg, item_e, item_nv, exp_w_gate[i], exp_b_gate[i], exp_w_up[i], exp_b_up[i],
                          exp_w_down[i], exp_b_down[i])
        h = _final(h1, eo, dest, top_w, p[i].reshape(n_tok, -1), ln2_g[i], ln2_b[i], ple_gate_w[i], ple_w[i], alpha)
    return h.reshape(b, t_len, d)
```

```python
import functools

import numpy as np
import jax
import jax.numpy as jnp
from jax import lax
from jax.experimental import pallas as pl
from jax.experimental.pallas import tpu as pltpu

F32 = jnp.float32
BF16 = jnp.bfloat16
HIGHEST = lax.Precision.HIGHEST

LANE = 128
D_MODEL = 2048
HEAD_DIM = 64
NSA_HEADS = 16
NSA_KV_HEADS = 4
NSA_GROUP = NSA_HEADS // NSA_KV_HEADS
NSA_WIDTH = NSA_HEADS * HEAD_DIM
NSA_KV_WIDTH = NSA_KV_HEADS * HEAD_DIM
CMP_BLOCK = 32
CMP_STRIDE = 16
SEL_BLOCK = 64
N_SELECT = 16
WINDOW = 512
N_GATES = 3
Q_BLOCK = 64
RWKV_HEADS = 16
RWKV_WIDTH = RWKV_HEADS * HEAD_DIM
DECAY_LORA = 96
ICLR_LORA = 96
GATE_LORA = 256
GN_EPS = 64e-5
N_EXPERTS = 32
TOP_K = 4
SWIGLU_ALPHA = 1.702
SWIGLU_LIMIT = 7.0
PLE_DIM = 256
LN_EPS = 1e-5
NEG_INF = -1e30

NSA_COLS = NSA_WIDTH + 6 * NSA_KV_WIDTH + NSA_HEADS * N_GATES
RWKV_COLS = 3 * RWKV_WIDTH + DECAY_LORA + ICLR_LORA + GATE_LORA

RB_R = 0
RB_K = RB_R + RWKV_WIDTH // LANE
RB_V = RB_K + RWKV_WIDTH // LANE
RB_WLO = RB_V + RWKV_WIDTH // LANE
RB_ALO = RB_WLO + 1
RB_GLO = RB_ALO + 1
RWKV_BLOCKS = RB_GLO + GATE_LORA // LANE
ZB_RWKV = 0
ZB_Q = ZB_RWKV + RWKV_BLOCKS
ZB_KVC = ZB_Q + NSA_WIDTH // LANE
ZB_KVS = ZB_KVC + NSA_KV_HEADS
ZB_KVW = ZB_KVS + NSA_KV_HEADS
ZB_GATE = ZB_KVW + NSA_KV_HEADS
Z_BLOCKS = ZB_GATE + NSA_KV_HEADS
Z_COLS = Z_BLOCKS * LANE
assert (ZB_Q * LANE) % (NSA_GROUP * HEAD_DIM) == 0

VMEM_LIMIT = 56 * 1024 * 1024


def _z_source_columns():
    src = np.full((Z_COLS,), -1, np.int64)
    src[ZB_Q * LANE:ZB_Q * LANE + NSA_WIDTH] = np.arange(NSA_WIDTH)
    for branch in range(3):
        k0 = NSA_WIDTH + 2 * branch * NSA_KV_WIDTH
        v0 = k0 + NSA_KV_WIDTH
        for g in range(NSA_KV_HEADS):
            base = (ZB_KVC + branch * NSA_KV_HEADS + g) * LANE
            src[base:base + HEAD_DIM] = k0 + g * HEAD_DIM + np.arange(HEAD_DIM)
            src[base + HEAD_DIM:base + 2 * HEAD_DIM] = v0 + g * HEAD_DIM + np.arange(HEAD_DIM)
    g0 = NSA_WIDTH + 6 * NSA_KV_WIDTH
    for g in range(NSA_KV_HEADS):
        base = (ZB_GATE + g) * LANE
        for i in range(N_GATES):
            for r in range(NSA_GROUP):
                src[base + i * NSA_GROUP + r] = g0 + (g * NSA_GROUP + r) * N_GATES + i
    rwkv = _rwkv_source_columns()
    src[ZB_RWKV * LANE:(ZB_RWKV + RWKV_BLOCKS) * LANE] = np.where(rwkv >= 0, NSA_COLS + rwkv, -1)
    return src


def _rwkv_source_columns():
    src = np.full((RWKV_BLOCKS * LANE,), -1, np.int64)
    src[:3 * RWKV_WIDTH] = np.arange(3 * RWKV_WIDTH)
    src[RB_WLO * LANE:RB_WLO * LANE + DECAY_LORA] = 3 * RWKV_WIDTH + np.arange(DECAY_LORA)
    src[RB_ALO * LANE:RB_ALO * LANE + ICLR_LORA] = 3 * RWKV_WIDTH + DECAY_LORA + np.arange(ICLR_LORA)
    src[RB_GLO * LANE:] = 3 * RWKV_WIDTH + DECAY_LORA + ICLR_LORA + np.arange(GATE_LORA)
    return src


def _take_columns(w, src):
    pieces, i, n = [], 0, len(src)
    while i < n:
        j = i + 1
        if src[i] < 0:
            while j < n and src[j] < 0:
                j += 1
            pieces.append(jnp.zeros(w.shape[:-1] + (j - i,), w.dtype))
        else:
            while j < n and src[j] == src[i] + (j - i):
                j += 1
            pieces.append(w[..., int(src[i]):int(src[i]) + j - i])
        i = j
    return jnp.concatenate(pieces, axis=-1)


def _pad_rows(w, rows):
    return jnp.pad(w, ((0, rows - w.shape[0]), (0, 0)))


def _in_proj_kernel(x_ref, w_ref, z_ref, xb_ref):
    @pl.when(pl.program_id(1) == 0)
    def _():
        xb_ref[...] = x_ref[...].astype(BF16)

    z_ref[...] = jnp.dot(xb_ref[...], w_ref[...], preferred_element_type=F32).astype(z_ref.dtype)


def _in_proj(x2, w_bf, tm=1024, tn=512):
    n_tok, d = x2.shape
    n_cols = w_bf.shape[1]
    return pl.pallas_call(
        _in_proj_kernel,
        out_shape=jax.ShapeDtypeStruct((n_tok, n_cols), BF16),
        grid=(n_tok // tm, n_cols // tn),
        in_specs=[pl.BlockSpec((tm, d), lambda i, j: (i, 0)),
                  pl.BlockSpec((d, tn), lambda i, j: (0, j))],
        out_specs=pl.BlockSpec((tm, tn), lambda i, j: (i, j)),
        scratch_shapes=[pltpu.VMEM((tm, d), BF16)],
        compiler_params=pltpu.CompilerParams(dimension_semantics=("parallel", "arbitrary"),
                                             vmem_limit_bytes=VMEM_LIMIT),
        name="in_proj",
    )(x2, w_bf)


def _compress_kernel(kv_ref, pe_ref, w1_ref, w2_ref, out_ref, kv32_ref):
    n_chunks = kv_ref.shape[1] // CMP_STRIDE
    kv32_ref[...] = kv_ref[0].astype(F32)
    acc_lo = jnp.zeros((n_chunks, LANE), F32)
    acc_hi = jnp.zeros((n_chunks, LANE), F32)
    for i in range(CMP_STRIDE):
        rows = kv32_ref[pl.ds(i, n_chunks, stride=CMP_STRIDE), :]
        lo = (rows + pe_ref[i:i + 1, :]).astype(BF16)
        hi = (rows + pe_ref[CMP_STRIDE + i:CMP_STRIDE + i + 1, :]).astype(BF16)
        acc_lo += jnp.dot(lo, w1_ref[i], preferred_element_type=F32)
        acc_hi += jnp.dot(hi, w1_ref[CMP_STRIDE + i], preferred_element_type=F32)
    shifted = jnp.concatenate([acc_hi[1:], jnp.zeros((1, LANE), F32)], axis=0)
    hid = jax.nn.gelu(acc_lo + shifted)
    out = jnp.dot(hid.astype(BF16), w2_ref[...], preferred_element_type=F32)
    row = lax.broadcasted_iota(jnp.int32, out.shape, 0)
    out_ref[0, 0] = jnp.where(row < n_chunks - 1, out, 0.0).astype(out_ref.dtype)


def _compress(z3, pe, w1, w2):
    b, t_len, _ = z3.shape
    n_chunks = t_len // CMP_STRIDE
    return pl.pallas_call(
        _compress_kernel,
        out_shape=jax.ShapeDtypeStruct((b, NSA_KV_HEADS, n_chunks, LANE), BF16),
        grid=(b, NSA_KV_HEADS),
        in_specs=[pl.BlockSpec((1, t_len, LANE), lambda bi, g: (bi, 0, ZB_KVC + g)),
                  pl.BlockSpec((CMP_BLOCK, LANE), lambda bi, g: (0, 0)),
                  pl.BlockSpec((CMP_BLOCK, LANE, LANE), lambda bi, g: (0, 0, 0)),
                  pl.BlockSpec((LANE, LANE), lambda bi, g: (0, 0))],
        out_specs=pl.BlockSpec((1, 1, n_chunks, LANE), lambda bi, g: (bi, g, 0, 0)),
        scratch_shapes=[pltpu.VMEM((t_len, LANE), F32)],
        compiler_params=pltpu.CompilerParams(dimension_semantics=("parallel", "parallel"),
                                             vmem_limit_bytes=VMEM_LIMIT),
        name="kv_compress",
    )(z3, pe, w1, w2)


def _block_diag2(a, b):
    za = jnp.zeros(a.shape[:-1] + (b.shape[-1],), a.dtype)
    zb = jnp.zeros(b.shape[:-1] + (a.shape[-1],), b.dtype)
    return jnp.concatenate([jnp.concatenate([a, za], axis=-1), jnp.concatenate([zb, b], axis=-1)], axis=-2)


def _stack_heads(x):
    return jnp.concatenate([x[:, r * HEAD_DIM:(r + 1) * HEAD_DIM] for r in range(NSA_GROUP)], axis=0)


def _dot_nt(a, b):
    return lax.dot_general(a, b, (((1,), (1,)), ((), ())), preferred_element_type=F32)


def _eye(n, dtype):
    return jnp.where(lax.broadcasted_iota(jnp.int32, (n, n), 0) == lax.broadcasted_iota(jnp.int32, (n, n), 1),
                     1.0, 0.0).astype(dtype)


NSA_KEY_CHUNK = 256
NSA_V_ROWS = HEAD_DIM + 16


def _nsa_kernel(slopes_ref, *refs, n_sel, n_g):
    q_refs = refs[:n_g]
    (kvc_ref, kvs_ref, kvw_ref, gate_ref, ovl_ref, o_ref, vst_ref, vwt_ref, vct_ref, sel_ref,
     sc0_ref, sc1_ref, p0_ref, p1_ref) = refs[n_g:]
    g_base = pl.program_id(1) * n_g
    qi = pl.program_id(2)
    q0 = qi * Q_BLOCK
    cols = NSA_GROUP * Q_BLOCK
    kc_len = NSA_KEY_CHUNK
    blocks_per_chunk = kc_len // SEL_BLOCK
    n_chunks = kvs_ref.shape[1] // kc_len
    n_cmp_pad = kvc_ref.shape[2]
    n_slc = ovl_ref.shape[0]
    v_rows = vst_ref.shape[2]
    eye_dh = _eye(HEAD_DIM, BF16)
    k_lanes = lambda gg: slice(gg * LANE, gg * LANE + HEAD_DIM)
    v_lanes = lambda gg: slice(gg * LANE + HEAD_DIM, (gg + 1) * LANE)

    @pl.when(qi == 0)
    def _():
        ones_row = jnp.where(lax.broadcasted_iota(jnp.int32, (v_rows - HEAD_DIM, kc_len), 0) == 0,
                             1.0, 0.0).astype(BF16)

        def body(c, carry):
            rows = pl.ds(pl.multiple_of(c * kc_len, kc_len), kc_len)
            for gg in range(n_g):
                vst_ref[gg, c, 0:HEAD_DIM] = _dot_nt(eye_dh, kvs_ref[0, rows, v_lanes(gg)]).astype(BF16)
                vwt_ref[gg, c, 0:HEAD_DIM] = _dot_nt(eye_dh, kvw_ref[0, rows, v_lanes(gg)]).astype(BF16)
                vst_ref[gg, c, HEAD_DIM:v_rows] = ones_row
                vwt_ref[gg, c, HEAD_DIM:v_rows] = ones_row
            return carry

        lax.fori_loop(0, n_chunks, body, 0)
        for gg in range(n_g):
            vct_ref[gg] = _dot_nt(eye_dh, kvc_ref[0, gg, :, HEAD_DIM:2 * HEAD_DIM]).astype(BF16)

    log2e = float(np.log2(np.e))
    lane = lax.broadcasted_iota(jnp.int32, (1, cols), 1)
    head = lane // Q_BLOCK
    tq = q0 + lane % Q_BLOCK
    sub = lax.broadcasted_iota(jnp.int32, (kc_len, cols), 0)
    half = lax.broadcasted_iota(jnp.int32, (1, LANE), 1) // Q_BLOCK
    blk = lax.broadcasted_iota(jnp.int32, (n_slc, LANE), 0)
    blk8 = lax.broadcasted_iota(jnp.int32, (8, LANE), 0)
    cmp_end = lax.broadcasted_iota(jnp.int32, (n_cmp_pad, 1), 0) * CMP_STRIDE + (CMP_BLOCK - 1)
    d_cmp = tq - cmp_end
    m_cmp = d_cmp >= 0
    d_cmp_f = d_cmp.astype(F32)
    forced = (blk == 0) | (blk == qi) | (blk == qi - 1)
    ovl = ovl_ref[...]

    groups = range(n_g)
    qs, slopes = [], []
    for gg in groups:
        q = _stack_heads(q_refs[gg][0]).astype(F32)
        qs.append((q * (HEAD_DIM ** -0.5 * log2e)).astype(BF16))
        slope = jnp.zeros((1, cols), F32)
        for r in range(NSA_GROUP):
            slope = jnp.where(head == r, slopes_ref[(g_base + gg) * NSA_GROUP + r] * log2e, slope)
        slopes.append(slope)
    st = [_dot_nt(kvc_ref[0, gg, :, 0:HEAD_DIM], qs[gg]) for gg in groups]
    p_cmp = []
    for gg in groups:
        s_m = jnp.where(m_cmp, st[gg] - slopes[gg] * d_cmp_f, NEG_INF)
        e = jnp.where(m_cmp, jnp.exp2(s_m - jnp.max(s_m, axis=0, keepdims=True)), 0.0)
        p_cmp.append(e / jnp.maximum(jnp.sum(e, axis=0, keepdims=True), 1e-30))
    o_cmp = [jnp.dot(vct_ref[gg], p_cmp[gg].astype(BF16), preferred_element_type=F32) for gg in groups]
    parts = []
    for gg in groups:
        y = p_cmp[gg][:, 0:LANE] + p_cmp[gg][:, LANE:2 * LANE]
        parts.append(_split3(y + pltpu.roll(y, Q_BLOCK, 1)))
    pooled = [[jnp.dot(ovl, parts[gg][i], preferred_element_type=F32) for gg in groups] for i in range(3)]
    for gg in groups:
        imp = pooled[0][gg] + pooled[1][gg] + pooled[2][gg]
        imp = jnp.where(forced, jnp.inf, jnp.where(blk > qi, -jnp.inf, imp))
        tiles = [imp[t:t + 8] for t in range(0, n_slc, 8)]
        ranks = [jnp.zeros((8, LANE), F32) for _ in tiles]
        for jp in range(0, n_slc, 2):
            row = jnp.where(half == 0, imp[jp:jp + 1, :], imp[jp + 1:jp + 2, :])
            for ti, tile in enumerate(tiles):
                t0 = ti * 8
                if t0 > jp + 1:
                    hit = jnp.where(row >= tile, 1.0, 0.0)
                elif t0 + 7 <= jp:
                    hit = jnp.where(row > tile, 1.0, 0.0)
                else:
                    hit = jnp.where(blk8 + t0 > jp + half, jnp.where(row >= tile, 1.0, 0.0),
                                    jnp.where(row > tile, 1.0, 0.0))
                ranks[ti] = ranks[ti] + hit
        rank = jnp.concatenate(ranks, axis=0)
        rank = rank + pltpu.roll(rank, Q_BLOCK, 1)
        neg = jnp.where((rank < n_sel) & (blk <= qi), 0.0, NEG_INF)
        neg2 = jnp.concatenate([neg, neg], axis=1)
        for j in range(n_slc):
            sel_ref[gg, j] = jnp.broadcast_to(neg2[j:j + 1, :], (8, cols))
    sub_f = sub.astype(F32)
    fronts = [(qs[gg], slopes[gg], slopes[gg] * sub_f, o_cmp[gg]) for gg in groups]

    init1 = (jnp.full((1, cols), NEG_INF, F32), jnp.zeros((v_rows, cols), F32))

    def block_mask(gg, c):
        tiles = [sel_ref[gg, c * blocks_per_chunk + i] for i in range(blocks_per_chunk)]
        return jnp.concatenate([t for t in tiles for _ in range(SEL_BLOCK // 8)], axis=0)

    streams = {"slc": (kvs_ref, vst_ref), "win": (kvw_ref, vwt_ref)}
    sc_slots = (sc0_ref, sc1_ref)
    p_slots = (p0_ref, p1_ref)

    def score(gg, pos, slot):
        stream, c, _ = pos
        rows = pl.ds(pl.multiple_of(c * kc_len, kc_len), kc_len)
        sc_slots[slot][gg] = _dot_nt(streams[stream][0][0, rows, k_lanes(gg)], fronts[gg][0])

    def softmax(gg, pos, slot, m_prev):
        _, c, mask_fn = pos
        _, slope, bias_local, _ = fronts[gg]
        sc = sc_slots[slot][gg] + bias_local + mask_fn(gg)
        shift = slope * (c * kc_len - q0).astype(F32)
        m_new = jnp.maximum(m_prev, jnp.max(sc, axis=0, keepdims=True) + shift)
        p_slots[slot][gg] = jnp.exp2(sc - (m_new - shift)).astype(BF16)
        return m_new, jnp.exp2(m_prev - m_new)

    def weigh(gg, pos, slot, alpha, acc):
        stream, c, _ = pos
        return alpha * acc + jnp.dot(streams[stream][1][gg, c], p_slots[slot][gg], preferred_element_type=F32)

    def step(pos, slot, nxt, prev, state, alphas):
        if nxt is not None:
            for gg in range(n_g):
                score(gg, nxt, 1 - slot)
        state = {s: list(v) for s, v in state.items()}
        new_alphas = []
        for gg in range(n_g):
            m_prev, acc = state[pos[0]][gg]
            m_new, alpha = softmax(gg, pos, slot, m_prev)
            state[pos[0]][gg] = (m_new, acc)
            new_alphas.append(alpha)
            if prev is not None:
                m_p, acc_p = state[prev[0]][gg]
                state[prev[0]][gg] = (m_p, weigh(gg, prev, 1 - slot, alphas[gg], acc_p))
        return state, new_alphas

    c_cur = qi // blocks_per_chunk
    n_pairs = c_cur // 2
    plain = lambda c: ("slc", c, lambda gg: block_mask(gg, c))

    for gg in range(n_g):
        p1_ref[gg] = jnp.zeros((kc_len, cols), BF16)
        score(gg, plain(0), 0)

    def pair_body(i, carry):
        slc_state, alphas = carry
        k0 = 2 * i
        state, alphas = step(plain(k0), 0, plain(k0 + 1), plain(jnp.maximum(k0 - 1, 0)),
                             {"slc": slc_state}, alphas)
        state, alphas = step(plain(k0 + 1), 1, plain(k0 + 2), plain(k0), state, alphas)
        return tuple(state["slc"]), tuple(alphas)

    slc_state, alphas = lax.fori_loop(0, n_pairs, pair_body,
                                      (tuple(init1 for _ in range(n_g)),
                                       tuple(jnp.ones((1, cols), F32) for _ in range(n_g))))
    k_t = 2 * n_pairs
    odd_neg = jnp.where(c_cur % 2 == 1, 0.0, NEG_INF)
    causal_neg = jnp.where((c_cur * kc_len + sub) <= tq, 0.0, NEG_INF)
    tail = [("slc", k_t, lambda gg: block_mask(gg, k_t) + odd_neg),
            ("slc", c_cur, lambda gg: block_mask(gg, c_cur) + causal_neg)]
    for i in range(WINDOW // kc_len, -1, -1):
        c_raw = c_cur - i
        c_win = jnp.maximum(c_raw, 0)
        dist = tq - (c_win * kc_len + sub)
        band_neg = jnp.where((dist >= 0) & (dist < WINDOW) & (c_raw >= 0), 0.0, NEG_INF)
        tail.append(("win", c_win, lambda gg, band_neg=band_neg: band_neg))
    state = {"slc": list(slc_state), "win": [init1 for _ in range(n_g)]}
    prev = plain(jnp.maximum(k_t - 1, 0))
    for idx, pos in enumerate(tail):
        nxt = tail[idx + 1] if idx + 1 < len(tail) else None
        state, alphas = step(pos, idx % 2, nxt, prev, state, alphas)
        prev = pos
    last_slot = (len(tail) - 1) % 2
    for gg in range(n_g):
        m_p, acc_p = state[prev[0]][gg]
        state[prev[0]][gg] = (m_p, weigh(gg, prev, last_slot, alphas[gg], acc_p))

    eye_q = _eye(Q_BLOCK, BF16)
    eye_lane = _eye(LANE, BF16)
    gates = [jax.nn.sigmoid(_dot_nt(eye_lane, gate_ref[0, :, gg * LANE:(gg + 1) * LANE])) for gg in groups]

    def gate_row(gg, i):
        return jnp.concatenate([gates[gg][i * NSA_GROUP + r:i * NSA_GROUP + r + 1, :] for r in range(NSA_GROUP)],
                               axis=1)

    def normalized(acc):
        return acc[0:HEAD_DIM] / jnp.maximum(acc[HEAD_DIM:HEAD_DIM + 1], 1e-30)

    o_rows = []
    for gg in groups:
        o = (gate_row(gg, 0) * fronts[gg][3] + gate_row(gg, 1) * normalized(state["slc"][gg][1])
             + gate_row(gg, 2) * normalized(state["win"][gg][1])).astype(BF16)
        o_rows.append(jnp.concatenate([o[:, r * Q_BLOCK:(r + 1) * Q_BLOCK] for r in range(NSA_GROUP)], axis=0))
    outs = [_dot_nt(eye_q, o_rows[gg]) for gg in groups]
    for gg in groups:
        o_ref[0, :, gg * cols:(gg + 1) * cols] = outs[gg].astype(o_ref.dtype)


NSA_GROUPS_PER_STEP = 4


def _nsa_attention(z3, kv_cmp):
    b, t_len, _ = z3.shape
    n_q = t_len // Q_BLOCK
    n_slc = t_len // SEL_BLOCK
    n_cmp_pad = kv_cmp.shape[2]
    n_sel = min(N_SELECT, n_slc)
    n_g = NSA_GROUPS_PER_STEP
    cols = NSA_GROUP * Q_BLOCK
    slopes = jnp.exp2(-8.0 * jnp.arange(1, NSA_HEADS + 1, dtype=F32) / NSA_HEADS)
    c_idx = np.arange(n_cmp_pad)
    cmp_start = c_idx * CMP_STRIDE
    cmp_end = cmp_start + CMP_BLOCK - 1
    slc_start = np.arange(n_slc) * SEL_BLOCK
    overlap = np.clip(np.minimum(cmp_end[None, :], slc_start[:, None] + SEL_BLOCK - 1)
                      - np.maximum(cmp_start[None, :], slc_start[:, None]) + 1, 0, None).astype(np.float32)
    overlap[:, c_idx >= t_len // CMP_STRIDE - CMP_BLOCK // CMP_STRIDE + 1] = 0.0
    kernel = functools.partial(_nsa_kernel, n_sel=n_sel, n_g=n_g)
    n_kc = t_len // NSA_KEY_CHUNK
    q_spec = lambda gg: pl.BlockSpec((1, Q_BLOCK, cols),
                                     lambda bi, g, qi: (bi, qi, ZB_Q * LANE // cols + g * n_g + gg))
    slab = lambda zb: pl.BlockSpec((1, t_len, n_g * LANE), lambda bi, g, qi: (bi, 0, zb // n_g + g))
    assert ZB_KVS % n_g == 0 and ZB_KVW % n_g == 0 and ZB_GATE % n_g == 0
    return pl.pallas_call(
        kernel,
        out_shape=jax.ShapeDtypeStruct((b, t_len, NSA_WIDTH), BF16),
        grid=(b, NSA_KV_HEADS // n_g, n_q),
        in_specs=[pl.BlockSpec(memory_space=pltpu.SMEM)] + [q_spec(gg) for gg in range(n_g)] + [
            pl.BlockSpec((1, n_g, n_cmp_pad, LANE), lambda bi, g, qi: (bi, g, 0, 0)),
            slab(ZB_KVS), slab(ZB_KVW),
            pl.BlockSpec((1, Q_BLOCK, n_g * LANE), lambda bi, g, qi: (bi, qi, ZB_GATE // n_g + g)),
            pl.BlockSpec((n_slc, n_cmp_pad), lambda bi, g, qi: (0, 0)),
        ],
        out_specs=pl.BlockSpec((1, Q_BLOCK, n_g * cols), lambda bi, g, qi: (bi, qi, g)),
        scratch_shapes=[pltpu.VMEM((n_g, n_kc, NSA_V_ROWS, NSA_KEY_CHUNK), BF16),
                        pltpu.VMEM((n_g, n_kc, NSA_V_ROWS, NSA_KEY_CHUNK), BF16),
                        pltpu.VMEM((n_g, HEAD_DIM, n_cmp_pad), BF16),
                        pltpu.VMEM((n_g, n_slc, 8, cols), F32),
                        pltpu.VMEM((n_g, NSA_KEY_CHUNK, cols), F32),
                        pltpu.VMEM((n_g, NSA_KEY_CHUNK, cols), F32),
                        pltpu.VMEM((n_g, NSA_KEY_CHUNK, cols), BF16),
                        pltpu.VMEM((n_g, NSA_KEY_CHUNK, cols), BF16)],
        compiler_params=pltpu.CompilerParams(dimension_semantics=("parallel", "parallel", "arbitrary"),
                                             vmem_limit_bytes=VMEM_LIMIT),
        name="nsa_attention",
    )(slopes, *([z3] * n_g), kv_cmp, z3, z3, z3, jnp.asarray(overlap, BF16))


def _nsa_prepare(cmp_pe_k, cmp_w1_k, cmp_w2_k, cmp_pe_v, cmp_w1_v, cmp_w2_v):
    pe = jnp.concatenate([cmp_pe_k, cmp_pe_v], axis=-1)
    w1 = _block_diag2(cmp_w1_k.reshape(CMP_BLOCK, HEAD_DIM, HEAD_DIM),
                      cmp_w1_v.reshape(CMP_BLOCK, HEAD_DIM, HEAD_DIM)).astype(BF16)
    w2 = _block_diag2(cmp_w2_k, cmp_w2_v).astype(BF16)
    return pe, w1, w2


RWKV_CHUNK = 64


def _rwkv_prep_kernel(z_ref, mu_ref, w0_ref, wup_ref, a0_ref, aup_ref, gup_ref, kk_ref, ka_ref,
                      r_ref, k_ref, v_ref, kkr_ref, a_ref, lw_ref, g_ref, carry_ref):
    w = RWKV_WIDTH

    @pl.when(pl.program_id(1) == 0)
    def _():
        carry_ref[...] = jnp.zeros_like(carry_ref)

    z = z_ref[0].astype(F32)
    tc = z.shape[0]
    row = lax.broadcasted_iota(jnp.int32, z.shape, 0)
    prev = jnp.where(row == 0, carry_ref[0:1, :], pltpu.roll(z, 1, 0))
    carry_ref[0:1, :] = z[tc - 1:tc, :]
    zs = z + (prev - z) * mu_ref[...]
    r = zs[:, RB_R * LANE:RB_R * LANE + w]
    k = zs[:, RB_K * LANE:RB_K * LANE + w]
    v = zs[:, RB_V * LANE:RB_V * LANE + w]
    w_lo = zs[:, RB_WLO * LANE:(RB_WLO + 1) * LANE]
    a_lo = zs[:, RB_ALO * LANE:(RB_ALO + 1) * LANE]
    g_lo = zs[:, RB_GLO * LANE:RB_GLO * LANE + GATE_LORA]
    d = w0_ref[...] + jnp.dot(jnp.tanh(w_lo).astype(BF16), wup_ref[...], preferred_element_type=F32)
    w_raw = -jax.nn.softplus(-d) - 0.5
    lw_ref[0] = -jnp.exp(w_raw)
    a = jax.nn.sigmoid(a0_ref[...] + jnp.dot(a_lo.astype(BF16), aup_ref[...], preferred_element_type=F32))
    g = jnp.dot(jax.nn.sigmoid(g_lo).astype(BF16), gup_ref[...], preferred_element_type=F32)
    r_ref[0] = r.astype(r_ref.dtype)
    v_ref[0] = v.astype(v_ref.dtype)
    kkr_ref[0] = (k * kk_ref[...]).astype(kkr_ref.dtype)
    k_ref[0] = (k * (1.0 + (a - 1.0) * ka_ref[...])).astype(k_ref.dtype)
    a_ref[0] = a.astype(a_ref.dtype)
    g_ref[0] = g.astype(g_ref.dtype)


def _rwkv_prep(z3, mu, w0, w_up, a0, a_up, g_up, k_k, k_a, tc=256):
    b, t_len, _ = z3.shape
    w = RWKV_WIDTH
    ncol = RWKV_BLOCKS * LANE
    vec = lambda n: pl.BlockSpec((1, n), lambda bi, ti: (0, 0))
    mat = lambda m, n: pl.BlockSpec((m, n), lambda bi, ti: (0, 0))
    out_bf = jax.ShapeDtypeStruct((b, t_len, w), BF16)
    out_f32 = jax.ShapeDtypeStruct((b, t_len, w), F32)
    out_spec = pl.BlockSpec((1, tc, w), lambda bi, ti: (bi, ti, 0))
    return pl.pallas_call(
        _rwkv_prep_kernel,
        out_shape=(out_bf, out_bf, out_bf, out_bf, out_bf, out_f32, out_bf),
        grid=(b, t_len // tc),
        in_specs=[pl.BlockSpec((1, tc, ncol), lambda bi, ti: (bi, ti, ZB_RWKV)),
                  vec(ncol), vec(w), mat(LANE, w), vec(w), mat(LANE, w), mat(GATE_LORA, w), vec(w), vec(w)],
        out_specs=(out_spec,) * 7,
        scratch_shapes=[pltpu.VMEM((8, ncol), F32)],
        compiler_params=pltpu.CompilerParams(dimension_semantics=("parallel", "arbitrary"),
                                             vmem_limit_bytes=VMEM_LIMIT),
        name="rwkv_prep",
    )(z3, mu, w0, w_up, a0, a_up, g_up, k_k, k_a)


def _pair_blocks(x):
    lane = lax.broadcasted_iota(jnp.int32, x.shape, 1)
    zero = jnp.zeros((), x.dtype)
    return jnp.concatenate([jnp.where(lane < HEAD_DIM, x, zero), jnp.where(lane >= HEAD_DIM, x, zero)], axis=0)


def _fold_pair(x):
    n = x.shape[0] // 2
    return x[:n] + x[n:]


def _split3(x):
    hi = x.astype(BF16)
    r1 = x - hi.astype(F32)
    mid = r1.astype(BF16)
    return hi, mid, (r1 - mid.astype(F32)).astype(BF16)


def _dot_split_rhs(a_bf, x):
    hi, mid, lo = _split3(x)
    return (jnp.dot(a_bf, hi, preferred_element_type=F32) + jnp.dot(a_bf, mid, preferred_element_type=F32)
            + jnp.dot(a_bf, lo, preferred_element_type=F32))


def _dot_split_lhs(x, b_bf):
    hi, mid, lo = _split3(x)
    return (jnp.dot(hi, b_bf, preferred_element_type=F32) + jnp.dot(mid, b_bf, preferred_element_type=F32)
            + jnp.dot(lo, b_bf, preferred_element_type=F32))


RWKV_PAIRS_PER_STEP = 4


def _rwkv_scan_kernel(r_ref, k_ref, v_ref, kkr_ref, a_ref, lw_ref, g_ref, rk_ref, lng_ref, lnb_ref,
                      o_ref, s_ref, *, n_chunks, n_pairs):
    L = RWKV_CHUNK
    L2 = 2 * L

    @pl.when(pl.program_id(2) == 0)
    def _():
        s_ref[...] = jnp.zeros_like(s_ref)

    ri = lax.broadcasted_iota(jnp.int32, (L, L), 0)
    ci = lax.broadcasted_iota(jnp.int32, (L, L), 1)
    tri_incl = jnp.where(ri >= ci, 1.0, 0.0).astype(BF16)
    r2 = lax.broadcasted_iota(jnp.int32, (L2, L2), 0)
    c2 = lax.broadcasted_iota(jnp.int32, (L2, L2), 1)
    same_head = (r2 // L) == (c2 // L)
    strict2 = same_head & (r2 > c2)
    incl2 = same_head & (r2 >= c2)
    eye2 = jnp.where(r2 == c2, 1.0, 0.0).astype(F32)
    h_r = lax.broadcasted_iota(jnp.int32, (LANE, LANE), 0) // HEAD_DIM
    h_c = lax.broadcasted_iota(jnp.int32, (LANE, LANE), 1) // HEAD_DIM
    head_mask = h_r == h_c
    head_ones = jnp.where(head_mask, 1.0, 0.0).astype(BF16)

    def head_sum(x):
        return _dot_split_lhs(x, head_ones)

    def mm(a, b):
        return jnp.dot(a.astype(BF16), b.astype(BF16), preferred_element_type=F32)

    chains = [(hp, c) for hp in range(n_pairs) for c in range(n_chunks)]
    lanes_of = lambda hp: slice(hp * LANE, (hp + 1) * LANE)
    rows_of = lambda c: pl.ds(c * L, L)
    each = lambda fn: {ch: fn(ch) for ch in chains}

    load = lambda ref: each(lambda ch: ref[0, rows_of(ch[1]), lanes_of(ch[0])])
    r, k, v, a = (each(lambda ch, d=d: d[ch].astype(F32)) for d in (load(r_ref), load(k_ref), load(v_ref), load(a_ref)))
    kkr = each(lambda ch, d=load(kkr_ref): d[ch].astype(F32))
    lw = load(lw_ref)
    kk_sq = each(lambda ch: head_sum(kkr[ch] * kkr[ch]))
    cl = each(lambda ch: _dot_split_rhs(tri_incl, lw[ch]))
    kk = each(lambda ch: kkr[ch] / jnp.maximum(jnp.sqrt(kk_sq[ch]), 1e-12))
    p_incl = each(lambda ch: jnp.exp(cl[ch]))
    p_inv = each(lambda ch: jnp.exp(-cl[ch]))
    rt = each(lambda ch: (r[ch] * p_incl[ch]).astype(BF16))
    at = each(lambda ch: (-kk[ch] * jnp.exp(cl[ch] - lw[ch])).astype(BF16))
    kt = each(lambda ch: k[ch] * p_inv[ch])
    bt = each(lambda ch: kk[ch] * a[ch] * p_inv[ch])
    gram = each(lambda ch: _dot_nt(
        jnp.concatenate([_pair_blocks(at[ch]), _pair_blocks(rt[ch])], axis=0),
        jnp.concatenate([_pair_blocks(bt[ch]), _pair_blocks(kt[ch])], axis=0).astype(BF16)))
    a_ab = each(lambda ch: jnp.where(strict2, gram[ch][:L2, :L2], 0.0))
    a_ak = each(lambda ch: jnp.where(strict2, gram[ch][:L2, L2:], 0.0).astype(BF16))
    a_rb = each(lambda ch: jnp.where(incl2, gram[ch][L2:, :L2], 0.0).astype(BF16))
    a_rk = each(lambda ch: jnp.where(incl2, gram[ch][L2:, L2:], 0.0).astype(BF16))
    tinv = each(lambda ch: eye2 + a_ab[ch])
    apow = a_ab
    for _ in range(int(np.log2(L)) - 1):
        apow = each(lambda ch: mm(apow[ch], apow[ch]))
        tinv = each(lambda ch: tinv[ch] + mm(tinv[ch], apow[ch]))
    tinv = each(lambda ch: tinv[ch].astype(BF16))
    v_bf = each(lambda ch: v[ch].astype(BF16))
    v2 = each(lambda ch: _pair_blocks(v_bf[ch]))
    akv = each(lambda ch: _fold_pair(jnp.dot(a_ak[ch], v2[ch], preferred_element_type=F32)))
    rkv = each(lambda ch: _fold_pair(jnp.dot(a_rk[ch], v2[ch], preferred_element_type=F32)))
    p_last = each(lambda ch: p_incl[ch][L - 1:L, :])
    wts = each(lambda ch: jnp.concatenate([bt[ch] * p_last[ch], kt[ch] * p_last[ch]], axis=0).astype(BF16))
    bonus = each(lambda ch: head_sum(r[ch] * k[ch] * rk_ref[:, lanes_of(ch[0])]) * v[ch])

    pairs = range(n_pairs)
    states = [s_ref[hp] for hp in pairs]
    for c in range(n_chunks):
        s_bf = [states[hp].astype(BF16) for hp in pairs]
        m = [_dot_nt(at[hp, c], s_bf[hp]) + akv[hp, c] for hp in pairs]
        y = [_dot_nt(rt[hp, c], s_bf[hp]) + rkv[hp, c] for hp in pairs]
        u = [_fold_pair(jnp.dot(tinv[hp, c], _pair_blocks(m[hp].astype(BF16)), preferred_element_type=F32))
             for hp in pairs]
        u_bf = [u[hp].astype(BF16) for hp in pairs]
        upd = [lax.dot_general(jnp.concatenate([u_bf[hp], v_bf[hp, c]], axis=0), wts[hp, c],
                               (((0,), (0,)), ((), ())), preferred_element_type=F32) for hp in pairs]
        states = [states[hp] * p_last[hp, c] + jnp.where(head_mask, upd[hp], 0.0) for hp in pairs]
        y = [y[hp] + _fold_pair(jnp.dot(a_rb[hp, c], _pair_blocks(u_bf[hp]), preferred_element_type=F32))
             for hp in pairs]
        mean = [head_sum(y[hp]) * (1.0 / HEAD_DIM) for hp in pairs]
        yc = [y[hp] - mean[hp] for hp in pairs]
        var = [head_sum(yc[hp] * yc[hp]) * (1.0 / HEAD_DIM) for hp in pairs]
        for hp in pairs:
            yn = yc[hp] * lax.rsqrt(var[hp] + GN_EPS) * lng_ref[:, lanes_of(hp)] + lnb_ref[:, lanes_of(hp)]
            o_ref[0, rows_of(c), lanes_of(hp)] = (
                (yn + bonus[hp, c]) * g_ref[0, rows_of(c), lanes_of(hp)].astype(F32)).astype(o_ref.dtype)
    for hp in pairs:
        s_ref[hp] = states[hp]


def _rwkv_scan(r, k, v, kkr, a, lw, g, r_k, ln_g, ln_b, tt=256):
    b, t_len, w = r.shape
    n_p = RWKV_PAIRS_PER_STEP
    width = n_p * LANE
    tile = pl.BlockSpec((1, tt, width), lambda bi, hp, ti: (bi, ti, hp))
    vec = pl.BlockSpec((1, width), lambda bi, hp, ti: (0, hp))
    kernel = functools.partial(_rwkv_scan_kernel, n_chunks=tt // RWKV_CHUNK, n_pairs=n_p)
    return pl.pallas_call(
        kernel,
        out_shape=jax.ShapeDtypeStruct((b, t_len, w), BF16),
        grid=(b, w // width, t_len // tt),
        in_specs=[tile] * 7 + [vec] * 3,
        out_specs=tile,
        scratch_shapes=[pltpu.VMEM((n_p, LANE, LANE), F32)],
        compiler_params=pltpu.CompilerParams(dimension_semantics=("parallel", "parallel", "arbitrary"),
                                             vmem_limit_bytes=VMEM_LIMIT),
        name="rwkv_scan",
    )(r, k, v, kkr, a, lw, g, r_k, ln_g, ln_b)


def _rwkv_time_mix(z3, mu, w0, w_up, a0, a_up, g_up, k_k, k_a, r_k, ln_g, ln_b):
    row = lambda u: u.reshape(1, -1)
    mu_p = row(_take_columns(mu, _rwkv_source_columns()))
    r, k, v, kkr, a, lw, g = _rwkv_prep(z3, mu_p, row(w0), _pad_rows(w_up, LANE).astype(BF16), row(a0),
                                        _pad_rows(a_up, LANE).astype(BF16), g_up.astype(BF16), row(k_k), row(k_a))
    return _rwkv_scan(r, k, v, kkr, a, lw, g, row(r_k), row(ln_g), row(ln_b))


def _layer_norm(x, g, b):
    mean = jnp.mean(x, axis=-1, keepdims=True)
    xc = x - mean
    var = jnp.mean(xc * xc, axis=-1, keepdims=True)
    return xc * lax.rsqrt(var + LN_EPS) * g + b


def _out_proj_kernel(on_ref, or_ref, x_ref, wa_ref, wb_ref, g_ref, b_ref, rw_ref, rb_ref,
                     h_ref, idx_ref, wgt_ref, *, alpha):
    mix = (jnp.dot(on_ref[...], wa_ref[...], preferred_element_type=F32)
           + jnp.dot(or_ref[...], wb_ref[...], preferred_element_type=F32))
    h = _layer_norm(alpha * x_ref[...] + mix, g_ref[...], b_ref[...])
    h_ref[...] = h
    logits = jnp.dot(h, rw_ref[...], preferred_element_type=F32, precision=HIGHEST) + rb_ref[...]
    lane = lax.broadcasted_iota(jnp.int32, logits.shape, 1)
    logits = jnp.where(lane < N_EXPERTS, logits, -jnp.inf)
    idx_out = jnp.zeros(logits.shape, jnp.int32)
    val_out = jnp.full(logits.shape, -jnp.inf, F32)
    for k in range(TOP_K):
        best = jnp.max(logits, axis=-1, keepdims=True)
        first = jnp.min(jnp.where(logits == best, lane, LANE), axis=-1, keepdims=True)
        idx_out = jnp.where(lane == k, first, idx_out)
        val_out = jnp.where(lane == k, best, val_out)
        logits = jnp.where(lane == first, -jnp.inf, logits)
    e = jnp.exp(val_out - jnp.max(val_out, axis=-1, keepdims=True))
    idx_ref[...] = idx_out
    wgt_ref[...] = e / jnp.sum(e, axis=-1, keepdims=True)


def _out_proj_router(o_nsa, o_rwkv, x2, w_out, ln_g, ln_b, router_w, router_b, alpha, tm=512):
    n_tok, d = x2.shape
    half = o_nsa.shape[1]
    rw = jnp.pad(router_w, ((0, 0), (0, LANE - N_EXPERTS)))
    rb = jnp.pad(router_b, (0, LANE - N_EXPERTS)).reshape(1, LANE)
    row_blk = lambda n: pl.BlockSpec((tm, n), lambda i: (i, 0))
    full = lambda m, n: pl.BlockSpec((m, n), lambda i: (0, 0))
    return pl.pallas_call(
        functools.partial(_out_proj_kernel, alpha=alpha),
        out_shape=(jax.ShapeDtypeStruct((n_tok, d), F32), jax.ShapeDtypeStruct((n_tok, LANE), jnp.int32),
                   jax.ShapeDtypeStruct((n_tok, LANE), F32)),
        grid=(n_tok // tm,),
        in_specs=[row_blk(half), row_blk(half), row_blk(d), full(half, d), full(half, d), full(1, d), full(1, d),
                  full(d, LANE), full(1, LANE)],
        out_specs=(row_blk(d), row_blk(LANE), row_blk(LANE)),
        compiler_params=pltpu.CompilerParams(dimension_semantics=("parallel",), vmem_limit_bytes=VMEM_LIMIT),
        name="out_proj_router",
    )(o_nsa, o_rwkv, x2, w_out[:half].astype(BF16), w_out[half:].astype(BF16), ln_g.reshape(1, d),
      ln_b.reshape(1, d), rw, rb)


MOE_ITEM_ROWS = 1024
MOE_SUB_ROWS = 256
MOE_F_TILE = 256
MOE_TOKEN_TILE = 256


MOE_RANK_BLOCK = 512


def _moe_rank_kernel(e_ref, rank_ref, cnt_ref, carry_ref):
    @pl.when(pl.program_id(0) == 0)
    def _():
        carry_ref[...] = jnp.zeros_like(carry_ref)

    blk = e_ref.shape[0]
    lane = lax.broadcasted_iota(jnp.int32, (blk, LANE), 1)
    onehot = jnp.where(e_ref[...] == lane, 1.0, 0.0)
    tri = jnp.where(lax.broadcasted_iota(jnp.int32, (blk, blk), 0) >= lax.broadcasted_iota(jnp.int32, (blk, blk), 1),
                    1.0, 0.0).astype(BF16)
    csum = jnp.dot(tri, onehot.astype(BF16), preferred_element_type=F32)
    carry = carry_ref[...]
    rank_ref[...] = (jnp.sum((csum + carry) * onehot, axis=1, keepdims=True) - 1.0).astype(jnp.int32)
    carry_ref[...] = carry + csum[blk - 1:blk, :]
    cnt_ref[...] = carry_ref[...].astype(jnp.int32)


def _moe_rank(flat_e):
    n_assign = flat_e.shape[0]
    return pl.pallas_call(
        _moe_rank_kernel,
        out_shape=(jax.ShapeDtypeStruct((n_assign, 1), jnp.int32), jax.ShapeDtypeStruct((1, LANE), jnp.int32)),
        grid=(n_assign // MOE_RANK_BLOCK,),
        in_specs=[pl.BlockSpec((MOE_RANK_BLOCK, 1), lambda i: (i, 0))],
        out_specs=(pl.BlockSpec((MOE_RANK_BLOCK, 1), lambda i: (i, 0)), pl.BlockSpec((1, LANE), lambda i: (0, 0))),
        scratch_shapes=[pltpu.VMEM((1, LANE), F32)],
        compiler_params=pltpu.CompilerParams(dimension_semantics=("arbitrary",), vmem_limit_bytes=VMEM_LIMIT),
        name="moe_rank",
    )(flat_e.reshape(n_assign, 1))


def _moe_tables(top_idx, n_items):
    n_tok = top_idx.shape[0]
    flat_e = top_idx.reshape(-1)
    rank, counts = _moe_rank(flat_e)
    counts = counts[0, :N_EXPERTS]
    items_e = (counts + MOE_ITEM_ROWS - 1) // MOE_ITEM_ROWS
    items_end = jnp.cumsum(items_e)
    item_start_e = items_end - items_e
    dest = jnp.take(item_start_e * MOE_ITEM_ROWS, flat_e) + rank[:, 0]
    item = jnp.arange(n_items, dtype=jnp.int32)
    valid = item < items_end[-1]
    last_e = jnp.max(jnp.where(counts > 0, jnp.arange(N_EXPERTS, dtype=jnp.int32), 0))
    item_e = jnp.minimum(jnp.sum(items_end[None, :] <= item[:, None], axis=1).astype(jnp.int32), N_EXPERTS - 1)
    item_e = jnp.where(valid, item_e, last_e)
    item_nv = jnp.where(valid, jnp.clip(counts[item_e] - (item - item_start_e[item_e]) * MOE_ITEM_ROWS,
                                        0, MOE_ITEM_ROWS), 0).astype(jnp.int32)
    return dest.reshape(n_tok, TOP_K), item_e, item_nv


DISPATCH_ISSUE_UNROLL = 8


def _dispatch_kernel(dest_ref, h_ref, xg_in_ref, xg_ref, stage_ref, sem_ref):
    del xg_in_ref
    tm = h_ref.shape[0]
    i = pl.program_id(0)
    slot = i % 2

    def wait_slot(s):
        for _ in range(TOP_K):
            pltpu.make_async_copy(stage_ref.at[s], xg_ref.at[pl.ds(0, tm)], sem_ref.at[s]).wait()

    @pl.when(i >= 2)
    def _():
        wait_slot(slot)

    stage_ref[slot] = h_ref[...]

    def body(j, carry):
        for u in range(DISPATCH_ISSUE_UNROLL):
            t = j * DISPATCH_ISSUE_UNROLL + u
            for k in range(TOP_K):
                pltpu.make_async_copy(stage_ref.at[slot, pl.ds(t, 1)], xg_ref.at[pl.ds(dest_ref[0, 0, k * tm + t], 1)],
                                      sem_ref.at[slot]).start()
        return carry

    lax.fori_loop(0, tm // DISPATCH_ISSUE_UNROLL, body, 0)

    @pl.when(i == pl.num_programs(0) - 1)
    def _():
        wait_slot(slot)

        @pl.when(i >= 1)
        def _():
            wait_slot(1 - slot)


def _dispatch(h1, dest_tiles, n_rows, tm):
    n_tok, d = h1.shape
    xg0 = jnp.zeros((n_rows, d), F32)
    return pl.pallas_call(
        _dispatch_kernel,
        out_shape=jax.ShapeDtypeStruct((n_rows, d), F32),
        grid=(n_tok // tm,),
        in_specs=[pl.BlockSpec((1, 1, TOP_K * tm), lambda i: (i, 0, 0), memory_space=pltpu.SMEM),
                  pl.BlockSpec((tm, d), lambda i: (i, 0)), pl.BlockSpec(memory_space=pl.ANY)],
        out_specs=pl.BlockSpec(memory_space=pl.ANY),
        scratch_shapes=[pltpu.VMEM((2, tm, d), F32), pltpu.SemaphoreType.DMA((2,))],
        input_output_aliases={2: 0},
        compiler_params=pltpu.CompilerParams(dimension_semantics=("arbitrary",), vmem_limit_bytes=VMEM_LIMIT),
        name="moe_dispatch",
    )(dest_tiles, h1, xg0)


def _moe_kernel(item_e_ref, item_nv_ref, x_ref, wg_ref, bg_ref, wu_ref, bu_ref, wd_ref, bd_ref, o_ref):
    i = pl.program_id(0)
    f = pl.program_id(1)
    nv = item_nv_ref[i]
    n_sub_max = MOE_ITEM_ROWS // MOE_SUB_ROWS
    n_sub = (nv + MOE_SUB_ROWS - 1) // MOE_SUB_ROWS
    rows = [pl.ds(sb * MOE_SUB_ROWS, MOE_SUB_ROWS) for sb in range(n_sub_max)]

    @pl.when(f == 0)
    def _():
        for sb in range(n_sub_max):
            bias = jnp.broadcast_to(bd_ref[0], (MOE_SUB_ROWS, o_ref.shape[1]))
            o_ref[rows[sb], :] = jnp.where(sb * MOE_SUB_ROWS < nv, bias, 0.0)

    for n in range(1, n_sub_max + 1):
        @pl.when(n_sub == n)
        def _(n=n):
            wg = wg_ref[0].astype(BF16)
            wu = wu_ref[0].astype(BF16)
            wd = wd_ref[0].astype(BF16)
            xs = [x_ref[rows[sb], :].astype(BF16) for sb in range(n)]
            gates = [jnp.dot(xs[sb], wg, preferred_element_type=F32) for sb in range(n)]
            ups = [jnp.dot(xs[sb], wu, preferred_element_type=F32) for sb in range(n)]
            for sb in range(n):
                gate = jnp.minimum(gates[sb] + bg_ref[0], SWIGLU_LIMIT)
                up = jnp.clip(ups[sb] + bu_ref[0], -SWIGLU_LIMIT, SWIGLU_LIMIT)
                h = gate * jax.nn.sigmoid(SWIGLU_ALPHA * gate) * (up + 1.0)
                o_ref[rows[sb], :] += jnp.dot(h.astype(BF16), wd, preferred_element_type=F32)


def _moe_experts(xg, item_e, item_nv, w_gate, b_gate, w_up, b_up, w_down, b_down):
    n_rows = xg.shape[0]
    d = w_gate.shape[1]
    n_items = n_rows // MOE_ITEM_ROWS
    n_e, _, d_ff = w_gate.shape
    n_f = d_ff // MOE_F_TILE

    def f_idx(i, f, nv):
        return jnp.where(nv[i] > 0, f, n_f - 1)

    grid_spec = pltpu.PrefetchScalarGridSpec(
        num_scalar_prefetch=2,
        grid=(n_items, n_f),
        in_specs=[
            pl.BlockSpec((MOE_ITEM_ROWS, d), lambda i, f, e, nv: (i, 0)),
            pl.BlockSpec((1, d, MOE_F_TILE), lambda i, f, e, nv: (e[i], 0, f_idx(i, f, nv))),
            pl.BlockSpec((1, 1, MOE_F_TILE), lambda i, f, e, nv: (e[i], 0, f_idx(i, f, nv))),
            pl.BlockSpec((1, d, MOE_F_TILE), lambda i, f, e, nv: (e[i], 0, f_idx(i, f, nv))),
            pl.BlockSpec((1, 1, MOE_F_TILE), lambda i, f, e, nv: (e[i], 0, f_idx(i, f, nv))),
            pl.BlockSpec((1, MOE_F_TILE, d), lambda i, f, e, nv: (e[i], f_idx(i, f, nv), 0)),
            pl.BlockSpec((1, 1, d), lambda i, f, e, nv: (e[i], 0, 0)),
        ],
        out_specs=pl.BlockSpec((MOE_ITEM_ROWS, d), lambda i, f, e, nv: (i, 0)),
    )
    return pl.pallas_call(
        _moe_kernel,
        out_shape=jax.ShapeDtypeStruct((n_rows, d), F32),
        grid_spec=grid_spec,
        compiler_params=pltpu.CompilerParams(dimension_semantics=("parallel", "arbitrary"),
                                             vmem_limit_bytes=VMEM_LIMIT),
        name="moe_experts",
    )(item_e, item_nv, xg, w_gate, b_gate.reshape(n_e, 1, d_ff), w_up, b_up.reshape(n_e, 1, d_ff), w_down,
      b_down.reshape(n_e, 1, d))


COMBINE_ISSUE_UNROLL = 8


def _final_kernel(dest_ref, dest_next_ref, h_ref, eo_ref, wgt_ref, p_ref, g_ref, b_ref, wgate_ref, wple_ref,
                  o_ref, rows_ref, sem_ref, *, alpha):
    tm, d = h_ref.shape
    n_rows = TOP_K * tm
    i = pl.program_id(0)
    slot = i % 2

    def row_copy(idx_ref, r, s):
        return pltpu.make_async_copy(eo_ref.at[pl.ds(idx_ref[0, 0, r], 1)], rows_ref.at[s, pl.ds(r, 1)],
                                     sem_ref.at[s])

    def issue(idx_ref, s):
        def body(j, carry):
            for u in range(COMBINE_ISSUE_UNROLL):
                row_copy(idx_ref, j * COMBINE_ISSUE_UNROLL + u, s).start()
            return carry

        lax.fori_loop(0, n_rows // COMBINE_ISSUE_UNROLL, body, 0)

    @pl.when(i == 0)
    def _():
        issue(dest_ref, 0)

    @pl.when(i + 1 < pl.num_programs(0))
    def _():
        issue(dest_next_ref, 1 - slot)

    pltpu.make_async_copy(eo_ref.at[pl.ds(0, n_rows)], rows_ref.at[slot], sem_ref.at[slot]).wait()
    wgt = wgt_ref[...]
    ffn = wgt[:, 0:1] * rows_ref[slot, 0:tm, :]
    for k in range(1, TOP_K):
        ffn = ffn + wgt[:, k:k + 1] * rows_ref[slot, k * tm:(k + 1) * tm, :]
    h = _layer_norm(alpha * h_ref[...] + ffn, g_ref[...], b_ref[...])
    gate = jax.nn.sigmoid(jnp.dot(h.astype(BF16), wgate_ref[...], preferred_element_type=F32))
    ple = jnp.dot(p_ref[...].astype(BF16), wple_ref[...], preferred_element_type=F32)
    o_ref[...] = h + gate * ple


def _dest_tiles(dest, tm):
    n_tiles = dest.shape[0] // tm
    return dest.reshape(n_tiles, tm, TOP_K).transpose(0, 2, 1).reshape(n_tiles, 1, TOP_K * tm)


def _final(h1, eo, dest_tiles, top_w, p2, ln_g, ln_b, ple_gate_w, ple_w, alpha, tm):
    n_tok, d = h1.shape
    n_tiles = n_tok // tm
    row_blk = lambda n: pl.BlockSpec((tm, n), lambda i: (i, 0))
    full = lambda m, n: pl.BlockSpec((m, n), lambda i: (0, 0))
    idx_blk = lambda fn: pl.BlockSpec((1, 1, TOP_K * tm), fn, memory_space=pltpu.SMEM)
    return pl.pallas_call(
        functools.partial(_final_kernel, alpha=alpha),
        out_shape=jax.ShapeDtypeStruct((n_tok, d), F32),
        grid=(n_tiles,),
        in_specs=[idx_blk(lambda i: (i, 0, 0)), idx_blk(lambda i: (jnp.minimum(i + 1, n_tiles - 1), 0, 0)),
                  row_blk(d), pl.BlockSpec(memory_space=pl.ANY), row_blk(LANE), row_blk(p2.shape[1]),
                  full(1, d), full(1, d), full(d, d), full(p2.shape[1], d)],
        out_specs=row_blk(d),
        scratch_shapes=[pltpu.VMEM((2, TOP_K * tm, d), F32), pltpu.SemaphoreType.DMA((2,))],
        compiler_params=pltpu.CompilerParams(dimension_semantics=("arbitrary",), vmem_limit_bytes=VMEM_LIMIT),
        name="combine_ln_ple",
    )(dest_tiles, dest_tiles, h1, eo, top_w, p2, ln_g.reshape(1, d), ln_b.reshape(1, d),
      ple_gate_w.astype(BF16), ple_w.astype(BF16))


def kernel(x, p, w_in, cmp_pe_k, cmp_w1_k, cmp_w2_k, cmp_pe_v, cmp_w1_v, cmp_w2_v, rwkv_mu, rwkv_w0, rwkv_w_up, rwkv_a0, rwkv_a_up, rwkv_g_up, rwkv_k_k, rwkv_k_a, rwkv_r_k, rwkv_ln_g, rwkv_ln_b, w_out, ln1_g, ln1_b, router_w, router_b, exp_w_gate, exp_b_gate, exp_w_up, exp_b_up, exp_w_down, exp_b_down, ln2_g, ln2_b, ple_w, ple_gate_w):
    b, t_len, d = x.shape
    depth = w_in.shape[0]
    alpha = float((2 * depth) ** 0.25)
    n_tok = b * t_len
    n_items = (n_tok * TOP_K) // MOE_ITEM_ROWS + N_EXPERTS
    h = x.reshape(n_tok, d)
    for i in range(depth):
        w_bf = _take_columns(w_in[i].astype(BF16), _z_source_columns())
        z3 = _in_proj(h, w_bf).reshape(b, t_len, Z_COLS)
        pe, w1, w2 = _nsa_prepare(cmp_pe_k[i], cmp_w1_k[i], cmp_w2_k[i], cmp_pe_v[i], cmp_w1_v[i], cmp_w2_v[i])
        kv_cmp = _compress(z3, pe, w1, w2)
        o_nsa = _nsa_attention(z3, kv_cmp)
        o_rwkv = _rwkv_time_mix(z3, rwkv_mu[i], rwkv_w0[i], rwkv_w_up[i], rwkv_a0[i], rwkv_a_up[i], rwkv_g_up[i],
                                rwkv_k_k[i], rwkv_k_a[i], rwkv_r_k[i].reshape(-1), rwkv_ln_g[i], rwkv_ln_b[i])
        h1, top_idx, top_w = _out_proj_router(o_nsa.reshape(n_tok, -1), o_rwkv.reshape(n_tok, -1), h, w_out[i],
                                              ln1_g[i], ln1_b[i], router_w[i], router_b[i], alpha)
        dest, item_e, item_nv = _moe_tables(top_idx[:, :TOP_K], n_items)
        dest_tiles = _dest_tiles(dest, MOE_TOKEN_TILE)
        xg = _dispatch(h1, dest_tiles, n_items * MOE_ITEM_ROWS, MOE_TOKEN_TILE)
        eo = _moe_experts(xg, item_e, item_nv, exp_w_gate[i], exp_b_gate[i], exp_w_up[i], exp_b_up[i],
                          exp_w_down[i], exp_b_down[i])
        h = _final(h1, eo, dest_tiles, top_w, p[i].reshape(n_tok, -1), ln2_g[i], ln2_b[i], ple_gate_w[i],
                   ple_w[i], alpha, MOE_TOKEN_TILE)
    return h.reshape(b, t_len, d)
```

```python
import functools

import numpy as np
import jax
import jax.numpy as jnp
from jax import lax
from jax.experimental import pallas as pl
from jax.experimental.pallas import tpu as pltpu

F32 = jnp.float32
BF16 = jnp.bfloat16

LANE = 128
D_MODEL = 2048
HEAD_DIM = 64
NSA_HEADS = 16
NSA_KV_HEADS = 4
NSA_GROUP = NSA_HEADS // NSA_KV_HEADS
NSA_WIDTH = NSA_HEADS * HEAD_DIM
NSA_KV_WIDTH = NSA_KV_HEADS * HEAD_DIM
CMP_BLOCK = 32
CMP_STRIDE = 16
SEL_BLOCK = 64
N_SELECT = 16
WINDOW = 512
N_GATES = 3
Q_BLOCK = 64
RWKV_HEADS = 16
RWKV_WIDTH = RWKV_HEADS * HEAD_DIM
DECAY_LORA = 96
ICLR_LORA = 96
GATE_LORA = 256
GN_EPS = 64e-5
N_EXPERTS = 32
TOP_K = 4
SWIGLU_ALPHA = 1.702
SWIGLU_LIMIT = 7.0
PLE_DIM = 256
LN_EPS = 1e-5
NEG_INF = -1e30

NSA_COLS = NSA_WIDTH + 6 * NSA_KV_WIDTH + NSA_HEADS * N_GATES
RWKV_COLS = 3 * RWKV_WIDTH + DECAY_LORA + ICLR_LORA + GATE_LORA

RB_R = 0
RB_K = RB_R + RWKV_WIDTH // LANE
RB_V = RB_K + RWKV_WIDTH // LANE
RB_WLO = RB_V + RWKV_WIDTH // LANE
RB_ALO = RB_WLO + 1
RB_GLO = RB_ALO + 1
RWKV_BLOCKS = RB_GLO + GATE_LORA // LANE
ZB_RWKV = 0
ZB_Q = ZB_RWKV + RWKV_BLOCKS
ZB_KVC = ZB_Q + NSA_WIDTH // LANE
ZB_KVS = ZB_KVC + NSA_KV_HEADS
ZB_KVW = ZB_KVS + NSA_KV_HEADS
ZB_GATE = ZB_KVW + NSA_KV_HEADS
Z_BLOCKS = ZB_GATE + NSA_KV_HEADS
Z_COLS = Z_BLOCKS * LANE
assert (ZB_Q * LANE) % (NSA_GROUP * HEAD_DIM) == 0

VMEM_LIMIT = 56 * 1024 * 1024


def _z_source_columns():
    src = np.full((Z_COLS,), -1, np.int64)
    src[ZB_Q * LANE:ZB_Q * LANE + NSA_WIDTH] = np.arange(NSA_WIDTH)
    for branch in range(3):
        k0 = NSA_WIDTH + 2 * branch * NSA_KV_WIDTH
        v0 = k0 + NSA_KV_WIDTH
        for g in range(NSA_KV_HEADS):
            base = (ZB_KVC + branch * NSA_KV_HEADS + g) * LANE
            src[base:base + HEAD_DIM] = k0 + g * HEAD_DIM + np.arange(HEAD_DIM)
            src[base + HEAD_DIM:base + 2 * HEAD_DIM] = v0 + g * HEAD_DIM + np.arange(HEAD_DIM)
    g0 = NSA_WIDTH + 6 * NSA_KV_WIDTH
    for g in range(NSA_KV_HEADS):
        base = (ZB_GATE + g) * LANE
        for i in range(N_GATES):
            for r in range(NSA_GROUP):
                src[base + i * NSA_GROUP + r] = g0 + (g * NSA_GROUP + r) * N_GATES + i
    rwkv = _rwkv_source_columns()
    src[ZB_RWKV * LANE:(ZB_RWKV + RWKV_BLOCKS) * LANE] = np.where(rwkv >= 0, NSA_COLS + rwkv, -1)
    return src


def _rwkv_source_columns():
    src = np.full((RWKV_BLOCKS * LANE,), -1, np.int64)
    src[:3 * RWKV_WIDTH] = np.arange(3 * RWKV_WIDTH)
    src[RB_WLO * LANE:RB_WLO * LANE + DECAY_LORA] = 3 * RWKV_WIDTH + np.arange(DECAY_LORA)
    src[RB_ALO * LANE:RB_ALO * LANE + ICLR_LORA] = 3 * RWKV_WIDTH + DECAY_LORA + np.arange(ICLR_LORA)
    src[RB_GLO * LANE:] = 3 * RWKV_WIDTH + DECAY_LORA + ICLR_LORA + np.arange(GATE_LORA)
    return src


def _take_columns(w, src):
    pieces, i, n = [], 0, len(src)
    while i < n:
        j = i + 1
        if src[i] < 0:
            while j < n and src[j] < 0:
                j += 1
            pieces.append(jnp.zeros(w.shape[:-1] + (j - i,), w.dtype))
        else:
            while j < n and src[j] == src[i] + (j - i):
                j += 1
            pieces.append(w[..., int(src[i]):int(src[i]) + j - i])
        i = j
    return jnp.concatenate(pieces, axis=-1)


def _pad_rows(w, rows):
    return jnp.pad(w, ((0, rows - w.shape[0]), (0, 0)))


def _in_proj_kernel(x_ref, w_ref, z_ref, xb_ref):
    @pl.when(pl.program_id(1) == 0)
    def _():
        xb_ref[...] = x_ref[...].astype(BF16)

    z_ref[...] = jnp.dot(xb_ref[...], w_ref[...], preferred_element_type=F32).astype(z_ref.dtype)


def _in_proj(x2, w_bf, tm=1024, tn=512):
    n_tok, d = x2.shape
    n_cols = w_bf.shape[1]
    return pl.pallas_call(
        _in_proj_kernel,
        out_shape=jax.ShapeDtypeStruct((n_tok, n_cols), BF16),
        grid=(n_tok // tm, n_cols // tn),
        in_specs=[pl.BlockSpec((tm, d), lambda i, j: (i, 0)),
                  pl.BlockSpec((d, tn), lambda i, j: (0, j))],
        out_specs=pl.BlockSpec((tm, tn), lambda i, j: (i, j)),
        scratch_shapes=[pltpu.VMEM((tm, d), BF16)],
        compiler_params=pltpu.CompilerParams(dimension_semantics=("parallel", "arbitrary"),
                                             vmem_limit_bytes=VMEM_LIMIT),
        name="in_proj",
    )(x2, w_bf)


def _compress_kernel(kv_ref, pe_ref, w1_ref, w2_ref, out_ref, kv32_ref):
    n_chunks = kv_ref.shape[1] // CMP_STRIDE
    kv32_ref[...] = kv_ref[0].astype(F32)
    acc_lo = jnp.zeros((n_chunks, LANE), F32)
    acc_hi = jnp.zeros((n_chunks, LANE), F32)
    for i in range(CMP_STRIDE):
        rows = kv32_ref[pl.ds(i, n_chunks, stride=CMP_STRIDE), :]
        lo = (rows + pe_ref[i:i + 1, :]).astype(BF16)
        hi = (rows + pe_ref[CMP_STRIDE + i:CMP_STRIDE + i + 1, :]).astype(BF16)
        acc_lo += jnp.dot(lo, w1_ref[i], preferred_element_type=F32)
        acc_hi += jnp.dot(hi, w1_ref[CMP_STRIDE + i], preferred_element_type=F32)
    shifted = jnp.concatenate([acc_hi[1:], jnp.zeros((1, LANE), F32)], axis=0)
    hid = jax.nn.gelu(acc_lo + shifted)
    out = jnp.dot(hid.astype(BF16), w2_ref[...], preferred_element_type=F32)
    row = lax.broadcasted_iota(jnp.int32, out.shape, 0)
    out_ref[0, 0] = jnp.where(row < n_chunks - 1, out, 0.0).astype(out_ref.dtype)


def _compress(z3, pe, w1, w2):
    b, t_len, _ = z3.shape
    n_chunks = t_len // CMP_STRIDE
    return pl.pallas_call(
        _compress_kernel,
        out_shape=jax.ShapeDtypeStruct((b, NSA_KV_HEADS, n_chunks, LANE), BF16),
        grid=(b, NSA_KV_HEADS),
        in_specs=[pl.BlockSpec((1, t_len, LANE), lambda bi, g: (bi, 0, ZB_KVC + g)),
                  pl.BlockSpec((CMP_BLOCK, LANE), lambda bi, g: (0, 0)),
                  pl.BlockSpec((CMP_BLOCK, LANE, LANE), lambda bi, g: (0, 0, 0)),
                  pl.BlockSpec((LANE, LANE), lambda bi, g: (0, 0))],
        out_specs=pl.BlockSpec((1, 1, n_chunks, LANE), lambda bi, g: (bi, g, 0, 0)),
        scratch_shapes=[pltpu.VMEM((t_len, LANE), F32)],
        compiler_params=pltpu.CompilerParams(dimension_semantics=("parallel", "parallel"),
                                             vmem_limit_bytes=VMEM_LIMIT),
        name="kv_compress",
    )(z3, pe, w1, w2)


def _block_diag2(a, b):
    za = jnp.zeros(a.shape[:-1] + (b.shape[-1],), a.dtype)
    zb = jnp.zeros(b.shape[:-1] + (a.shape[-1],), b.dtype)
    return jnp.concatenate([jnp.concatenate([a, za], axis=-1), jnp.concatenate([zb, b], axis=-1)], axis=-2)


def _stack_heads(x):
    return jnp.concatenate([x[:, r * HEAD_DIM:(r + 1) * HEAD_DIM] for r in range(NSA_GROUP)], axis=0)


def _dot_nt(a, b):
    return lax.dot_general(a, b, (((1,), (1,)), ((), ())), preferred_element_type=F32)


def _eye(n, dtype):
    return jnp.where(lax.broadcasted_iota(jnp.int32, (n, n), 0) == lax.broadcasted_iota(jnp.int32, (n, n), 1),
                     1.0, 0.0).astype(dtype)


NSA_KEY_CHUNK = 256
NSA_V_ROWS = HEAD_DIM + 16


def _nsa_kernel(slopes_ref, *refs, n_sel, n_g):
    q_refs = refs[:n_g]
    (kvc_ref, kvs_ref, kvw_ref, gate_ref, ovl_ref, o_ref, vst_ref, vwt_ref, vct_ref, sel_ref,
     sc0_ref, sc1_ref, p0_ref, p1_ref) = refs[n_g:]
    g_base = pl.program_id(1) * n_g
    qi = pl.program_id(2)
    q0 = qi * Q_BLOCK
    cols = NSA_GROUP * Q_BLOCK
    kc_len = NSA_KEY_CHUNK
    blocks_per_chunk = kc_len // SEL_BLOCK
    n_chunks = kvs_ref.shape[1] // kc_len
    n_cmp_pad = kvc_ref.shape[2]
    n_slc = ovl_ref.shape[0]
    v_rows = vst_ref.shape[2]
    eye_dh = _eye(HEAD_DIM, BF16)
    k_lanes = lambda gg: slice(gg * LANE, gg * LANE + HEAD_DIM)
    v_lanes = lambda gg: slice(gg * LANE + HEAD_DIM, (gg + 1) * LANE)

    @pl.when(qi == 0)
    def _():
        ones_row = jnp.where(lax.broadcasted_iota(jnp.int32, (v_rows - HEAD_DIM, kc_len), 0) == 0,
                             1.0, 0.0).astype(BF16)

        def body(c, carry):
            rows = pl.ds(pl.multiple_of(c * kc_len, kc_len), kc_len)
            for gg in range(n_g):
                vst_ref[gg, c, 0:HEAD_DIM] = _dot_nt(eye_dh, kvs_ref[0, rows, v_lanes(gg)]).astype(BF16)
                vwt_ref[gg, c, 0:HEAD_DIM] = _dot_nt(eye_dh, kvw_ref[0, rows, v_lanes(gg)]).astype(BF16)
                vst_ref[gg, c, HEAD_DIM:v_rows] = ones_row
                vwt_ref[gg, c, HEAD_DIM:v_rows] = ones_row
            return carry

        lax.fori_loop(0, n_chunks, body, 0)
        for gg in range(n_g):
            vct_ref[gg] = _dot_nt(eye_dh, kvc_ref[0, gg, :, HEAD_DIM:2 * HEAD_DIM]).astype(BF16)

    log2e = float(np.log2(np.e))
    lane = lax.broadcasted_iota(jnp.int32, (1, cols), 1)
    head = lane // Q_BLOCK
    tq = q0 + lane % Q_BLOCK
    sub = lax.broadcasted_iota(jnp.int32, (kc_len, cols), 0)
    half = lax.broadcasted_iota(jnp.int32, (1, LANE), 1) // Q_BLOCK
    blk = lax.broadcasted_iota(jnp.int32, (n_slc, LANE), 0)
    blk8 = lax.broadcasted_iota(jnp.int32, (8, LANE), 0)
    cmp_end = lax.broadcasted_iota(jnp.int32, (n_cmp_pad, 1), 0) * CMP_STRIDE + (CMP_BLOCK - 1)
    d_cmp = tq - cmp_end
    m_cmp = d_cmp >= 0
    d_cmp_f = d_cmp.astype(F32)
    forced = (blk == 0) | (blk == qi) | (blk == qi - 1)
    ovl = ovl_ref[...]

    groups = range(n_g)
    qs, slopes = [], []
    for gg in groups:
        q = _stack_heads(q_refs[gg][0]).astype(F32)
        qs.append((q * (HEAD_DIM ** -0.5 * log2e)).astype(BF16))
        slope = jnp.zeros((1, cols), F32)
        for r in range(NSA_GROUP):
            slope = jnp.where(head == r, slopes_ref[(g_base + gg) * NSA_GROUP + r] * log2e, slope)
        slopes.append(slope)
    st = [_dot_nt(kvc_ref[0, gg, :, 0:HEAD_DIM], qs[gg]) for gg in groups]
    p_cmp = []
    for gg in groups:
        s_m = jnp.where(m_cmp, st[gg] - slopes[gg] * d_cmp_f, NEG_INF)
        e = jnp.where(m_cmp, jnp.exp2(s_m - jnp.max(s_m, axis=0, keepdims=True)), 0.0)
        p_cmp.append(e / jnp.maximum(jnp.sum(e, axis=0, keepdims=True), 1e-30))
    o_cmp = [jnp.dot(vct_ref[gg], p_cmp[gg].astype(BF16), preferred_element_type=F32) for gg in groups]
    parts = []
    for gg in groups:
        y = p_cmp[gg][:, 0:LANE] + p_cmp[gg][:, LANE:2 * LANE]
        parts.append(_split3(y + pltpu.roll(y, Q_BLOCK, 1)))
    pooled = [[jnp.dot(ovl, parts[gg][i], preferred_element_type=F32) for gg in groups] for i in range(3)]
    for gg in groups:
        imp = pooled[0][gg] + pooled[1][gg] + pooled[2][gg]
        imp = jnp.where(forced, jnp.inf, jnp.where(blk > qi, -jnp.inf, imp))
        tiles = [imp[t:t + 8] for t in range(0, n_slc, 8)]
        ranks = [jnp.zeros((8, LANE), F32) for _ in tiles]
        for jp in range(0, n_slc, 2):
            row = jnp.where(half == 0, imp[jp:jp + 1, :], imp[jp + 1:jp + 2, :])
            for ti, tile in enumerate(tiles):
                t0 = ti * 8
                if t0 > jp + 1:
                    hit = jnp.where(row >= tile, 1.0, 0.0)
                elif t0 + 7 <= jp:
                    hit = jnp.where(row > tile, 1.0, 0.0)
                else:
                    hit = jnp.where(blk8 + t0 > jp + half, jnp.where(row >= tile, 1.0, 0.0),
                                    jnp.where(row > tile, 1.0, 0.0))
                ranks[ti] = ranks[ti] + hit
        rank = jnp.concatenate(ranks, axis=0)
        rank = rank + pltpu.roll(rank, Q_BLOCK, 1)
        neg = jnp.where((rank < n_sel) & (blk <= qi), 0.0, NEG_INF)
        neg2 = jnp.concatenate([neg, neg], axis=1)
        for j in range(n_slc):
            sel_ref[gg, j] = jnp.broadcast_to(neg2[j:j + 1, :], (8, cols))
    sub_f = sub.astype(F32)
    fronts = [(qs[gg], slopes[gg], slopes[gg] * sub_f, o_cmp[gg]) for gg in groups]

    init1 = (jnp.full((1, cols), NEG_INF, F32), jnp.zeros((v_rows, cols), F32))

    def block_mask(gg, c):
        tiles = [sel_ref[gg, c * blocks_per_chunk + i] for i in range(blocks_per_chunk)]
        return jnp.concatenate([t for t in tiles for _ in range(SEL_BLOCK // 8)], axis=0)

    streams = {"slc": (kvs_ref, vst_ref), "win": (kvw_ref, vwt_ref)}
    sc_slots = (sc0_ref, sc1_ref)
    p_slots = (p0_ref, p1_ref)

    def score(gg, pos, slot):
        stream, c, _ = pos
        rows = pl.ds(pl.multiple_of(c * kc_len, kc_len), kc_len)
        sc_slots[slot][gg] = _dot_nt(streams[stream][0][0, rows, k_lanes(gg)], fronts[gg][0])

    def softmax(gg, pos, slot, m_prev):
        _, c, mask_fn = pos
        _, slope, bias_local, _ = fronts[gg]
        sc = sc_slots[slot][gg] + bias_local + mask_fn(gg)
        shift = slope * (c * kc_len - q0).astype(F32)
        m_new = jnp.maximum(m_prev, jnp.max(sc, axis=0, keepdims=True) + shift)
        p_slots[slot][gg] = jnp.exp2(sc - (m_new - shift)).astype(BF16)
        return m_new, jnp.exp2(m_prev - m_new)

    def weigh(gg, pos, slot, alpha, acc):
        stream, c, _ = pos
        return alpha * acc + jnp.dot(streams[stream][1][gg, c], p_slots[slot][gg], preferred_element_type=F32)

    def step(pos, slot, nxt, prev, state, alphas):
        if nxt is not None:
            for gg in range(n_g):
                score(gg, nxt, 1 - slot)
        state = {s: list(v) for s, v in state.items()}
        new_alphas = []
        for gg in range(n_g):
            m_prev, acc = state[pos[0]][gg]
            m_new, alpha = softmax(gg, pos, slot, m_prev)
            state[pos[0]][gg] = (m_new, acc)
            new_alphas.append(alpha)
            if prev is not None:
                m_p, acc_p = state[prev[0]][gg]
                state[prev[0]][gg] = (m_p, weigh(gg, prev, 1 - slot, alphas[gg], acc_p))
        return state, new_alphas

    c_cur = qi // blocks_per_chunk
    n_pairs = c_cur // 2
    plain = lambda c: ("slc", c, lambda gg: block_mask(gg, c))

    for gg in range(n_g):
        p1_ref[gg] = jnp.zeros((kc_len, cols), BF16)
        score(gg, plain(0), 0)

    def pair_body(i, carry):
        slc_state, alphas = carry
        k0 = 2 * i
        state, alphas = step(plain(k0), 0, plain(k0 + 1), plain(jnp.maximum(k0 - 1, 0)),
                             {"slc": slc_state}, alphas)
        state, alphas = step(plain(k0 + 1), 1, plain(k0 + 2), plain(k0), state, alphas)
        return tuple(state["slc"]), tuple(alphas)

    slc_state, alphas = lax.fori_loop(0, n_pairs, pair_body,
                                      (tuple(init1 for _ in range(n_g)),
                                       tuple(jnp.ones((1, cols), F32) for _ in range(n_g))))
    k_t = 2 * n_pairs
    odd_neg = jnp.where(c_cur % 2 == 1, 0.0, NEG_INF)
    causal_neg = jnp.where((c_cur * kc_len + sub) <= tq, 0.0, NEG_INF)
    tail = [("slc", k_t, lambda gg: block_mask(gg, k_t) + odd_neg),
            ("slc", c_cur, lambda gg: block_mask(gg, c_cur) + causal_neg)]
    for i in range(WINDOW // kc_len, -1, -1):
        c_raw = c_cur - i
        c_win = jnp.maximum(c_raw, 0)
        dist = tq - (c_win * kc_len + sub)
        band_neg = jnp.where((dist >= 0) & (dist < WINDOW) & (c_raw >= 0), 0.0, NEG_INF)
        tail.append(("win", c_win, lambda gg, band_neg=band_neg: band_neg))
    state = {"slc": list(slc_state), "win": [init1 for _ in range(n_g)]}
    prev = plain(jnp.maximum(k_t - 1, 0))
    for idx, pos in enumerate(tail):
        nxt = tail[idx + 1] if idx + 1 < len(tail) else None
        state, alphas = step(pos, idx % 2, nxt, prev, state, alphas)
        prev = pos
    last_slot = (len(tail) - 1) % 2
    for gg in range(n_g):
        m_p, acc_p = state[prev[0]][gg]
        state[prev[0]][gg] = (m_p, weigh(gg, prev, last_slot, alphas[gg], acc_p))

    eye_q = _eye(Q_BLOCK, BF16)
    eye_lane = _eye(LANE, BF16)
    gates = [jax.nn.sigmoid(_dot_nt(eye_lane, gate_ref[0, :, gg * LANE:(gg + 1) * LANE])) for gg in groups]

    def gate_row(gg, i):
        return jnp.concatenate([gates[gg][i * NSA_GROUP + r:i * NSA_GROUP + r + 1, :] for r in range(NSA_GROUP)],
                               axis=1)

    def normalized(acc):
        return acc[0:HEAD_DIM] / jnp.maximum(acc[HEAD_DIM:HEAD_DIM + 1], 1e-30)

    o_rows = []
    for gg in groups:
        o = (gate_row(gg, 0) * fronts[gg][3] + gate_row(gg, 1) * normalized(state["slc"][gg][1])
             + gate_row(gg, 2) * normalized(state["win"][gg][1])).astype(BF16)
        o_rows.append(jnp.concatenate([o[:, r * Q_BLOCK:(r + 1) * Q_BLOCK] for r in range(NSA_GROUP)], axis=0))
    outs = [_dot_nt(eye_q, o_rows[gg]) for gg in groups]
    for gg in groups:
        o_ref[0, :, gg * cols:(gg + 1) * cols] = outs[gg].astype(o_ref.dtype)


NSA_GROUPS_PER_STEP = 4


def _nsa_attention(z3, kv_cmp):
    b, t_len, _ = z3.shape
    n_q = t_len // Q_BLOCK
    n_slc = t_len // SEL_BLOCK
    n_cmp_pad = kv_cmp.shape[2]
    n_sel = min(N_SELECT, n_slc)
    n_g = NSA_GROUPS_PER_STEP
    cols = NSA_GROUP * Q_BLOCK
    slopes = jnp.exp2(-8.0 * jnp.arange(1, NSA_HEADS + 1, dtype=F32) / NSA_HEADS)
    c_idx = np.arange(n_cmp_pad)
    cmp_start = c_idx * CMP_STRIDE
    cmp_end = cmp_start + CMP_BLOCK - 1
    slc_start = np.arange(n_slc) * SEL_BLOCK
    overlap = np.clip(np.minimum(cmp_end[None, :], slc_start[:, None] + SEL_BLOCK - 1)
                      - np.maximum(cmp_start[None, :], slc_start[:, None]) + 1, 0, None).astype(np.float32)
    overlap[:, c_idx >= t_len // CMP_STRIDE - CMP_BLOCK // CMP_STRIDE + 1] = 0.0
    kernel = functools.partial(_nsa_kernel, n_sel=n_sel, n_g=n_g)
    n_kc = t_len // NSA_KEY_CHUNK
    q_spec = lambda gg: pl.BlockSpec((1, Q_BLOCK, cols),
                                     lambda bi, g, qi: (bi, qi, ZB_Q * LANE // cols + g * n_g + gg))
    slab = lambda zb: pl.BlockSpec((1, t_len, n_g * LANE), lambda bi, g, qi: (bi, 0, zb // n_g + g))
    assert ZB_KVS % n_g == 0 and ZB_KVW % n_g == 0 and ZB_GATE % n_g == 0
    return pl.pallas_call(
        kernel,
        out_shape=jax.ShapeDtypeStruct((b, t_len, NSA_WIDTH), BF16),
        grid=(b, NSA_KV_HEADS // n_g, n_q),
        in_specs=[pl.BlockSpec(memory_space=pltpu.SMEM)] + [q_spec(gg) for gg in range(n_g)] + [
            pl.BlockSpec((1, n_g, n_cmp_pad, LANE), lambda bi, g, qi: (bi, g, 0, 0)),
            slab(ZB_KVS), slab(ZB_KVW),
            pl.BlockSpec((1, Q_BLOCK, n_g * LANE), lambda bi, g, qi: (bi, qi, ZB_GATE // n_g + g)),
            pl.BlockSpec((n_slc, n_cmp_pad), lambda bi, g, qi: (0, 0)),
        ],
        out_specs=pl.BlockSpec((1, Q_BLOCK, n_g * cols), lambda bi, g, qi: (bi, qi, g)),
        scratch_shapes=[pltpu.VMEM((n_g, n_kc, NSA_V_ROWS, NSA_KEY_CHUNK), BF16),
                        pltpu.VMEM((n_g, n_kc, NSA_V_ROWS, NSA_KEY_CHUNK), BF16),
                        pltpu.VMEM((n_g, HEAD_DIM, n_cmp_pad), BF16),
                        pltpu.VMEM((n_g, n_slc, 8, cols), F32),
                        pltpu.VMEM((n_g, NSA_KEY_CHUNK, cols), F32),
                        pltpu.VMEM((n_g, NSA_KEY_CHUNK, cols), F32),
                        pltpu.VMEM((n_g, NSA_KEY_CHUNK, cols), BF16),
                        pltpu.VMEM((n_g, NSA_KEY_CHUNK, cols), BF16)],
        compiler_params=pltpu.CompilerParams(dimension_semantics=("parallel", "parallel", "arbitrary"),
                                             vmem_limit_bytes=VMEM_LIMIT),
        name="nsa_attention",
    )(slopes, *([z3] * n_g), kv_cmp, z3, z3, z3, jnp.asarray(overlap, BF16))


def _nsa_prepare(cmp_pe_k, cmp_w1_k, cmp_w2_k, cmp_pe_v, cmp_w1_v, cmp_w2_v):
    pe = jnp.concatenate([cmp_pe_k, cmp_pe_v], axis=-1)
    w1 = _block_diag2(cmp_w1_k.reshape(CMP_BLOCK, HEAD_DIM, HEAD_DIM),
                      cmp_w1_v.reshape(CMP_BLOCK, HEAD_DIM, HEAD_DIM)).astype(BF16)
    w2 = _block_diag2(cmp_w2_k, cmp_w2_v).astype(BF16)
    return pe, w1, w2


RWKV_CHUNK = 64


def _rwkv_prep_kernel(z_ref, mu_ref, w0_ref, wup_ref, a0_ref, aup_ref, gup_ref, kk_ref, ka_ref,
                      r_ref, k_ref, v_ref, kkr_ref, a_ref, lw_ref, g_ref, carry_ref):
    w = RWKV_WIDTH

    @pl.when(pl.program_id(1) == 0)
    def _():
        carry_ref[...] = jnp.zeros_like(carry_ref)

    z = z_ref[0].astype(F32)
    tc = z.shape[0]
    row = lax.broadcasted_iota(jnp.int32, z.shape, 0)
    prev = jnp.where(row == 0, carry_ref[0:1, :], pltpu.roll(z, 1, 0))
    carry_ref[0:1, :] = z[tc - 1:tc, :]
    zs = z + (prev - z) * mu_ref[...]
    r = zs[:, RB_R * LANE:RB_R * LANE + w]
    k = zs[:, RB_K * LANE:RB_K * LANE + w]
    v = zs[:, RB_V * LANE:RB_V * LANE + w]
    w_lo = zs[:, RB_WLO * LANE:(RB_WLO + 1) * LANE]
    a_lo = zs[:, RB_ALO * LANE:(RB_ALO + 1) * LANE]
    g_lo = zs[:, RB_GLO * LANE:RB_GLO * LANE + GATE_LORA]
    d = w0_ref[...] + jnp.dot(jnp.tanh(w_lo).astype(BF16), wup_ref[...], preferred_element_type=F32)
    w_raw = -jax.nn.softplus(-d) - 0.5
    lw_ref[0] = -jnp.exp(w_raw)
    a = jax.nn.sigmoid(a0_ref[...] + jnp.dot(a_lo.astype(BF16), aup_ref[...], preferred_element_type=F32))
    g = jnp.dot(jax.nn.sigmoid(g_lo).astype(BF16), gup_ref[...], preferred_element_type=F32)
    r_ref[0] = r.astype(r_ref.dtype)
    v_ref[0] = v.astype(v_ref.dtype)
    kkr_ref[0] = (k * kk_ref[...]).astype(kkr_ref.dtype)
    k_ref[0] = (k * (1.0 + (a - 1.0) * ka_ref[...])).astype(k_ref.dtype)
    a_ref[0] = a.astype(a_ref.dtype)
    g_ref[0] = g.astype(g_ref.dtype)


def _rwkv_prep(z3, mu, w0, w_up, a0, a_up, g_up, k_k, k_a, tc=256):
    b, t_len, _ = z3.shape
    w = RWKV_WIDTH
    ncol = RWKV_BLOCKS * LANE
    vec = lambda n: pl.BlockSpec((1, n), lambda bi, ti: (0, 0))
    mat = lambda m, n: pl.BlockSpec((m, n), lambda bi, ti: (0, 0))
    out_bf = jax.ShapeDtypeStruct((b, t_len, w), BF16)
    out_f32 = jax.ShapeDtypeStruct((b, t_len, w), F32)
    out_spec = pl.BlockSpec((1, tc, w), lambda bi, ti: (bi, ti, 0))
    return pl.pallas_call(
        _rwkv_prep_kernel,
        out_shape=(out_bf, out_bf, out_bf, out_bf, out_bf, out_f32, out_bf),
        grid=(b, t_len // tc),
        in_specs=[pl.BlockSpec((1, tc, ncol), lambda bi, ti: (bi, ti, ZB_RWKV)),
                  vec(ncol), vec(w), mat(LANE, w), vec(w), mat(LANE, w), mat(GATE_LORA, w), vec(w), vec(w)],
        out_specs=(out_spec,) * 7,
        scratch_shapes=[pltpu.VMEM((8, ncol), F32)],
        compiler_params=pltpu.CompilerParams(dimension_semantics=("parallel", "arbitrary"),
                                             vmem_limit_bytes=VMEM_LIMIT),
        name="rwkv_prep",
    )(z3, mu, w0, w_up, a0, a_up, g_up, k_k, k_a)


def _pair_blocks(x):
    lane = lax.broadcasted_iota(jnp.int32, x.shape, 1)
    zero = jnp.zeros((), x.dtype)
    return jnp.concatenate([jnp.where(lane < HEAD_DIM, x, zero), jnp.where(lane >= HEAD_DIM, x, zero)], axis=0)


def _fold_pair(x):
    n = x.shape[0] // 2
    return x[:n] + x[n:]


def _split3(x):
    hi = x.astype(BF16)
    r1 = x - hi.astype(F32)
    mid = r1.astype(BF16)
    return hi, mid, (r1 - mid.astype(F32)).astype(BF16)


def _dot_split_rhs(a_bf, x):
    hi, mid, lo = _split3(x)
    return (jnp.dot(a_bf, hi, preferred_element_type=F32) + jnp.dot(a_bf, mid, preferred_element_type=F32)
            + jnp.dot(a_bf, lo, preferred_element_type=F32))


RWKV_PAIRS_PER_STEP = 4


def _rwkv_scan_kernel(r_ref, k_ref, v_ref, kkr_ref, a_ref, lw_ref, g_ref, rk_ref, lng_ref, lnb_ref,
                      o_ref, s_ref, *, n_chunks, n_pairs):
    L = RWKV_CHUNK
    L2 = 2 * L

    @pl.when(pl.program_id(2) == 0)
    def _():
        s_ref[...] = jnp.zeros_like(s_ref)

    ri = lax.broadcasted_iota(jnp.int32, (L, L), 0)
    ci = lax.broadcasted_iota(jnp.int32, (L, L), 1)
    tri_incl = jnp.where(ri >= ci, 1.0, 0.0).astype(BF16)
    r2 = lax.broadcasted_iota(jnp.int32, (L2, L2), 0)
    c2 = lax.broadcasted_iota(jnp.int32, (L2, L2), 1)
    same_head = (r2 // L) == (c2 // L)
    strict2 = same_head & (r2 > c2)
    incl2 = same_head & (r2 >= c2)
    eye2 = jnp.where(r2 == c2, 1.0, 0.0).astype(F32)
    h_r = lax.broadcasted_iota(jnp.int32, (LANE, LANE), 0) // HEAD_DIM
    h_c = lax.broadcasted_iota(jnp.int32, (LANE, LANE), 1) // HEAD_DIM
    head_mask = h_r == h_c
    head_ones = jnp.where(head_mask, 1.0, 0.0).astype(BF16)

    def head_sum(x):
        hi, mid, _ = _split3(x)
        return (jnp.dot(hi, head_ones, preferred_element_type=F32)
                + jnp.dot(mid, head_ones, preferred_element_type=F32))

    def mm(a, b):
        return jnp.dot(a.astype(BF16), b.astype(BF16), preferred_element_type=F32)

    chains = [(hp, c) for hp in range(n_pairs) for c in range(n_chunks)]
    lanes_of = lambda hp: slice(hp * LANE, (hp + 1) * LANE)
    rows_of = lambda c: pl.ds(c * L, L)
    each = lambda fn: {ch: fn(ch) for ch in chains}

    load = lambda ref: each(lambda ch: ref[0, rows_of(ch[1]), lanes_of(ch[0])])
    r, k, v, a = (each(lambda ch, d=d: d[ch].astype(F32)) for d in (load(r_ref), load(k_ref), load(v_ref), load(a_ref)))
    kkr = each(lambda ch, d=load(kkr_ref): d[ch].astype(F32))
    lw = load(lw_ref)
    kk_sq = each(lambda ch: head_sum(kkr[ch] * kkr[ch]))
    cl = each(lambda ch: _dot_split_rhs(tri_incl, lw[ch]))
    kk = each(lambda ch: kkr[ch] / jnp.maximum(jnp.sqrt(kk_sq[ch]), 1e-12))
    p_incl = each(lambda ch: jnp.exp(cl[ch]))
    p_inv = each(lambda ch: jnp.exp(-cl[ch]))
    rt = each(lambda ch: (r[ch] * p_incl[ch]).astype(BF16))
    at = each(lambda ch: (-kk[ch] * jnp.exp(cl[ch] - lw[ch])).astype(BF16))
    kt = each(lambda ch: k[ch] * p_inv[ch])
    bt = each(lambda ch: kk[ch] * a[ch] * p_inv[ch])
    gram = each(lambda ch: _dot_nt(
        jnp.concatenate([_pair_blocks(at[ch]), _pair_blocks(rt[ch])], axis=0),
        jnp.concatenate([_pair_blocks(bt[ch]), _pair_blocks(kt[ch])], axis=0).astype(BF16)))
    a_ab = each(lambda ch: jnp.where(strict2, gram[ch][:L2, :L2], 0.0))
    a_ak = each(lambda ch: jnp.where(strict2, gram[ch][:L2, L2:], 0.0).astype(BF16))
    a_rb = each(lambda ch: jnp.where(incl2, gram[ch][L2:, :L2], 0.0).astype(BF16))
    a_rk = each(lambda ch: jnp.where(incl2, gram[ch][L2:, L2:], 0.0).astype(BF16))
    tinv = each(lambda ch: eye2 + a_ab[ch])
    apow = a_ab
    for _ in range(int(np.log2(L)) - 1):
        apow = each(lambda ch: mm(apow[ch], apow[ch]))
        tinv = each(lambda ch: tinv[ch] + mm(tinv[ch], apow[ch]))
    tinv = each(lambda ch: tinv[ch].astype(BF16))
    v_bf = each(lambda ch: v[ch].astype(BF16))
    v2 = each(lambda ch: _pair_blocks(v_bf[ch]))
    akv = each(lambda ch: _fold_pair(jnp.dot(a_ak[ch], v2[ch], preferred_element_type=F32)))
    rkv = each(lambda ch: _fold_pair(jnp.dot(a_rk[ch], v2[ch], preferred_element_type=F32)))
    p_last = each(lambda ch: p_incl[ch][L - 1:L, :])
    wts = each(lambda ch: jnp.concatenate([bt[ch] * p_last[ch], kt[ch] * p_last[ch]], axis=0).astype(BF16))
    bonus = each(lambda ch: head_sum(r[ch] * k[ch] * rk_ref[:, lanes_of(ch[0])]) * v[ch])

    pairs = range(n_pairs)
    states = [s_ref[hp] for hp in pairs]
    for c in range(n_chunks):
        s_bf = [states[hp].astype(BF16) for hp in pairs]
        m = [_dot_nt(at[hp, c], s_bf[hp]) + akv[hp, c] for hp in pairs]
        y = [_dot_nt(rt[hp, c], s_bf[hp]) + rkv[hp, c] for hp in pairs]
        u = [_fold_pair(jnp.dot(tinv[hp, c], _pair_blocks(m[hp].astype(BF16)), preferred_element_type=F32))
             for hp in pairs]
        u_bf = [u[hp].astype(BF16) for hp in pairs]
        upd = [lax.dot_general(jnp.concatenate([u_bf[hp], v_bf[hp, c]], axis=0), wts[hp, c],
                               (((0,), (0,)), ((), ())), preferred_element_type=F32) for hp in pairs]
        states = [states[hp] * p_last[hp, c] + jnp.where(head_mask, upd[hp], 0.0) for hp in pairs]
        y = [y[hp] + _fold_pair(jnp.dot(a_rb[hp, c], _pair_blocks(u_bf[hp]), preferred_element_type=F32))
             for hp in pairs]
        mean = [head_sum(y[hp]) * (1.0 / HEAD_DIM) for hp in pairs]
        yc = [y[hp] - mean[hp] for hp in pairs]
        var = [head_sum(yc[hp] * yc[hp]) * (1.0 / HEAD_DIM) for hp in pairs]
        for hp in pairs:
            yn = yc[hp] * lax.rsqrt(var[hp] + GN_EPS) * lng_ref[:, lanes_of(hp)] + lnb_ref[:, lanes_of(hp)]
            o_ref[0, rows_of(c), lanes_of(hp)] = (
                (yn + bonus[hp, c]) * g_ref[0, rows_of(c), lanes_of(hp)].astype(F32)).astype(o_ref.dtype)
    for hp in pairs:
        s_ref[hp] = states[hp]


def _rwkv_scan(r, k, v, kkr, a, lw, g, r_k, ln_g, ln_b, tt=256):
    b, t_len, w = r.shape
    n_p = RWKV_PAIRS_PER_STEP
    width = n_p * LANE
    tile = pl.BlockSpec((1, tt, width), lambda bi, hp, ti: (bi, ti, hp))
    vec = pl.BlockSpec((1, width), lambda bi, hp, ti: (0, hp))
    kernel = functools.partial(_rwkv_scan_kernel, n_chunks=tt // RWKV_CHUNK, n_pairs=n_p)
    return pl.pallas_call(
        kernel,
        out_shape=jax.ShapeDtypeStruct((b, t_len, w), BF16),
        grid=(b, w // width, t_len // tt),
        in_specs=[tile] * 7 + [vec] * 3,
        out_specs=tile,
        scratch_shapes=[pltpu.VMEM((n_p, LANE, LANE), F32)],
        compiler_params=pltpu.CompilerParams(dimension_semantics=("parallel", "parallel", "arbitrary"),
                                             vmem_limit_bytes=VMEM_LIMIT),
        name="rwkv_scan",
    )(r, k, v, kkr, a, lw, g, r_k, ln_g, ln_b)


def _rwkv_time_mix(z3, mu, w0, w_up, a0, a_up, g_up, k_k, k_a, r_k, ln_g, ln_b):
    row = lambda u: u.reshape(1, -1)
    mu_p = row(_take_columns(mu, _rwkv_source_columns()))
    r, k, v, kkr, a, lw, g = _rwkv_prep(z3, mu_p, row(w0), _pad_rows(w_up, LANE).astype(BF16), row(a0),
                                        _pad_rows(a_up, LANE).astype(BF16), g_up.astype(BF16), row(k_k), row(k_a))
    return _rwkv_scan(r, k, v, kkr, a, lw, g, row(r_k), row(ln_g), row(ln_b))


def _layer_norm(x, g, b):
    mean = jnp.mean(x, axis=-1, keepdims=True)
    xc = x - mean
    var = jnp.mean(xc * xc, axis=-1, keepdims=True)
    return xc * lax.rsqrt(var + LN_EPS) * g + b


def _out_proj_kernel(on_ref, or_ref, x_ref, w_ref, g_ref, b_ref, rwh_ref, rwm_ref, rb_ref,
                     h_ref, idx_ref, wgt_ref, *, alpha):
    o = jnp.concatenate([on_ref[...], or_ref[...]], axis=1)
    mix = jnp.dot(o, w_ref[...], preferred_element_type=F32)
    h = _layer_norm(alpha * x_ref[...] + mix, g_ref[...], b_ref[...])
    h_ref[...] = h
    h_hi, h_mid, _ = _split3(h)
    logits = (jnp.dot(h_hi, rwh_ref[...], preferred_element_type=F32)
              + jnp.dot(h_hi, rwm_ref[...], preferred_element_type=F32)
              + jnp.dot(h_mid, rwh_ref[...], preferred_element_type=F32)) + rb_ref[...]
    lane = lax.broadcasted_iota(jnp.int32, logits.shape, 1)
    logits = jnp.where(lane < N_EXPERTS, logits, -jnp.inf)
    idx_out = jnp.zeros(logits.shape, jnp.int32)
    val_out = jnp.full(logits.shape, -jnp.inf, F32)
    for k in range(TOP_K):
        best = jnp.max(logits, axis=-1, keepdims=True)
        first = jnp.min(jnp.where(logits == best, lane, LANE), axis=-1, keepdims=True)
        idx_out = jnp.where(lane == k, first, idx_out)
        val_out = jnp.where(lane == k, best, val_out)
        logits = jnp.where(lane == first, -jnp.inf, logits)
    e = jnp.exp(val_out - jnp.max(val_out, axis=-1, keepdims=True))
    idx_ref[...] = idx_out
    wgt_ref[...] = e / jnp.sum(e, axis=-1, keepdims=True)


def _out_proj_router(o_nsa, o_rwkv, x2, w_out, ln_g, ln_b, router_w, router_b, alpha, tm=512):
    n_tok, d = x2.shape
    half = o_nsa.shape[1]
    rw = jnp.pad(router_w, ((0, 0), (0, LANE - N_EXPERTS)))
    rw_hi = rw.astype(BF16)
    rw_mid = (rw - rw_hi.astype(F32)).astype(BF16)
    rb = jnp.pad(router_b, (0, LANE - N_EXPERTS)).reshape(1, LANE)
    row_blk = lambda n: pl.BlockSpec((tm, n), lambda i: (i, 0))
    full = lambda m, n: pl.BlockSpec((m, n), lambda i: (0, 0))
    return pl.pallas_call(
        functools.partial(_out_proj_kernel, alpha=alpha),
        out_shape=(jax.ShapeDtypeStruct((n_tok, d), F32), jax.ShapeDtypeStruct((n_tok, LANE), jnp.int32),
                   jax.ShapeDtypeStruct((n_tok, LANE), F32)),
        grid=(n_tok // tm,),
        in_specs=[row_blk(half), row_blk(half), row_blk(d), full(2 * half, d), full(1, d), full(1, d),
                  full(d, LANE), full(d, LANE), full(1, LANE)],
        out_specs=(row_blk(d), row_blk(LANE), row_blk(LANE)),
        compiler_params=pltpu.CompilerParams(dimension_semantics=("parallel",), vmem_limit_bytes=VMEM_LIMIT),
        name="out_proj_router",
    )(o_nsa, o_rwkv, x2, w_out.astype(BF16), ln_g.reshape(1, d), ln_b.reshape(1, d), rw_hi, rw_mid, rb)


MOE_ITEM_ROWS = 1024
MOE_SUB_ROWS = 256
MOE_F_TILE = 256
MOE_TOKEN_TILE = 256


MOE_RANK_BLOCK = 512


def _moe_rank_kernel(e_ref, rank_ref, cnt_ref, carry_ref):
    @pl.when(pl.program_id(0) == 0)
    def _():
        carry_ref[...] = jnp.zeros_like(carry_ref)

    blk = e_ref.shape[0]
    lane = lax.broadcasted_iota(jnp.int32, (blk, LANE), 1)
    onehot = jnp.where(e_ref[...] == lane, 1.0, 0.0)
    tri = jnp.where(lax.broadcasted_iota(jnp.int32, (blk, blk), 0) >= lax.broadcasted_iota(jnp.int32, (blk, blk), 1),
                    1.0, 0.0).astype(BF16)
    csum = jnp.dot(tri, onehot.astype(BF16), preferred_element_type=F32)
    carry = carry_ref[...]
    rank_ref[...] = (jnp.sum((csum + carry) * onehot, axis=1, keepdims=True) - 1.0).astype(jnp.int32)
    carry_ref[...] = carry + csum[blk - 1:blk, :]
    cnt_ref[...] = carry_ref[...].astype(jnp.int32)


def _moe_rank(flat_e):
    n_assign = flat_e.shape[0]
    return pl.pallas_call(
        _moe_rank_kernel,
        out_shape=(jax.ShapeDtypeStruct((n_assign, 1), jnp.int32), jax.ShapeDtypeStruct((1, LANE), jnp.int32)),
        grid=(n_assign // MOE_RANK_BLOCK,),
        in_specs=[pl.BlockSpec((MOE_RANK_BLOCK, 1), lambda i: (i, 0))],
        out_specs=(pl.BlockSpec((MOE_RANK_BLOCK, 1), lambda i: (i, 0)), pl.BlockSpec((1, LANE), lambda i: (0, 0))),
        scratch_shapes=[pltpu.VMEM((1, LANE), F32)],
        compiler_params=pltpu.CompilerParams(dimension_semantics=("arbitrary",), vmem_limit_bytes=VMEM_LIMIT),
        name="moe_rank",
    )(flat_e.reshape(n_assign, 1))


def _moe_tables(top_idx, n_items):
    n_tok = top_idx.shape[0]
    flat_e = top_idx.reshape(-1)
    rank, counts = _moe_rank(flat_e)
    counts = counts[0, :N_EXPERTS]
    items_e = (counts + MOE_ITEM_ROWS - 1) // MOE_ITEM_ROWS
    items_end = jnp.cumsum(items_e)
    item_start_e = items_end - items_e
    dest = jnp.take(item_start_e * MOE_ITEM_ROWS, flat_e) + rank[:, 0]
    item = jnp.arange(n_items, dtype=jnp.int32)
    valid = item < items_end[-1]
    last_e = jnp.max(jnp.where(counts > 0, jnp.arange(N_EXPERTS, dtype=jnp.int32), 0))
    item_e = jnp.minimum(jnp.sum(items_end[None, :] <= item[:, None], axis=1).astype(jnp.int32), N_EXPERTS - 1)
    item_e = jnp.where(valid, item_e, last_e)
    item_nv = jnp.where(valid, jnp.clip(counts[item_e] - (item - item_start_e[item_e]) * MOE_ITEM_ROWS,
                                        0, MOE_ITEM_ROWS), 0).astype(jnp.int32)
    return dest.reshape(n_tok, TOP_K), item_e, item_nv


DISPATCH_ISSUE_UNROLL = 8


def _dispatch_kernel(nv_ref, dest_ref, h_ref, xg_ref, stage_ref, zero_ref, sem_ref, zsem_ref):
    tm = h_ref.shape[0]
    i = pl.program_id(0)
    slot = i % 2
    subs_per_item = MOE_ITEM_ROWS // MOE_SUB_ROWS

    @pl.when(i == 0)
    def _():
        zero_ref[...] = jnp.zeros_like(zero_ref)

        def fill(sub):
            return pltpu.make_async_copy(
                zero_ref, xg_ref.at[pl.ds(pl.multiple_of(sub * MOE_SUB_ROWS, MOE_SUB_ROWS), MOE_SUB_ROWS)], zsem_ref)

        def not_full(sub):
            return nv_ref[sub // subs_per_item] < (sub % subs_per_item + 1) * MOE_SUB_ROWS

        def start_body(sub, carry):
            @pl.when(not_full(sub))
            def _():
                fill(sub).start()
            return carry

        def wait_body(sub, carry):
            @pl.when(not_full(sub))
            def _():
                fill(sub).wait()
            return carry

        n_subs = xg_ref.shape[0] // MOE_SUB_ROWS
        lax.fori_loop(0, n_subs, start_body, 0)
        lax.fori_loop(0, n_subs, wait_body, 0)

    def wait_slot(s):
        for _ in range(TOP_K):
            pltpu.make_async_copy(stage_ref.at[s], xg_ref.at[pl.ds(0, tm)], sem_ref.at[s]).wait()

    @pl.when(i >= 2)
    def _():
        wait_slot(slot)

    stage_ref[slot] = h_ref[...]

    def body(j, carry):
        for u in range(DISPATCH_ISSUE_UNROLL):
            t = j * DISPATCH_ISSUE_UNROLL + u
            for k in range(TOP_K):
                pltpu.make_async_copy(stage_ref.at[slot, pl.ds(t, 1)], xg_ref.at[pl.ds(dest_ref[0, 0, k * tm + t], 1)],
                                      sem_ref.at[slot]).start()
        return carry

    lax.fori_loop(0, tm // DISPATCH_ISSUE_UNROLL, body, 0)

    @pl.when(i == pl.num_programs(0) - 1)
    def _():
        wait_slot(slot)

        @pl.when(i >= 1)
        def _():
            wait_slot(1 - slot)


def _dispatch(h1, dest_tiles, item_nv, n_rows, tm):
    n_tok, d = h1.shape
    return pl.pallas_call(
        _dispatch_kernel,
        out_shape=jax.ShapeDtypeStruct((n_rows, d), F32),
        grid=(n_tok // tm,),
        in_specs=[pl.BlockSpec(memory_space=pltpu.SMEM),
                  pl.BlockSpec((1, 1, TOP_K * tm), lambda i: (i, 0, 0), memory_space=pltpu.SMEM),
                  pl.BlockSpec((tm, d), lambda i: (i, 0))],
        out_specs=pl.BlockSpec(memory_space=pl.ANY),
        scratch_shapes=[pltpu.VMEM((2, tm, d), F32), pltpu.VMEM((MOE_SUB_ROWS, d), F32),
                        pltpu.SemaphoreType.DMA((2,)), pltpu.SemaphoreType.DMA(())],
        compiler_params=pltpu.CompilerParams(dimension_semantics=("arbitrary",), vmem_limit_bytes=VMEM_LIMIT),
        name="moe_dispatch",
    )(item_nv, dest_tiles, h1)


def _moe_kernel(item_e_ref, item_nv_ref, x_ref, wg_ref, bg_ref, wu_ref, bu_ref, wd_ref, bd_ref, o_ref):
    i = pl.program_id(0)
    f = pl.program_id(1)
    nv = item_nv_ref[i]
    n_sub_max = MOE_ITEM_ROWS // MOE_SUB_ROWS
    n_sub = (nv + MOE_SUB_ROWS - 1) // MOE_SUB_ROWS
    rows = [pl.ds(sb * MOE_SUB_ROWS, MOE_SUB_ROWS) for sb in range(n_sub_max)]

    @pl.when(f == 0)
    def _():
        for sb in range(n_sub_max):
            bias = jnp.broadcast_to(bd_ref[0], (MOE_SUB_ROWS, o_ref.shape[1]))
            o_ref[rows[sb], :] = jnp.where(sb * MOE_SUB_ROWS < nv, bias, 0.0)

    for n in range(1, n_sub_max + 1):
        @pl.when(n_sub == n)
        def _(n=n):
            wg = wg_ref[0].astype(BF16)
            wu = wu_ref[0].astype(BF16)
            wd = wd_ref[0].astype(BF16)
            xs = [x_ref[rows[sb], :].astype(BF16) for sb in range(n)]
            gates = [jnp.dot(xs[sb], wg, preferred_element_type=F32) for sb in range(n)]
            ups = [jnp.dot(xs[sb], wu, preferred_element_type=F32) for sb in range(n)]
            for sb in range(n):
                gate = jnp.minimum(gates[sb] + bg_ref[0], SWIGLU_LIMIT)
                up = jnp.clip(ups[sb] + bu_ref[0], -SWIGLU_LIMIT, SWIGLU_LIMIT)
                h = gate * jax.nn.sigmoid(SWIGLU_ALPHA * gate) * (up + 1.0)
                o_ref[rows[sb], :] += jnp.dot(h.astype(BF16), wd, preferred_element_type=F32)


def _moe_experts(xg, item_e, item_nv, w_gate, b_gate, w_up, b_up, w_down, b_down):
    n_rows = xg.shape[0]
    d = w_gate.shape[1]
    n_items = n_rows // MOE_ITEM_ROWS
    n_e, _, d_ff = w_gate.shape
    n_f = d_ff // MOE_F_TILE

    def f_idx(i, f, nv):
        return jnp.where(nv[i] > 0, f, n_f - 1)

    grid_spec = pltpu.PrefetchScalarGridSpec(
        num_scalar_prefetch=2,
        grid=(n_items, n_f),
        in_specs=[
            pl.BlockSpec((MOE_ITEM_ROWS, d), lambda i, f, e, nv: (i, 0)),
            pl.BlockSpec((1, d, MOE_F_TILE), lambda i, f, e, nv: (e[i], 0, f_idx(i, f, nv))),
            pl.BlockSpec((1, 1, MOE_F_TILE), lambda i, f, e, nv: (e[i], 0, f_idx(i, f, nv))),
            pl.BlockSpec((1, d, MOE_F_TILE), lambda i, f, e, nv: (e[i], 0, f_idx(i, f, nv))),
            pl.BlockSpec((1, 1, MOE_F_TILE), lambda i, f, e, nv: (e[i], 0, f_idx(i, f, nv))),
            pl.BlockSpec((1, MOE_F_TILE, d), lambda i, f, e, nv: (e[i], f_idx(i, f, nv), 0)),
            pl.BlockSpec((1, 1, d), lambda i, f, e, nv: (e[i], 0, 0)),
        ],
        out_specs=pl.BlockSpec((MOE_ITEM_ROWS, d), lambda i, f, e, nv: (i, 0)),
    )
    return pl.pallas_call(
        _moe_kernel,
        out_shape=jax.ShapeDtypeStruct((n_rows, d), F32),
        grid_spec=grid_spec,
        compiler_params=pltpu.CompilerParams(dimension_semantics=("parallel", "arbitrary"),
                                             vmem_limit_bytes=VMEM_LIMIT),
        name="moe_experts",
    )(item_e, item_nv, xg, w_gate, b_gate.reshape(n_e, 1, d_ff), w_up, b_up.reshape(n_e, 1, d_ff), w_down,
      b_down.reshape(n_e, 1, d))


COMBINE_ISSUE_UNROLL = 16


def _final_kernel(dest_ref, dest_next_ref, h_ref, eo_ref, wgt_ref, p_ref, g_ref, b_ref, wgate_ref, wple_ref,
                  o_ref, rows_ref, sem_ref, *, alpha):
    tm, d = h_ref.shape
    n_rows = TOP_K * tm
    i = pl.program_id(0)
    slot = i % 2

    def row_copy(idx_ref, r, s):
        return pltpu.make_async_copy(eo_ref.at[pl.ds(idx_ref[0, 0, r], 1)], rows_ref.at[s, pl.ds(r, 1)],
                                     sem_ref.at[s])

    def issue(idx_ref, s):
        def body(j, carry):
            for u in range(COMBINE_ISSUE_UNROLL):
                row_copy(idx_ref, j * COMBINE_ISSUE_UNROLL + u, s).start()
            return carry

        lax.fori_loop(0, n_rows // COMBINE_ISSUE_UNROLL, body, 0)

    @pl.when(i == 0)
    def _():
        issue(dest_ref, 0)

    @pl.when(i + 1 < pl.num_programs(0))
    def _():
        issue(dest_next_ref, 1 - slot)

    pltpu.make_async_copy(eo_ref.at[pl.ds(0, n_rows)], rows_ref.at[slot], sem_ref.at[slot]).wait()
    wgt = wgt_ref[...]
    ffn = wgt[:, 0:1] * rows_ref[slot, 0:tm, :]
    for k in range(1, TOP_K):
        ffn = ffn + wgt[:, k:k + 1] * rows_ref[slot, k * tm:(k + 1) * tm, :]
    h = _layer_norm(alpha * h_ref[...] + ffn, g_ref[...], b_ref[...])
    gate = jax.nn.sigmoid(jnp.dot(h.astype(BF16), wgate_ref[...], preferred_element_type=F32))
    ple = jnp.dot(p_ref[...].astype(BF16), wple_ref[...], preferred_element_type=F32)
    o_ref[...] = h + gate * ple


def _dest_tiles(dest, tm):
    n_tiles = dest.shape[0] // tm
    return dest.reshape(n_tiles, tm, TOP_K).transpose(0, 2, 1).reshape(n_tiles, 1, TOP_K * tm)


def _final(h1, eo, dest_tiles, top_w, p2, ln_g, ln_b, ple_gate_w, ple_w, alpha, tm):
    n_tok, d = h1.shape
    n_tiles = n_tok // tm
    row_blk = lambda n: pl.BlockSpec((tm, n), lambda i: (i, 0))
    full = lambda m, n: pl.BlockSpec((m, n), lambda i: (0, 0))
    idx_blk = lambda fn: pl.BlockSpec((1, 1, TOP_K * tm), fn, memory_space=pltpu.SMEM)
    return pl.pallas_call(
        functools.partial(_final_kernel, alpha=alpha),
        out_shape=jax.ShapeDtypeStruct((n_tok, d), F32),
        grid=(n_tiles,),
        in_specs=[idx_blk(lambda i: (i, 0, 0)), idx_blk(lambda i: (jnp.minimum(i + 1, n_tiles - 1), 0, 0)),
                  row_blk(d), pl.BlockSpec(memory_space=pl.ANY), row_blk(LANE), row_blk(p2.shape[1]),
                  full(1, d), full(1, d), full(d, d), full(p2.shape[1], d)],
        out_specs=row_blk(d),
        scratch_shapes=[pltpu.VMEM((2, TOP_K * tm, d), F32), pltpu.SemaphoreType.DMA((2,))],
        compiler_params=pltpu.CompilerParams(dimension_semantics=("arbitrary",), vmem_limit_bytes=VMEM_LIMIT),
        name="combine_ln_ple",
    )(dest_tiles, dest_tiles, h1, eo, top_w, p2, ln_g.reshape(1, d), ln_b.reshape(1, d),
      ple_gate_w.astype(BF16), ple_w.astype(BF16))


def kernel(x, p, w_in, cmp_pe_k, cmp_w1_k, cmp_w2_k, cmp_pe_v, cmp_w1_v, cmp_w2_v, rwkv_mu, rwkv_w0, rwkv_w_up, rwkv_a0, rwkv_a_up, rwkv_g_up, rwkv_k_k, rwkv_k_a, rwkv_r_k, rwkv_ln_g, rwkv_ln_b, w_out, ln1_g, ln1_b, router_w, router_b, exp_w_gate, exp_b_gate, exp_w_up, exp_b_up, exp_w_down, exp_b_down, ln2_g, ln2_b, ple_w, ple_gate_w):
    b, t_len, d = x.shape
    depth = w_in.shape[0]
    alpha = float((2 * depth) ** 0.25)
    n_tok = b * t_len
    n_items = (n_tok * TOP_K) // MOE_ITEM_ROWS + N_EXPERTS
    h = x.reshape(n_tok, d)
    for i in range(depth):
        w_bf = _take_columns(w_in[i].astype(BF16), _z_source_columns())
        z3 = _in_proj(h, w_bf).reshape(b, t_len, Z_COLS)
        pe, w1, w2 = _nsa_prepare(cmp_pe_k[i], cmp_w1_k[i], cmp_w2_k[i], cmp_pe_v[i], cmp_w1_v[i], cmp_w2_v[i])
        kv_cmp = _compress(z3, pe, w1, w2)
        o_nsa = _nsa_attention(z3, kv_cmp)
        o_rwkv = _rwkv_time_mix(z3, rwkv_mu[i], rwkv_w0[i], rwkv_w_up[i], rwkv_a0[i], rwkv_a_up[i], rwkv_g_up[i],
                                rwkv_k_k[i], rwkv_k_a[i], rwkv_r_k[i].reshape(-1), rwkv_ln_g[i], rwkv_ln_b[i])
        h1, top_idx, top_w = _out_proj_router(o_nsa.reshape(n_tok, -1), o_rwkv.reshape(n_tok, -1), h, w_out[i],
                                              ln1_g[i], ln1_b[i], router_w[i], router_b[i], alpha)
        dest, item_e, item_nv = _moe_tables(top_idx[:, :TOP_K], n_items)
        dest_tiles = _dest_tiles(dest, MOE_TOKEN_TILE)
        xg = _dispatch(h1, dest_tiles, item_nv, n_items * MOE_ITEM_ROWS, MOE_TOKEN_TILE)
        eo = _moe_experts(xg, item_e, item_nv, exp_w_gate[i], exp_b_gate[i], exp_w_up[i], exp_b_up[i],
                          exp_w_down[i], exp_b_down[i])
        h = _final(h1, eo, dest_tiles, top_w, p[i].reshape(n_tok, -1), ln2_g[i], ln2_b[i], ple_gate_w[i],
                   ple_w[i], alpha, MOE_TOKEN_TILE)
    return h.reshape(b, t_len, d)
```

```python
import functools

import numpy as np
import jax
import jax.numpy as jnp
from jax import lax
from jax.experimental import pallas as pl
from jax.experimental.pallas import tpu as pltpu

F32 = jnp.float32
BF16 = jnp.bfloat16

LANE = 128
D_MODEL = 2048
HEAD_DIM = 64
NSA_HEADS = 16
NSA_KV_HEADS = 4
NSA_GROUP = NSA_HEADS // NSA_KV_HEADS
NSA_WIDTH = NSA_HEADS * HEAD_DIM
NSA_KV_WIDTH = NSA_KV_HEADS * HEAD_DIM
CMP_BLOCK = 32
CMP_STRIDE = 16
SEL_BLOCK = 64
N_SELECT = 16
WINDOW = 512
N_GATES = 3
Q_BLOCK = 64
RWKV_HEADS = 16
RWKV_WIDTH = RWKV_HEADS * HEAD_DIM
DECAY_LORA = 96
ICLR_LORA = 96
GATE_LORA = 256
GN_EPS = 64e-5
N_EXPERTS = 32
TOP_K = 4
SWIGLU_ALPHA = 1.702
SWIGLU_LIMIT = 7.0
PLE_DIM = 256
LN_EPS = 1e-5
NEG_INF = -1e30

NSA_COLS = NSA_WIDTH + 6 * NSA_KV_WIDTH + NSA_HEADS * N_GATES
RWKV_COLS = 3 * RWKV_WIDTH + DECAY_LORA + ICLR_LORA + GATE_LORA

RB_R = 0
RB_K = RB_R + RWKV_WIDTH // LANE
RB_V = RB_K + RWKV_WIDTH // LANE
RB_WLO = RB_V + RWKV_WIDTH // LANE
RB_ALO = RB_WLO + 1
RB_GLO = RB_ALO + 1
RWKV_BLOCKS = RB_GLO + GATE_LORA // LANE
ZB_RWKV = 0
ZB_Q = ZB_RWKV + RWKV_BLOCKS
ZB_KVC = ZB_Q + NSA_WIDTH // LANE
ZB_KVS = ZB_KVC + NSA_KV_HEADS
ZB_KVW = ZB_KVS + NSA_KV_HEADS
ZB_GATE = ZB_KVW + NSA_KV_HEADS
Z_BLOCKS = ZB_GATE + NSA_KV_HEADS
Z_COLS = Z_BLOCKS * LANE
assert (ZB_Q * LANE) % (NSA_GROUP * HEAD_DIM) == 0

VMEM_LIMIT = 56 * 1024 * 1024


def _z_source_columns():
    src = np.full((Z_COLS,), -1, np.int64)
    src[ZB_Q * LANE:ZB_Q * LANE + NSA_WIDTH] = np.arange(NSA_WIDTH)
    for branch in range(3):
        k0 = NSA_WIDTH + 2 * branch * NSA_KV_WIDTH
        v0 = k0 + NSA_KV_WIDTH
        for g in range(NSA_KV_HEADS):
            base = (ZB_KVC + branch * NSA_KV_HEADS + g) * LANE
            src[base:base + HEAD_DIM] = k0 + g * HEAD_DIM + np.arange(HEAD_DIM)
            src[base + HEAD_DIM:base + 2 * HEAD_DIM] = v0 + g * HEAD_DIM + np.arange(HEAD_DIM)
    g0 = NSA_WIDTH + 6 * NSA_KV_WIDTH
    for g in range(NSA_KV_HEADS):
        base = (ZB_GATE + g) * LANE
        for i in range(N_GATES):
            for r in range(NSA_GROUP):
                src[base + i * NSA_GROUP + r] = g0 + (g * NSA_GROUP + r) * N_GATES + i
    rwkv = _rwkv_source_columns()
    src[ZB_RWKV * LANE:(ZB_RWKV + RWKV_BLOCKS) * LANE] = np.where(rwkv >= 0, NSA_COLS + rwkv, -1)
    return src


def _rwkv_source_columns():
    src = np.full((RWKV_BLOCKS * LANE,), -1, np.int64)
    src[:3 * RWKV_WIDTH] = np.arange(3 * RWKV_WIDTH)
    src[RB_WLO * LANE:RB_WLO * LANE + DECAY_LORA] = 3 * RWKV_WIDTH + np.arange(DECAY_LORA)
    src[RB_ALO * LANE:RB_ALO * LANE + ICLR_LORA] = 3 * RWKV_WIDTH + DECAY_LORA + np.arange(ICLR_LORA)
    src[RB_GLO * LANE:] = 3 * RWKV_WIDTH + DECAY_LORA + ICLR_LORA + np.arange(GATE_LORA)
    return src


def _take_columns(w, src):
    pieces, i, n = [], 0, len(src)
    while i < n:
        j = i + 1
        if src[i] < 0:
            while j < n and src[j] < 0:
                j += 1
            pieces.append(jnp.zeros(w.shape[:-1] + (j - i,), w.dtype))
        else:
            while j < n and src[j] == src[i] + (j - i):
                j += 1
            pieces.append(w[..., int(src[i]):int(src[i]) + j - i])
        i = j
    return jnp.concatenate(pieces, axis=-1)


def _pad_rows(w, rows):
    return jnp.pad(w, ((0, rows - w.shape[0]), (0, 0)))


def _in_proj_kernel(x_ref, w_ref, z_ref, xb_ref):
    @pl.when(pl.program_id(1) == 0)
    def _():
        xb_ref[...] = x_ref[...].astype(BF16)

    z_ref[...] = jnp.dot(xb_ref[...], w_ref[...], preferred_element_type=F32).astype(z_ref.dtype)


def _in_proj(x2, w_bf, tm=1024, tn=512):
    n_tok, d = x2.shape
    n_cols = w_bf.shape[1]
    return pl.pallas_call(
        _in_proj_kernel,
        out_shape=jax.ShapeDtypeStruct((n_tok, n_cols), BF16),
        grid=(n_tok // tm, n_cols // tn),
        in_specs=[pl.BlockSpec((tm, d), lambda i, j: (i, 0)),
                  pl.BlockSpec((d, tn), lambda i, j: (0, j))],
        out_specs=pl.BlockSpec((tm, tn), lambda i, j: (i, j)),
        scratch_shapes=[pltpu.VMEM((tm, d), BF16)],
        compiler_params=pltpu.CompilerParams(dimension_semantics=("parallel", "arbitrary"),
                                             vmem_limit_bytes=VMEM_LIMIT),
        name="in_proj",
    )(x2, w_bf)


def _compress_kernel(kv_ref, pe_ref, w1_ref, w2_ref, out_ref, kv32_ref):
    n_chunks = kv_ref.shape[1] // CMP_STRIDE
    kv32_ref[...] = kv_ref[0].astype(F32)
    acc_lo = jnp.zeros((n_chunks, LANE), F32)
    acc_hi = jnp.zeros((n_chunks, LANE), F32)
    for i in range(CMP_STRIDE):
        rows = kv32_ref[pl.ds(i, n_chunks, stride=CMP_STRIDE), :]
        lo = (rows + pe_ref[i:i + 1, :]).astype(BF16)
        hi = (rows + pe_ref[CMP_STRIDE + i:CMP_STRIDE + i + 1, :]).astype(BF16)
        acc_lo += jnp.dot(lo, w1_ref[i], preferred_element_type=F32)
        acc_hi += jnp.dot(hi, w1_ref[CMP_STRIDE + i], preferred_element_type=F32)
    shifted = jnp.concatenate([acc_hi[1:], jnp.zeros((1, LANE), F32)], axis=0)
    hid = jax.nn.gelu(acc_lo + shifted)
    out = jnp.dot(hid.astype(BF16), w2_ref[...], preferred_element_type=F32)
    row = lax.broadcasted_iota(jnp.int32, out.shape, 0)
    out_ref[0, 0] = jnp.where(row < n_chunks - 1, out, 0.0).astype(out_ref.dtype)


def _compress(z3, pe, w1, w2):
    b, t_len, _ = z3.shape
    n_chunks = t_len // CMP_STRIDE
    return pl.pallas_call(
        _compress_kernel,
        out_shape=jax.ShapeDtypeStruct((b, NSA_KV_HEADS, n_chunks, LANE), BF16),
        grid=(b, NSA_KV_HEADS),
        in_specs=[pl.BlockSpec((1, t_len, LANE), lambda bi, g: (bi, 0, ZB_KVC + g)),
                  pl.BlockSpec((CMP_BLOCK, LANE), lambda bi, g: (0, 0)),
                  pl.BlockSpec((CMP_BLOCK, LANE, LANE), lambda bi, g: (0, 0, 0)),
                  pl.BlockSpec((LANE, LANE), lambda bi, g: (0, 0))],
        out_specs=pl.BlockSpec((1, 1, n_chunks, LANE), lambda bi, g: (bi, g, 0, 0)),
        scratch_shapes=[pltpu.VMEM((t_len, LANE), F32)],
        compiler_params=pltpu.CompilerParams(dimension_semantics=("parallel", "parallel"),
                                             vmem_limit_bytes=VMEM_LIMIT),
        name="kv_compress",
    )(z3, pe, w1, w2)


def _block_diag2(a, b):
    za = jnp.zeros(a.shape[:-1] + (b.shape[-1],), a.dtype)
    zb = jnp.zeros(b.shape[:-1] + (a.shape[-1],), b.dtype)
    return jnp.concatenate([jnp.concatenate([a, za], axis=-1), jnp.concatenate([zb, b], axis=-1)], axis=-2)


def _stack_heads(x):
    return jnp.concatenate([x[:, r * HEAD_DIM:(r + 1) * HEAD_DIM] for r in range(NSA_GROUP)], axis=0)


def _dot_nt(a, b):
    return lax.dot_general(a, b, (((1,), (1,)), ((), ())), preferred_element_type=F32)


def _eye(n, dtype):
    return jnp.where(lax.broadcasted_iota(jnp.int32, (n, n), 0) == lax.broadcasted_iota(jnp.int32, (n, n), 1),
                     1.0, 0.0).astype(dtype)


NSA_KEY_CHUNK = 256
NSA_V_ROWS = HEAD_DIM + 16


def _nsa_kernel(slopes_ref, *refs, n_sel, n_g):
    q_refs = refs[:n_g]
    (kvc_ref, kvs_ref, kvw_ref, gate_ref, ovl_ref, o_ref, vst_ref, vwt_ref, vct_ref, sel_ref,
     sc0_ref, sc1_ref, p0_ref, p1_ref) = refs[n_g:]
    g_base = pl.program_id(1) * n_g
    qi = pl.program_id(2)
    q0 = qi * Q_BLOCK
    cols = NSA_GROUP * Q_BLOCK
    kc_len = NSA_KEY_CHUNK
    blocks_per_chunk = kc_len // SEL_BLOCK
    n_chunks = kvs_ref.shape[1] // kc_len
    n_cmp_pad = kvc_ref.shape[2]
    n_slc = ovl_ref.shape[0]
    v_rows = vst_ref.shape[2]
    eye_dh = _eye(HEAD_DIM, BF16)
    k_lanes = lambda gg: slice(gg * LANE, gg * LANE + HEAD_DIM)
    v_lanes = lambda gg: slice(gg * LANE + HEAD_DIM, (gg + 1) * LANE)

    @pl.when(qi == 0)
    def _():
        ones_row = jnp.where(lax.broadcasted_iota(jnp.int32, (v_rows - HEAD_DIM, kc_len), 0) == 0,
                             1.0, 0.0).astype(BF16)

        def body(c, carry):
            rows = pl.ds(pl.multiple_of(c * kc_len, kc_len), kc_len)
            for gg in range(n_g):
                vst_ref[gg, c, 0:HEAD_DIM] = _dot_nt(eye_dh, kvs_ref[0, rows, v_lanes(gg)]).astype(BF16)
                vwt_ref[gg, c, 0:HEAD_DIM] = _dot_nt(eye_dh, kvw_ref[0, rows, v_lanes(gg)]).astype(BF16)
                vst_ref[gg, c, HEAD_DIM:v_rows] = ones_row
                vwt_ref[gg, c, HEAD_DIM:v_rows] = ones_row
            return carry

        lax.fori_loop(0, n_chunks, body, 0)
        for gg in range(n_g):
            vct_ref[gg] = _dot_nt(eye_dh, kvc_ref[0, gg, :, HEAD_DIM:2 * HEAD_DIM]).astype(BF16)

    log2e = float(np.log2(np.e))
    lane = lax.broadcasted_iota(jnp.int32, (1, cols), 1)
    head = lane // Q_BLOCK
    tq = q0 + lane % Q_BLOCK
    sub = lax.broadcasted_iota(jnp.int32, (kc_len, cols), 0)
    half = lax.broadcasted_iota(jnp.int32, (1, LANE), 1) // Q_BLOCK
    blk = lax.broadcasted_iota(jnp.int32, (n_slc, LANE), 0)
    blk8 = lax.broadcasted_iota(jnp.int32, (8, LANE), 0)
    cmp_end = lax.broadcasted_iota(jnp.int32, (n_cmp_pad, 1), 0) * CMP_STRIDE + (CMP_BLOCK - 1)
    d_cmp = tq - cmp_end
    m_cmp = d_cmp >= 0
    d_cmp_f = d_cmp.astype(F32)
    forced = (blk == 0) | (blk == qi) | (blk == qi - 1)
    ovl = ovl_ref[...]

    groups = range(n_g)
    qs, slopes = [], []
    for gg in groups:
        q = _stack_heads(q_refs[gg][0]).astype(F32)
        qs.append((q * (HEAD_DIM ** -0.5 * log2e)).astype(BF16))
        slope = jnp.zeros((1, cols), F32)
        for r in range(NSA_GROUP):
            slope = jnp.where(head == r, slopes_ref[(g_base + gg) * NSA_GROUP + r] * log2e, slope)
        slopes.append(slope)
    st = [_dot_nt(kvc_ref[0, gg, :, 0:HEAD_DIM], qs[gg]) for gg in groups]
    p_cmp = []
    for gg in groups:
        s_m = jnp.where(m_cmp, st[gg] - slopes[gg] * d_cmp_f, NEG_INF)
        e = jnp.where(m_cmp, jnp.exp2(s_m - jnp.max(s_m, axis=0, keepdims=True)), 0.0)
        p_cmp.append(e / jnp.maximum(jnp.sum(e, axis=0, keepdims=True), 1e-30))
    o_cmp = [jnp.dot(vct_ref[gg], p_cmp[gg].astype(BF16), preferred_element_type=F32) for gg in groups]
    parts = []
    for gg in groups:
        y = p_cmp[gg][:, 0:LANE] + p_cmp[gg][:, LANE:2 * LANE]
        parts.append(_split3(y + pltpu.roll(y, Q_BLOCK, 1)))
    pooled = [[jnp.dot(ovl, parts[gg][i], preferred_element_type=F32) for gg in groups] for i in range(3)]
    for gg in groups:
        imp = pooled[0][gg] + pooled[1][gg] + pooled[2][gg]
        imp = jnp.where(forced, jnp.inf, jnp.where(blk > qi, -jnp.inf, imp))
        tiles = [imp[t:t + 8] for t in range(0, n_slc, 8)]
        ranks = [jnp.zeros((8, LANE), F32) for _ in tiles]
        for jp in range(0, n_slc, 2):
            row = jnp.where(half == 0, imp[jp:jp + 1, :], imp[jp + 1:jp + 2, :])
            for ti, tile in enumerate(tiles):
                t0 = ti * 8
                if t0 > jp + 1:
                    hit = jnp.where(row >= tile, 1.0, 0.0)
                elif t0 + 7 <= jp:
                    hit = jnp.where(row > tile, 1.0, 0.0)
                else:
                    hit = jnp.where(blk8 + t0 > jp + half, jnp.where(row >= tile, 1.0, 0.0),
                                    jnp.where(row > tile, 1.0, 0.0))
                ranks[ti] = ranks[ti] + hit
        rank = jnp.concatenate(ranks, axis=0)
        rank = rank + pltpu.roll(rank, Q_BLOCK, 1)
        neg = jnp.where((rank < n_sel) & (blk <= qi), 0.0, NEG_INF)
        neg2 = jnp.concatenate([neg, neg], axis=1)
        for j in range(n_slc):
            sel_ref[gg, j] = jnp.broadcast_to(neg2[j:j + 1, :], (8, cols))
    sub_f = sub.astype(F32)
    fronts = [(qs[gg], slopes[gg], slopes[gg] * sub_f, o_cmp[gg]) for gg in groups]

    init1 = (jnp.full((1, cols), NEG_INF, F32), jnp.zeros((v_rows, cols), F32))

    def block_mask(gg, c):
        tiles = [sel_ref[gg, c * blocks_per_chunk + i] for i in range(blocks_per_chunk)]
        return jnp.concatenate([t for t in tiles for _ in range(SEL_BLOCK // 8)], axis=0)

    streams = {"slc": (kvs_ref, vst_ref), "win": (kvw_ref, vwt_ref)}
    sc_slots = (sc0_ref, sc1_ref)
    p_slots = (p0_ref, p1_ref)

    def score(gg, pos, slot):
        stream, c, _ = pos
        rows = pl.ds(pl.multiple_of(c * kc_len, kc_len), kc_len)
        sc_slots[slot][gg] = _dot_nt(streams[stream][0][0, rows, k_lanes(gg)], fronts[gg][0])

    def softmax(gg, pos, slot, m_prev):
        _, c, mask_fn = pos
        _, slope, bias_local, _ = fronts[gg]
        sc = sc_slots[slot][gg] + bias_local + mask_fn(gg)
        shift = slope * (c * kc_len - q0).astype(F32)
        m_new = jnp.maximum(m_prev, jnp.max(sc, axis=0, keepdims=True) + shift)
        p_slots[slot][gg] = jnp.exp2(sc - (m_new - shift)).astype(BF16)
        return m_new, jnp.exp2(m_prev - m_new)

    def weigh(gg, pos, slot, alpha, acc):
        stream, c, _ = pos
        return alpha * acc + jnp.dot(streams[stream][1][gg, c], p_slots[slot][gg], preferred_element_type=F32)

    def step(pos, slot, nxt, prev, state, alphas):
        if nxt is not None:
            for gg in range(n_g):
                score(gg, nxt, 1 - slot)
        state = {s: list(v) for s, v in state.items()}
        new_alphas = []
        for gg in range(n_g):
            m_prev, acc = state[pos[0]][gg]
            m_new, alpha = softmax(gg, pos, slot, m_prev)
            state[pos[0]][gg] = (m_new, acc)
            new_alphas.append(alpha)
            if prev is not None:
                m_p, acc_p = state[prev[0]][gg]
                state[prev[0]][gg] = (m_p, weigh(gg, prev, 1 - slot, alphas[gg], acc_p))
        return state, new_alphas

    c_cur = qi // blocks_per_chunk
    n_pairs = c_cur // 2
    plain = lambda c: ("slc", c, lambda gg: block_mask(gg, c))

    for gg in range(n_g):
        p1_ref[gg] = jnp.zeros((kc_len, cols), BF16)
        score(gg, plain(0), 0)

    def pair_body(i, carry):
        slc_state, alphas = carry
        k0 = 2 * i
        state, alphas = step(plain(k0), 0, plain(k0 + 1), plain(jnp.maximum(k0 - 1, 0)),
                             {"slc": slc_state}, alphas)
        state, alphas = step(plain(k0 + 1), 1, plain(k0 + 2), plain(k0), state, alphas)
        return tuple(state["slc"]), tuple(alphas)

    slc_state, alphas = lax.fori_loop(0, n_pairs, pair_body,
                                      (tuple(init1 for _ in range(n_g)),
                                       tuple(jnp.ones((1, cols), F32) for _ in range(n_g))))
    k_t = 2 * n_pairs
    odd_neg = jnp.where(c_cur % 2 == 1, 0.0, NEG_INF)
    causal_neg = jnp.where((c_cur * kc_len + sub) <= tq, 0.0, NEG_INF)
    tail = [("slc", k_t, lambda gg: block_mask(gg, k_t) + odd_neg),
            ("slc", c_cur, lambda gg: block_mask(gg, c_cur) + causal_neg)]
    for i in range(WINDOW // kc_len, -1, -1):
        c_raw = c_cur - i
        c_win = jnp.maximum(c_raw, 0)
        dist = tq - (c_win * kc_len + sub)
        band_neg = jnp.where((dist >= 0) & (dist < WINDOW) & (c_raw >= 0), 0.0, NEG_INF)
        tail.append(("win", c_win, lambda gg, band_neg=band_neg: band_neg))
    state = {"slc": list(slc_state), "win": [init1 for _ in range(n_g)]}
    prev = plain(jnp.maximum(k_t - 1, 0))
    for idx, pos in enumerate(tail):
        nxt = tail[idx + 1] if idx + 1 < len(tail) else None
        state, alphas = step(pos, idx % 2, nxt, prev, state, alphas)
        prev = pos
    last_slot = (len(tail) - 1) % 2
    for gg in range(n_g):
        m_p, acc_p = state[prev[0]][gg]
        state[prev[0]][gg] = (m_p, weigh(gg, prev, last_slot, alphas[gg], acc_p))

    eye_q = _eye(Q_BLOCK, BF16)
    eye_lane = _eye(LANE, BF16)
    gates = [jax.nn.sigmoid(_dot_nt(eye_lane, gate_ref[0, :, gg * LANE:(gg + 1) * LANE])) for gg in groups]

    def gate_row(gg, i):
        return jnp.concatenate([gates[gg][i * NSA_GROUP + r:i * NSA_GROUP + r + 1, :] for r in range(NSA_GROUP)],
                               axis=1)

    def normalized(acc):
        return acc[0:HEAD_DIM] / jnp.maximum(acc[HEAD_DIM:HEAD_DIM + 1], 1e-30)

    o_rows = []
    for gg in groups:
        o = (gate_row(gg, 0) * fronts[gg][3] + gate_row(gg, 1) * normalized(state["slc"][gg][1])
             + gate_row(gg, 2) * normalized(state["win"][gg][1])).astype(BF16)
        o_rows.append(jnp.concatenate([o[:, r * Q_BLOCK:(r + 1) * Q_BLOCK] for r in range(NSA_GROUP)], axis=0))
    outs = [_dot_nt(eye_q, o_rows[gg]) for gg in groups]
    for gg in groups:
        o_ref[0, :, gg * cols:(gg + 1) * cols] = outs[gg].astype(o_ref.dtype)


NSA_GROUPS_PER_STEP = 4


def _nsa_attention(z3, kv_cmp):
    b, t_len, _ = z3.shape
    n_q = t_len // Q_BLOCK
    n_slc = t_len // SEL_BLOCK
    n_cmp_pad = kv_cmp.shape[2]
    n_sel = min(N_SELECT, n_slc)
    n_g = NSA_GROUPS_PER_STEP
    cols = NSA_GROUP * Q_BLOCK
    slopes = jnp.exp2(-8.0 * jnp.arange(1, NSA_HEADS + 1, dtype=F32) / NSA_HEADS)
    c_idx = np.arange(n_cmp_pad)
    cmp_start = c_idx * CMP_STRIDE
    cmp_end = cmp_start + CMP_BLOCK - 1
    slc_start = np.arange(n_slc) * SEL_BLOCK
    overlap = np.clip(np.minimum(cmp_end[None, :], slc_start[:, None] + SEL_BLOCK - 1)
                      - np.maximum(cmp_start[None, :], slc_start[:, None]) + 1, 0, None).astype(np.float32)
    overlap[:, c_idx >= t_len // CMP_STRIDE - CMP_BLOCK // CMP_STRIDE + 1] = 0.0
    kernel = functools.partial(_nsa_kernel, n_sel=n_sel, n_g=n_g)
    n_kc = t_len // NSA_KEY_CHUNK
    q_spec = lambda gg: pl.BlockSpec((1, Q_BLOCK, cols),
                                     lambda bi, g, qi: (bi, qi, ZB_Q * LANE // cols + g * n_g + gg))
    slab = lambda zb: pl.BlockSpec((1, t_len, n_g * LANE), lambda bi, g, qi: (bi, 0, zb // n_g + g))
    assert ZB_KVS % n_g == 0 and ZB_KVW % n_g == 0 and ZB_GATE % n_g == 0
    return pl.pallas_call(
        kernel,
        out_shape=jax.ShapeDtypeStruct((b, t_len, NSA_WIDTH), BF16),
        grid=(b, NSA_KV_HEADS // n_g, n_q),
        in_specs=[pl.BlockSpec(memory_space=pltpu.SMEM)] + [q_spec(gg) for gg in range(n_g)] + [
            pl.BlockSpec((1, n_g, n_cmp_pad, LANE), lambda bi, g, qi: (bi, g, 0, 0)),
            slab(ZB_KVS), slab(ZB_KVW),
            pl.BlockSpec((1, Q_BLOCK, n_g * LANE), lambda bi, g, qi: (bi, qi, ZB_GATE // n_g + g)),
            pl.BlockSpec((n_slc, n_cmp_pad), lambda bi, g, qi: (0, 0)),
        ],
        out_specs=pl.BlockSpec((1, Q_BLOCK, n_g * cols), lambda bi, g, qi: (bi, qi, g)),
        scratch_shapes=[pltpu.VMEM((n_g, n_kc, NSA_V_ROWS, NSA_KEY_CHUNK), BF16),
                        pltpu.VMEM((n_g, n_kc, NSA_V_ROWS, NSA_KEY_CHUNK), BF16),
                        pltpu.VMEM((n_g, HEAD_DIM, n_cmp_pad), BF16),
                        pltpu.VMEM((n_g, n_slc, 8, cols), F32),
                        pltpu.VMEM((n_g, NSA_KEY_CHUNK, cols), F32),
                        pltpu.VMEM((n_g, NSA_KEY_CHUNK, cols), F32),
                        pltpu.VMEM((n_g, NSA_KEY_CHUNK, cols), BF16),
                        pltpu.VMEM((n_g, NSA_KEY_CHUNK, cols), BF16)],
        compiler_params=pltpu.CompilerParams(dimension_semantics=("parallel", "parallel", "arbitrary"),
                                             vmem_limit_bytes=VMEM_LIMIT),
        name="nsa_attention",
    )(slopes, *([z3] * n_g), kv_cmp, z3, z3, z3, jnp.asarray(overlap, BF16))


def _nsa_prepare(cmp_pe_k, cmp_w1_k, cmp_w2_k, cmp_pe_v, cmp_w1_v, cmp_w2_v):
    pe = jnp.concatenate([cmp_pe_k, cmp_pe_v], axis=-1)
    w1 = _block_diag2(cmp_w1_k.reshape(CMP_BLOCK, HEAD_DIM, HEAD_DIM),
                      cmp_w1_v.reshape(CMP_BLOCK, HEAD_DIM, HEAD_DIM)).astype(BF16)
    w2 = _block_diag2(cmp_w2_k, cmp_w2_v).astype(BF16)
    return pe, w1, w2


RWKV_CHUNK = 64


def _rwkv_prep_kernel(z_ref, mu_ref, w0_ref, wup_ref, a0_ref, aup_ref, gup_ref, kk_ref, ka_ref,
                      r_ref, k_ref, v_ref, kkr_ref, a_ref, lw_ref, g_ref, carry_ref):
    w = RWKV_WIDTH

    @pl.when(pl.program_id(1) == 0)
    def _():
        carry_ref[...] = jnp.zeros_like(carry_ref)

    z = z_ref[0].astype(F32)
    tc = z.shape[0]
    row = lax.broadcasted_iota(jnp.int32, z.shape, 0)
    prev = jnp.where(row == 0, carry_ref[0:1, :], pltpu.roll(z, 1, 0))
    carry_ref[0:1, :] = z[tc - 1:tc, :]
    zs = z + (prev - z) * mu_ref[...]
    r = zs[:, RB_R * LANE:RB_R * LANE + w]
    k = zs[:, RB_K * LANE:RB_K * LANE + w]
    v = zs[:, RB_V * LANE:RB_V * LANE + w]
    w_lo = zs[:, RB_WLO * LANE:(RB_WLO + 1) * LANE]
    a_lo = zs[:, RB_ALO * LANE:(RB_ALO + 1) * LANE]
    g_lo = zs[:, RB_GLO * LANE:RB_GLO * LANE + GATE_LORA]
    d = w0_ref[...] + jnp.dot(jnp.tanh(w_lo).astype(BF16), wup_ref[...], preferred_element_type=F32)
    w_raw = -jax.nn.softplus(-d) - 0.5
    lw_ref[0] = -jnp.exp(w_raw)
    a = jax.nn.sigmoid(a0_ref[...] + jnp.dot(a_lo.astype(BF16), aup_ref[...], preferred_element_type=F32))
    g = jnp.dot(jax.nn.sigmoid(g_lo).astype(BF16), gup_ref[...], preferred_element_type=F32)
    r_ref[0] = r.astype(r_ref.dtype)
    v_ref[0] = v.astype(v_ref.dtype)
    kkr_ref[0] = (k * kk_ref[...]).astype(kkr_ref.dtype)
    k_ref[0] = (k * (1.0 + (a - 1.0) * ka_ref[...])).astype(k_ref.dtype)
    a_ref[0] = a.astype(a_ref.dtype)
    g_ref[0] = g.astype(g_ref.dtype)


def _rwkv_prep(z3, mu, w0, w_up, a0, a_up, g_up, k_k, k_a, tc=256):
    b, t_len, _ = z3.shape
    w = RWKV_WIDTH
    ncol = RWKV_BLOCKS * LANE
    vec = lambda n: pl.BlockSpec((1, n), lambda bi, ti: (0, 0))
    mat = lambda m, n: pl.BlockSpec((m, n), lambda bi, ti: (0, 0))
    out_bf = jax.ShapeDtypeStruct((b, t_len, w), BF16)
    out_f32 = jax.ShapeDtypeStruct((b, t_len, w), F32)
    out_spec = pl.BlockSpec((1, tc, w), lambda bi, ti: (bi, ti, 0))
    return pl.pallas_call(
        _rwkv_prep_kernel,
        out_shape=(out_bf, out_bf, out_bf, out_bf, out_bf, out_f32, out_bf),
        grid=(b, t_len // tc),
        in_specs=[pl.BlockSpec((1, tc, ncol), lambda bi, ti: (bi, ti, ZB_RWKV)),
                  vec(ncol), vec(w), mat(LANE, w), vec(w), mat(LANE, w), mat(GATE_LORA, w), vec(w), vec(w)],
        out_specs=(out_spec,) * 7,
        scratch_shapes=[pltpu.VMEM((8, ncol), F32)],
        compiler_params=pltpu.CompilerParams(dimension_semantics=("parallel", "arbitrary"),
                                             vmem_limit_bytes=VMEM_LIMIT),
        name="rwkv_prep",
    )(z3, mu, w0, w_up, a0, a_up, g_up, k_k, k_a)


def _pair_blocks(x):
    lane = lax.broadcasted_iota(jnp.int32, x.shape, 1)
    zero = jnp.zeros((), x.dtype)
    return jnp.concatenate([jnp.where(lane < HEAD_DIM, x, zero), jnp.where(lane >= HEAD_DIM, x, zero)], axis=0)


def _fold_pair(x):
    n = x.shape[0] // 2
    return x[:n] + x[n:]


def _split3(x):
    hi = x.astype(BF16)
    r1 = x - hi.astype(F32)
    mid = r1.astype(BF16)
    return hi, mid, (r1 - mid.astype(F32)).astype(BF16)


def _dot_split_rhs(a_bf, x):
    hi, mid, lo = _split3(x)
    return (jnp.dot(a_bf, hi, preferred_element_type=F32) + jnp.dot(a_bf, mid, preferred_element_type=F32)
            + jnp.dot(a_bf, lo, preferred_element_type=F32))


RWKV_PAIRS_PER_STEP = 4


def _rwkv_scan_kernel(r_ref, k_ref, v_ref, kkr_ref, a_ref, lw_ref, g_ref, rk_ref, lng_ref, lnb_ref,
                      o_ref, s_ref, *, n_chunks, n_pairs):
    L = RWKV_CHUNK
    L2 = 2 * L

    @pl.when(pl.program_id(2) == 0)
    def _():
        s_ref[...] = jnp.zeros_like(s_ref)

    ri = lax.broadcasted_iota(jnp.int32, (L, L), 0)
    ci = lax.broadcasted_iota(jnp.int32, (L, L), 1)
    tri_incl = jnp.where(ri >= ci, 1.0, 0.0).astype(BF16)
    r2 = lax.broadcasted_iota(jnp.int32, (L2, L2), 0)
    c2 = lax.broadcasted_iota(jnp.int32, (L2, L2), 1)
    same_head = (r2 // L) == (c2 // L)
    strict2 = same_head & (r2 > c2)
    incl2 = same_head & (r2 >= c2)
    eye2 = jnp.where(r2 == c2, 1.0, 0.0).astype(F32)
    h_r = lax.broadcasted_iota(jnp.int32, (LANE, LANE), 0) // HEAD_DIM
    h_c = lax.broadcasted_iota(jnp.int32, (LANE, LANE), 1) // HEAD_DIM
    head_mask = h_r == h_c
    head_ones = jnp.where(head_mask, 1.0, 0.0).astype(BF16)

    def head_sum(x):
        hi, mid, _ = _split3(x)
        return (jnp.dot(hi, head_ones, preferred_element_type=F32)
                + jnp.dot(mid, head_ones, preferred_element_type=F32))

    def mm(a, b):
        return jnp.dot(a.astype(BF16), b.astype(BF16), preferred_element_type=F32)

    chains = [(hp, c) for hp in range(n_pairs) for c in range(n_chunks)]
    lanes_of = lambda hp: slice(hp * LANE, (hp + 1) * LANE)
    rows_of = lambda c: pl.ds(c * L, L)
    each = lambda fn: {ch: fn(ch) for ch in chains}

    load = lambda ref: each(lambda ch: ref[0, rows_of(ch[1]), lanes_of(ch[0])])
    r, k, v, a = (each(lambda ch, d=d: d[ch].astype(F32)) for d in (load(r_ref), load(k_ref), load(v_ref), load(a_ref)))
    kkr = each(lambda ch, d=load(kkr_ref): d[ch].astype(F32))
    lw = load(lw_ref)
    kk_sq = each(lambda ch: head_sum(kkr[ch] * kkr[ch]))
    cl = each(lambda ch: _dot_split_rhs(tri_incl, lw[ch]))
    kk = each(lambda ch: kkr[ch] / jnp.maximum(jnp.sqrt(kk_sq[ch]), 1e-12))
    p_incl = each(lambda ch: jnp.exp(cl[ch]))
    p_inv = each(lambda ch: jnp.exp(-cl[ch]))
    rt = each(lambda ch: (r[ch] * p_incl[ch]).astype(BF16))
    at = each(lambda ch: (-kk[ch] * jnp.exp(cl[ch] - lw[ch])).astype(BF16))
    kt = each(lambda ch: k[ch] * p_inv[ch])
    bt = each(lambda ch: kk[ch] * a[ch] * p_inv[ch])
    gram = each(lambda ch: _dot_nt(
        jnp.concatenate([_pair_blocks(at[ch]), _pair_blocks(rt[ch])], axis=0),
        jnp.concatenate([_pair_blocks(bt[ch]), _pair_blocks(kt[ch])], axis=0).astype(BF16)))
    a_ab = each(lambda ch: jnp.where(strict2, gram[ch][:L2, :L2], 0.0))
    a_ak = each(lambda ch: jnp.where(strict2, gram[ch][:L2, L2:], 0.0).astype(BF16))
    a_rb = each(lambda ch: jnp.where(incl2, gram[ch][L2:, :L2], 0.0).astype(BF16))
    a_rk = each(lambda ch: jnp.where(incl2, gram[ch][L2:, L2:], 0.0).astype(BF16))
    tinv = each(lambda ch: eye2 + a_ab[ch])
    apow = a_ab
    for _ in range(int(np.log2(L)) - 1):
        apow = each(lambda ch: mm(apow[ch], apow[ch]))
        tinv = each(lambda ch: tinv[ch] + mm(tinv[ch], apow[ch]))
    tinv = each(lambda ch: tinv[ch].astype(BF16))
    v_bf = each(lambda ch: v[ch].astype(BF16))
    v2 = each(lambda ch: _pair_blocks(v_bf[ch]))
    akv = each(lambda ch: _fold_pair(jnp.dot(a_ak[ch], v2[ch], preferred_element_type=F32)))
    rkv = each(lambda ch: _fold_pair(jnp.dot(a_rk[ch], v2[ch], preferred_element_type=F32)))
    p_last = each(lambda ch: p_incl[ch][L - 1:L, :])
    wts = each(lambda ch: jnp.concatenate([bt[ch] * p_last[ch], kt[ch] * p_last[ch]], axis=0).astype(BF16))
    bonus = each(lambda ch: head_sum(r[ch] * k[ch] * rk_ref[:, lanes_of(ch[0])]) * v[ch])

    pairs = range(n_pairs)
    states = [s_ref[hp] for hp in pairs]
    for c in range(n_chunks):
        s_bf = [states[hp].astype(BF16) for hp in pairs]
        m = [_dot_nt(at[hp, c], s_bf[hp]) + akv[hp, c] for hp in pairs]
        y = [_dot_nt(rt[hp, c], s_bf[hp]) + rkv[hp, c] for hp in pairs]
        u = [_fold_pair(jnp.dot(tinv[hp, c], _pair_blocks(m[hp].astype(BF16)), preferred_element_type=F32))
             for hp in pairs]
        u_bf = [u[hp].astype(BF16) for hp in pairs]
        upd = [lax.dot_general(jnp.concatenate([u_bf[hp], v_bf[hp, c]], axis=0), wts[hp, c],
                               (((0,), (0,)), ((), ())), preferred_element_type=F32) for hp in pairs]
        states = [states[hp] * p_last[hp, c] + jnp.where(head_mask, upd[hp], 0.0) for hp in pairs]
        y = [y[hp] + _fold_pair(jnp.dot(a_rb[hp, c], _pair_blocks(u_bf[hp]), preferred_element_type=F32))
             for hp in pairs]
        mean = [head_sum(y[hp]) * (1.0 / HEAD_DIM) for hp in pairs]
        yc = [y[hp] - mean[hp] for hp in pairs]
        var = [head_sum(yc[hp] * yc[hp]) * (1.0 / HEAD_DIM) for hp in pairs]
        for hp in pairs:
            yn = yc[hp] * lax.rsqrt(var[hp] + GN_EPS) * lng_ref[:, lanes_of(hp)] + lnb_ref[:, lanes_of(hp)]
            o_ref[0, rows_of(c), lanes_of(hp)] = (
                (yn + bonus[hp, c]) * g_ref[0, rows_of(c), lanes_of(hp)].astype(F32)).astype(o_ref.dtype)
    for hp in pairs:
        s_ref[hp] = states[hp]


def _rwkv_scan(r, k, v, kkr, a, lw, g, r_k, ln_g, ln_b, tt=256):
    b, t_len, w = r.shape
    n_p = RWKV_PAIRS_PER_STEP
    width = n_p * LANE
    tile = pl.BlockSpec((1, tt, width), lambda bi, hp, ti: (bi, ti, hp))
    vec = pl.BlockSpec((1, width), lambda bi, hp, ti: (0, hp))
    kernel = functools.partial(_rwkv_scan_kernel, n_chunks=tt // RWKV_CHUNK, n_pairs=n_p)
    return pl.pallas_call(
        kernel,
        out_shape=jax.ShapeDtypeStruct((b, t_len, w), BF16),
        grid=(b, w // width, t_len // tt),
        in_specs=[tile] * 7 + [vec] * 3,
        out_specs=tile,
        scratch_shapes=[pltpu.VMEM((n_p, LANE, LANE), F32)],
        compiler_params=pltpu.CompilerParams(dimension_semantics=("parallel", "parallel", "arbitrary"),
                                             vmem_limit_bytes=VMEM_LIMIT),
        name="rwkv_scan",
    )(r, k, v, kkr, a, lw, g, r_k, ln_g, ln_b)


def _rwkv_time_mix(z3, mu, w0, w_up, a0, a_up, g_up, k_k, k_a, r_k, ln_g, ln_b):
    row = lambda u: u.reshape(1, -1)
    mu_p = row(_take_columns(mu, _rwkv_source_columns()))
    r, k, v, kkr, a, lw, g = _rwkv_prep(z3, mu_p, row(w0), _pad_rows(w_up, LANE).astype(BF16), row(a0),
                                        _pad_rows(a_up, LANE).astype(BF16), g_up.astype(BF16), row(k_k), row(k_a))
    return _rwkv_scan(r, k, v, kkr, a, lw, g, row(r_k), row(ln_g), row(ln_b))


def _layer_norm(x, g, b):
    mean = jnp.mean(x, axis=-1, keepdims=True)
    xc = x - mean
    var = jnp.mean(xc * xc, axis=-1, keepdims=True)
    return xc * lax.rsqrt(var + LN_EPS) * g + b


def _out_proj_kernel(on_ref, or_ref, x_ref, w_ref, g_ref, b_ref, rwh_ref, rwm_ref, rb_ref,
                     h_ref, idx_ref, wgt_ref, *, alpha):
    o = jnp.concatenate([on_ref[...], or_ref[...]], axis=1)
    mix = jnp.dot(o, w_ref[...], preferred_element_type=F32)
    h = _layer_norm(alpha * x_ref[...] + mix, g_ref[...], b_ref[...])
    h_ref[...] = h
    h_hi, h_mid, _ = _split3(h)
    logits = (jnp.dot(h_hi, rwh_ref[...], preferred_element_type=F32)
              + jnp.dot(h_hi, rwm_ref[...], preferred_element_type=F32)
              + jnp.dot(h_mid, rwh_ref[...], preferred_element_type=F32)) + rb_ref[...]
    lane = lax.broadcasted_iota(jnp.int32, logits.shape, 1)
    logits = jnp.where(lane < N_EXPERTS, logits, -jnp.inf)
    idx_out = jnp.zeros(logits.shape, jnp.int32)
    val_out = jnp.full(logits.shape, -jnp.inf, F32)
    for k in range(TOP_K):
        best = jnp.max(logits, axis=-1, keepdims=True)
        first = jnp.min(jnp.where(logits == best, lane, LANE), axis=-1, keepdims=True)
        idx_out = jnp.where(lane == k, first, idx_out)
        val_out = jnp.where(lane == k, best, val_out)
        logits = jnp.where(lane == first, -jnp.inf, logits)
    e = jnp.exp(val_out - jnp.max(val_out, axis=-1, keepdims=True))
    idx_ref[...] = idx_out
    wgt_ref[...] = e / jnp.sum(e, axis=-1, keepdims=True)


def _out_proj_router(o_nsa, o_rwkv, x2, w_out, ln_g, ln_b, router_w, router_b, alpha, tm=512):
    n_tok, d = x2.shape
    half = o_nsa.shape[1]
    rw = jnp.pad(router_w, ((0, 0), (0, LANE - N_EXPERTS)))
    rw_hi = rw.astype(BF16)
    rw_mid = (rw - rw_hi.astype(F32)).astype(BF16)
    rb = jnp.pad(router_b, (0, LANE - N_EXPERTS)).reshape(1, LANE)
    row_blk = lambda n: pl.BlockSpec((tm, n), lambda i: (i, 0))
    full = lambda m, n: pl.BlockSpec((m, n), lambda i: (0, 0))
    return pl.pallas_call(
        functools.partial(_out_proj_kernel, alpha=alpha),
        out_shape=(jax.ShapeDtypeStruct((n_tok, d), F32), jax.ShapeDtypeStruct((n_tok, LANE), jnp.int32),
                   jax.ShapeDtypeStruct((n_tok, LANE), F32)),
        grid=(n_tok // tm,),
        in_specs=[row_blk(half), row_blk(half), row_blk(d), full(2 * half, d), full(1, d), full(1, d),
                  full(d, LANE), full(d, LANE), full(1, LANE)],
        out_specs=(row_blk(d), row_blk(LANE), row_blk(LANE)),
        compiler_params=pltpu.CompilerParams(dimension_semantics=("parallel",), vmem_limit_bytes=VMEM_LIMIT),
        name="out_proj_router",
    )(o_nsa, o_rwkv, x2, w_out.astype(BF16), ln_g.reshape(1, d), ln_b.reshape(1, d), rw_hi, rw_mid, rb)


MOE_ITEM_ROWS = 1280
MOE_SUB_ROWS = 256
MOE_F_TILE = 256
MOE_TOKEN_TILE = 256


MOE_RANK_BLOCK = 512


def _moe_rank_kernel(e_ref, rank_ref, cnt_ref, carry_ref):
    @pl.when(pl.program_id(0) == 0)
    def _():
        carry_ref[...] = jnp.zeros_like(carry_ref)

    blk = e_ref.shape[0]
    lane = lax.broadcasted_iota(jnp.int32, (blk, LANE), 1)
    onehot = jnp.where(e_ref[...] == lane, 1.0, 0.0)
    tri = jnp.where(lax.broadcasted_iota(jnp.int32, (blk, blk), 0) >= lax.broadcasted_iota(jnp.int32, (blk, blk), 1),
                    1.0, 0.0).astype(BF16)
    csum = jnp.dot(tri, onehot.astype(BF16), preferred_element_type=F32)
    carry = carry_ref[...]
    rank_ref[...] = (jnp.sum((csum + carry) * onehot, axis=1, keepdims=True) - 1.0).astype(jnp.int32)
    carry_ref[...] = carry + csum[blk - 1:blk, :]
    cnt_ref[...] = carry_ref[...].astype(jnp.int32)


def _moe_rank(flat_e):
    n_assign = flat_e.shape[0]
    return pl.pallas_call(
        _moe_rank_kernel,
        out_shape=(jax.ShapeDtypeStruct((n_assign, 1), jnp.int32), jax.ShapeDtypeStruct((1, LANE), jnp.int32)),
        grid=(n_assign // MOE_RANK_BLOCK,),
        in_specs=[pl.BlockSpec((MOE_RANK_BLOCK, 1), lambda i: (i, 0))],
        out_specs=(pl.BlockSpec((MOE_RANK_BLOCK, 1), lambda i: (i, 0)), pl.BlockSpec((1, LANE), lambda i: (0, 0))),
        scratch_shapes=[pltpu.VMEM((1, LANE), F32)],
        compiler_params=pltpu.CompilerParams(dimension_semantics=("arbitrary",), vmem_limit_bytes=VMEM_LIMIT),
        name="moe_rank",
    )(flat_e.reshape(n_assign, 1))


def _moe_tables(top_idx, n_items):
    n_tok = top_idx.shape[0]
    flat_e = top_idx.reshape(-1)
    rank, counts = _moe_rank(flat_e)
    counts = counts[0, :N_EXPERTS]
    items_e = (counts + MOE_ITEM_ROWS - 1) // MOE_ITEM_ROWS
    items_end = jnp.cumsum(items_e)
    item_start_e = items_end - items_e
    dest = jnp.take(item_start_e * MOE_ITEM_ROWS, flat_e) + rank[:, 0]
    item = jnp.arange(n_items, dtype=jnp.int32)
    valid = item < items_end[-1]
    last_e = jnp.max(jnp.where(counts > 0, jnp.arange(N_EXPERTS, dtype=jnp.int32), 0))
    item_e = jnp.minimum(jnp.sum(items_end[None, :] <= item[:, None], axis=1).astype(jnp.int32), N_EXPERTS - 1)
    item_e = jnp.where(valid, item_e, last_e)
    item_nv = jnp.where(valid, jnp.clip(counts[item_e] - (item - item_start_e[item_e]) * MOE_ITEM_ROWS,
                                        0, MOE_ITEM_ROWS), 0).astype(jnp.int32)
    return dest.reshape(n_tok, TOP_K), item_e, item_nv


DISPATCH_ISSUE_UNROLL = 8


def _dispatch_kernel(nv_ref, dest_ref, h_ref, xg_ref, stage_ref, zero_ref, sem_ref, zsem_ref):
    tm = h_ref.shape[0]
    i = pl.program_id(0)
    slot = i % 2
    subs_per_item = MOE_ITEM_ROWS // MOE_SUB_ROWS

    @pl.when(i == 0)
    def _():
        zero_ref[...] = jnp.zeros_like(zero_ref)

        def fill(sub):
            return pltpu.make_async_copy(
                zero_ref, xg_ref.at[pl.ds(pl.multiple_of(sub * MOE_SUB_ROWS, MOE_SUB_ROWS), MOE_SUB_ROWS)], zsem_ref)

        def not_full(sub):
            return nv_ref[sub // subs_per_item] < (sub % subs_per_item + 1) * MOE_SUB_ROWS

        def start_body(sub, carry):
            @pl.when(not_full(sub))
            def _():
                fill(sub).start()
            return carry

        def wait_body(sub, carry):
            @pl.when(not_full(sub))
            def _():
                fill(sub).wait()
            return carry

        n_subs = xg_ref.shape[0] // MOE_SUB_ROWS
        lax.fori_loop(0, n_subs, start_body, 0)
        lax.fori_loop(0, n_subs, wait_body, 0)

    def wait_slot(s):
        for _ in range(TOP_K):
            pltpu.make_async_copy(stage_ref.at[s], xg_ref.at[pl.ds(0, tm)], sem_ref.at[s]).wait()

    @pl.when(i >= 2)
    def _():
        wait_slot(slot)

    stage_ref[slot] = h_ref[...]

    def body(j, carry):
        for u in range(DISPATCH_ISSUE_UNROLL):
            t = j * DISPATCH_ISSUE_UNROLL + u
            for k in range(TOP_K):
                pltpu.make_async_copy(stage_ref.at[slot, pl.ds(t, 1)], xg_ref.at[pl.ds(dest_ref[0, 0, k * tm + t], 1)],
                                      sem_ref.at[slot]).start()
        return carry

    lax.fori_loop(0, tm // DISPATCH_ISSUE_UNROLL, body, 0)

    @pl.when(i == pl.num_programs(0) - 1)
    def _():
        wait_slot(slot)

        @pl.when(i >= 1)
        def _():
            wait_slot(1 - slot)


def _dispatch(h1, dest_tiles, item_nv, n_rows, tm):
    n_tok, d = h1.shape
    return pl.pallas_call(
        _dispatch_kernel,
        out_shape=jax.ShapeDtypeStruct((n_rows, d), F32),
        grid=(n_tok // tm,),
        in_specs=[pl.BlockSpec(memory_space=pltpu.SMEM),
                  pl.BlockSpec((1, 1, TOP_K * tm), lambda i: (i, 0, 0), memory_space=pltpu.SMEM),
                  pl.BlockSpec((tm, d), lambda i: (i, 0))],
        out_specs=pl.BlockSpec(memory_space=pl.ANY),
        scratch_shapes=[pltpu.VMEM((2, tm, d), F32), pltpu.VMEM((MOE_SUB_ROWS, d), F32),
                        pltpu.SemaphoreType.DMA((2,)), pltpu.SemaphoreType.DMA(())],
        compiler_params=pltpu.CompilerParams(dimension_semantics=("arbitrary",), vmem_limit_bytes=VMEM_LIMIT),
        name="moe_dispatch",
    )(item_nv, dest_tiles, h1)


def _moe_kernel(item_e_ref, item_nv_ref, xg_ref, wg_ref, bg_ref, wu_ref, bu_ref, wd_ref, bd_ref, o_ref,
                stage_ref, xb_ref, sem_ref):
    i = pl.program_id(0)
    f = pl.program_id(1)
    nv = item_nv_ref[i]
    n_sub_max = MOE_ITEM_ROWS // MOE_SUB_ROWS
    sub_count = lambda rows_used: (rows_used + MOE_SUB_ROWS - 1) // MOE_SUB_ROWS
    n_sub = sub_count(nv)
    rows = [pl.ds(sb * MOE_SUB_ROWS, MOE_SUB_ROWS) for sb in range(n_sub_max)]
    cur = i % 2

    def fetch(item, sb):
        first = pl.multiple_of(item * MOE_ITEM_ROWS + sb * MOE_SUB_ROWS, MOE_SUB_ROWS)
        return pltpu.make_async_copy(xg_ref.at[pl.ds(first, MOE_SUB_ROWS)], stage_ref.at[sb % 2], sem_ref.at[sb % 2])

    @pl.when((i == 0) & (f == 0))
    def _():
        for sb in range(n_sub_max):
            @pl.when(sb < n_sub)
            def _(sb=sb):
                copy = fetch(0, sb)
                copy.start()
                copy.wait()
                xb_ref[0, rows[sb], :] = stage_ref[sb % 2].astype(BF16)

    has_next = i + 1 < pl.num_programs(0)
    n_sub_next = jnp.where(has_next, sub_count(item_nv_ref[jnp.minimum(i + 1, pl.num_programs(0) - 1)]), 0)
    for sb in range(n_sub_max):
        @pl.when((f == sb + 1) & (sb < n_sub_next))
        def _(sb=sb):
            fetch(i + 1, sb).wait()
            xb_ref[1 - cur, rows[sb], :] = stage_ref[sb % 2].astype(BF16)

        @pl.when((f == sb) & (sb < n_sub_next))
        def _(sb=sb):
            fetch(i + 1, sb).start()

    @pl.when(f == 0)
    def _():
        for sb in range(n_sub_max):
            bias = jnp.broadcast_to(bd_ref[0], (MOE_SUB_ROWS, o_ref.shape[1]))
            o_ref[rows[sb], :] = jnp.where(sb * MOE_SUB_ROWS < nv, bias, 0.0)

    for n in range(1, n_sub_max + 1):
        @pl.when(n_sub == n)
        def _(n=n):
            wg = wg_ref[0].astype(BF16)
            wu = wu_ref[0].astype(BF16)
            wd = wd_ref[0].astype(BF16)
            xs = [xb_ref[cur, rows[sb], :] for sb in range(n)]
            gates = [jnp.dot(xs[sb], wg, preferred_element_type=F32) for sb in range(n)]
            ups = [jnp.dot(xs[sb], wu, preferred_element_type=F32) for sb in range(n)]
            for sb in range(n):
                gate = jnp.minimum(gates[sb] + bg_ref[0], SWIGLU_LIMIT)
                up = jnp.clip(ups[sb] + bu_ref[0], -SWIGLU_LIMIT, SWIGLU_LIMIT)
                h = gate * jax.nn.sigmoid(SWIGLU_ALPHA * gate) * (up + 1.0)
                o_ref[rows[sb], :] += jnp.dot(h.astype(BF16), wd, preferred_element_type=F32)


def _moe_experts(xg, item_e, item_nv, w_gate, b_gate, w_up, b_up, w_down, b_down):
    n_rows = xg.shape[0]
    d = w_gate.shape[1]
    n_items = n_rows // MOE_ITEM_ROWS
    n_e, _, d_ff = w_gate.shape
    n_f = d_ff // MOE_F_TILE

    def f_idx(i, f, nv):
        return jnp.where(nv[i] > 0, f, n_f - 1)

    grid_spec = pltpu.PrefetchScalarGridSpec(
        num_scalar_prefetch=2,
        grid=(n_items, n_f),
        in_specs=[
            pl.BlockSpec(memory_space=pl.ANY),
            pl.BlockSpec((1, d, MOE_F_TILE), lambda i, f, e, nv: (e[i], 0, f_idx(i, f, nv))),
            pl.BlockSpec((1, 1, MOE_F_TILE), lambda i, f, e, nv: (e[i], 0, f_idx(i, f, nv))),
            pl.BlockSpec((1, d, MOE_F_TILE), lambda i, f, e, nv: (e[i], 0, f_idx(i, f, nv))),
            pl.BlockSpec((1, 1, MOE_F_TILE), lambda i, f, e, nv: (e[i], 0, f_idx(i, f, nv))),
            pl.BlockSpec((1, MOE_F_TILE, d), lambda i, f, e, nv: (e[i], f_idx(i, f, nv), 0)),
            pl.BlockSpec((1, 1, d), lambda i, f, e, nv: (e[i], 0, 0)),
        ],
        out_specs=pl.BlockSpec((MOE_ITEM_ROWS, d), lambda i, f, e, nv: (i, 0)),
        scratch_shapes=[pltpu.VMEM((2, MOE_SUB_ROWS, d), F32), pltpu.VMEM((2, MOE_ITEM_ROWS, d), BF16),
                        pltpu.SemaphoreType.DMA((2,))],
    )
    assert MOE_ITEM_ROWS // MOE_SUB_ROWS < n_f
    return pl.pallas_call(
        _moe_kernel,
        out_shape=jax.ShapeDtypeStruct((n_rows, d), F32),
        grid_spec=grid_spec,
        compiler_params=pltpu.CompilerParams(dimension_semantics=("arbitrary", "arbitrary"),
                                             vmem_limit_bytes=VMEM_LIMIT),
        name="moe_experts",
    )(item_e, item_nv, xg, w_gate, b_gate.reshape(n_e, 1, d_ff), w_up, b_up.reshape(n_e, 1, d_ff), w_down,
      b_down.reshape(n_e, 1, d))


COMBINE_ISSUE_UNROLL = 16


def _final_kernel(dest_ref, dest_next_ref, h_ref, eo_ref, wgt_ref, p_ref, g_ref, b_ref, wgate_ref, wple_ref,
                  o_ref, rows_ref, sem_ref, *, alpha):
    tm, d = h_ref.shape
    n_rows = TOP_K * tm
    i = pl.program_id(0)
    slot = i % 2

    def row_copy(idx_ref, r, s):
        return pltpu.make_async_copy(eo_ref.at[pl.ds(idx_ref[0, 0, r], 1)], rows_ref.at[s, pl.ds(r, 1)],
                                     sem_ref.at[s])

    def issue(idx_ref, s):
        def body(j, carry):
            for u in range(COMBINE_ISSUE_UNROLL):
                row_copy(idx_ref, j * COMBINE_ISSUE_UNROLL + u, s).start()
            return carry

        lax.fori_loop(0, n_rows // COMBINE_ISSUE_UNROLL, body, 0)

    @pl.when(i == 0)
    def _():
        issue(dest_ref, 0)

    @pl.when(i + 1 < pl.num_programs(0))
    def _():
        issue(dest_next_ref, 1 - slot)

    pltpu.make_async_copy(eo_ref.at[pl.ds(0, n_rows)], rows_ref.at[slot], sem_ref.at[slot]).wait()
    wgt = wgt_ref[...]
    ffn = wgt[:, 0:1] * rows_ref[slot, 0:tm, :]
    for k in range(1, TOP_K):
        ffn = ffn + wgt[:, k:k + 1] * rows_ref[slot, k * tm:(k + 1) * tm, :]
    h = _layer_norm(alpha * h_ref[...] + ffn, g_ref[...], b_ref[...])
    gate = jax.nn.sigmoid(jnp.dot(h.astype(BF16), wgate_ref[...], preferred_element_type=F32))
    ple = jnp.dot(p_ref[...].astype(BF16), wple_ref[...], preferred_element_type=F32)
    o_ref[...] = h + gate * ple


def _dest_tiles(dest, tm):
    n_tiles = dest.shape[0] // tm
    return dest.reshape(n_tiles, tm, TOP_K).transpose(0, 2, 1).reshape(n_tiles, 1, TOP_K * tm)


def _final(h1, eo, dest_tiles, top_w, p2, ln_g, ln_b, ple_gate_w, ple_w, alpha, tm):
    n_tok, d = h1.shape
    n_tiles = n_tok // tm
    row_blk = lambda n: pl.BlockSpec((tm, n), lambda i: (i, 0))
    full = lambda m, n: pl.BlockSpec((m, n), lambda i: (0, 0))
    idx_blk = lambda fn: pl.BlockSpec((1, 1, TOP_K * tm), fn, memory_space=pltpu.SMEM)
    return pl.pallas_call(
        functools.partial(_final_kernel, alpha=alpha),
        out_shape=jax.ShapeDtypeStruct((n_tok, d), F32),
        grid=(n_tiles,),
        in_specs=[idx_blk(lambda i: (i, 0, 0)), idx_blk(lambda i: (jnp.minimum(i + 1, n_tiles - 1), 0, 0)),
                  row_blk(d), pl.BlockSpec(memory_space=pl.ANY), row_blk(LANE), row_blk(p2.shape[1]),
                  full(1, d), full(1, d), full(d, d), full(p2.shape[1], d)],
        out_specs=row_blk(d),
        scratch_shapes=[pltpu.VMEM((2, TOP_K * tm, d), F32), pltpu.SemaphoreType.DMA((2,))],
        compiler_params=pltpu.CompilerParams(dimension_semantics=("arbitrary",), vmem_limit_bytes=VMEM_LIMIT),
        name="combine_ln_ple",
    )(dest_tiles, dest_tiles, h1, eo, top_w, p2, ln_g.reshape(1, d), ln_b.reshape(1, d),
      ple_gate_w.astype(BF16), ple_w.astype(BF16))


def kernel(x, p, w_in, cmp_pe_k, cmp_w1_k, cmp_w2_k, cmp_pe_v, cmp_w1_v, cmp_w2_v, rwkv_mu, rwkv_w0, rwkv_w_up, rwkv_a0, rwkv_a_up, rwkv_g_up, rwkv_k_k, rwkv_k_a, rwkv_r_k, rwkv_ln_g, rwkv_ln_b, w_out, ln1_g, ln1_b, router_w, router_b, exp_w_gate, exp_b_gate, exp_w_up, exp_b_up, exp_w_down, exp_b_down, ln2_g, ln2_b, ple_w, ple_gate_w):
    b, t_len, d = x.shape
    depth = w_in.shape[0]
    alpha = float((2 * depth) ** 0.25)
    n_tok = b * t_len
    n_items = (n_tok * TOP_K) // MOE_ITEM_ROWS + N_EXPERTS
    h = x.reshape(n_tok, d)
    for i in range(depth):
        w_bf = _take_columns(w_in[i].astype(BF16), _z_source_columns())
        z3 = _in_proj(h, w_bf).reshape(b, t_len, Z_COLS)
        pe, w1, w2 = _nsa_prepare(cmp_pe_k[i], cmp_w1_k[i], cmp_w2_k[i], cmp_pe_v[i], cmp_w1_v[i], cmp_w2_v[i])
        kv_cmp = _compress(z3, pe, w1, w2)
        o_nsa = _nsa_attention(z3, kv_cmp)
        o_rwkv = _rwkv_time_mix(z3, rwkv_mu[i], rwkv_w0[i], rwkv_w_up[i], rwkv_a0[i], rwkv_a_up[i], rwkv_g_up[i],
                                rwkv_k_k[i], rwkv_k_a[i], rwkv_r_k[i].reshape(-1), rwkv_ln_g[i], rwkv_ln_b[i])
        h1, top_idx, top_w = _out_proj_router(o_nsa.reshape(n_tok, -1), o_rwkv.reshape(n_tok, -1), h, w_out[i],
                                              ln1_g[i], ln1_b[i], router_w[i], router_b[i], alpha)
        dest, item_e, item_nv = _moe_tables(top_idx[:, :TOP_K], n_items)
        dest_tiles = _dest_tiles(dest, MOE_TOKEN_TILE)
        xg = _dispatch(h1, dest_tiles, item_nv, n_items * MOE_ITEM_ROWS, MOE_TOKEN_TILE)
        eo = _moe_experts(xg, item_e, item_nv, exp_w_gate[i], exp_b_gate[i], exp_w_up[i], exp_b_up[i],
                          exp_w_down[i], exp_b_down[i])
        h = _final(h1, eo, dest_tiles, top_w, p[i].reshape(n_tok, -1), ln2_g[i], ln2_b[i], ple_gate_w[i],
                   ple_w[i], alpha, MOE_TOKEN_TILE)
    return h.reshape(b, t_len, d)
```

```python
import functools

import numpy as np
import jax
import jax.numpy as jnp
from jax import lax
from jax.experimental import pallas as pl
from jax.experimental.pallas import tpu as pltpu

F32 = jnp.float32
BF16 = jnp.bfloat16

LANE = 128
D_MODEL = 2048
HEAD_DIM = 64
NSA_HEADS = 16
NSA_KV_HEADS = 4
NSA_GROUP = NSA_HEADS // NSA_KV_HEADS
NSA_WIDTH = NSA_HEADS * HEAD_DIM
NSA_KV_WIDTH = NSA_KV_HEADS * HEAD_DIM
CMP_BLOCK = 32
CMP_STRIDE = 16
SEL_BLOCK = 64
N_SELECT = 16
WINDOW = 512
N_GATES = 3
Q_BLOCK = 64
RWKV_HEADS = 16
RWKV_WIDTH = RWKV_HEADS * HEAD_DIM
DECAY_LORA = 96
ICLR_LORA = 96
GATE_LORA = 256
GN_EPS = 64e-5
N_EXPERTS = 32
TOP_K = 4
SWIGLU_ALPHA = 1.702
SWIGLU_LIMIT = 7.0
PLE_DIM = 256
LN_EPS = 1e-5
NEG_INF = -1e30

NSA_COLS = NSA_WIDTH + 6 * NSA_KV_WIDTH + NSA_HEADS * N_GATES
RWKV_COLS = 3 * RWKV_WIDTH + DECAY_LORA + ICLR_LORA + GATE_LORA

RB_R = 0
RB_K = RB_R + RWKV_WIDTH // LANE
RB_V = RB_K + RWKV_WIDTH // LANE
RB_WLO = RB_V + RWKV_WIDTH // LANE
RB_ALO = RB_WLO + 1
RB_GLO = RB_ALO + 1
RWKV_BLOCKS = RB_GLO + GATE_LORA // LANE
ZB_RWKV = 0
ZB_Q = ZB_RWKV + RWKV_BLOCKS
ZB_KVC = ZB_Q + NSA_WIDTH // LANE
ZB_KVS = ZB_KVC + NSA_KV_HEADS
ZB_KVW = ZB_KVS + NSA_KV_HEADS
ZB_GATE = ZB_KVW + NSA_KV_HEADS
Z_BLOCKS = ZB_GATE + NSA_KV_HEADS
Z_COLS = Z_BLOCKS * LANE
assert (ZB_Q * LANE) % (NSA_GROUP * HEAD_DIM) == 0

VMEM_LIMIT = 56 * 1024 * 1024


def _z_source_columns():
    src = np.full((Z_COLS,), -1, np.int64)
    src[ZB_Q * LANE:ZB_Q * LANE + NSA_WIDTH] = np.arange(NSA_WIDTH)
    for branch in range(3):
        k0 = NSA_WIDTH + 2 * branch * NSA_KV_WIDTH
        v0 = k0 + NSA_KV_WIDTH
        for g in range(NSA_KV_HEADS):
            base = (ZB_KVC + branch * NSA_KV_HEADS + g) * LANE
            src[base:base + HEAD_DIM] = k0 + g * HEAD_DIM + np.arange(HEAD_DIM)
            src[base + HEAD_DIM:base + 2 * HEAD_DIM] = v0 + g * HEAD_DIM + np.arange(HEAD_DIM)
    g0 = NSA_WIDTH + 6 * NSA_KV_WIDTH
    for g in range(NSA_KV_HEADS):
        base = (ZB_GATE + g) * LANE
        for i in range(N_GATES):
            for r in range(NSA_GROUP):
                src[base + i * NSA_GROUP + r] = g0 + (g * NSA_GROUP + r) * N_GATES + i
    rwkv = _rwkv_source_columns()
    src[ZB_RWKV * LANE:(ZB_RWKV + RWKV_BLOCKS) * LANE] = np.where(rwkv >= 0, NSA_COLS + rwkv, -1)
    return src


def _rwkv_source_columns():
    src = np.full((RWKV_BLOCKS * LANE,), -1, np.int64)
    src[:3 * RWKV_WIDTH] = np.arange(3 * RWKV_WIDTH)
    src[RB_WLO * LANE:RB_WLO * LANE + DECAY_LORA] = 3 * RWKV_WIDTH + np.arange(DECAY_LORA)
    src[RB_ALO * LANE:RB_ALO * LANE + ICLR_LORA] = 3 * RWKV_WIDTH + DECAY_LORA + np.arange(ICLR_LORA)
    src[RB_GLO * LANE:] = 3 * RWKV_WIDTH + DECAY_LORA + ICLR_LORA + np.arange(GATE_LORA)
    return src


def _take_columns(w, src):
    pieces, i, n = [], 0, len(src)
    while i < n:
        j = i + 1
        if src[i] < 0:
            while j < n and src[j] < 0:
                j += 1
            pieces.append(jnp.zeros(w.shape[:-1] + (j - i,), w.dtype))
        else:
            while j < n and src[j] == src[i] + (j - i):
                j += 1
            pieces.append(w[..., int(src[i]):int(src[i]) + j - i])
        i = j
    return jnp.concatenate(pieces, axis=-1)


def _pad_rows(w, rows):
    return jnp.pad(w, ((0, rows - w.shape[0]), (0, 0)))


def _in_proj_kernel(x_ref, w_ref, z_ref, xb_ref):
    @pl.when(pl.program_id(1) == 0)
    def _():
        xb_ref[...] = x_ref[...].astype(BF16)

    z_ref[...] = jnp.dot(xb_ref[...], w_ref[...], preferred_element_type=F32).astype(z_ref.dtype)


def _in_proj(x2, w_bf, tm=1024, tn=1664):
    n_tok, d = x2.shape
    n_cols = w_bf.shape[1]
    return pl.pallas_call(
        _in_proj_kernel,
        out_shape=jax.ShapeDtypeStruct((n_tok, n_cols), BF16),
        grid=(n_tok // tm, n_cols // tn),
        in_specs=[pl.BlockSpec((tm, d), lambda i, j: (i, 0)),
                  pl.BlockSpec((d, tn), lambda i, j: (0, j))],
        out_specs=pl.BlockSpec((tm, tn), lambda i, j: (i, j)),
        scratch_shapes=[pltpu.VMEM((tm, d), BF16)],
        compiler_params=pltpu.CompilerParams(dimension_semantics=("parallel", "arbitrary"),
                                             vmem_limit_bytes=VMEM_LIMIT),
        name="in_proj",
    )(x2, w_bf)


def _compress_kernel(kv_ref, pe_ref, w1_ref, w2_ref, out_ref, kv32_ref):
    n_chunks = kv_ref.shape[1] // CMP_STRIDE
    kv32_ref[...] = kv_ref[0].astype(F32)
    acc_lo = jnp.zeros((n_chunks, LANE), F32)
    acc_hi = jnp.zeros((n_chunks, LANE), F32)
    for i in range(CMP_STRIDE):
        rows = kv32_ref[pl.ds(i, n_chunks, stride=CMP_STRIDE), :]
        lo = (rows + pe_ref[i:i + 1, :]).astype(BF16)
        hi = (rows + pe_ref[CMP_STRIDE + i:CMP_STRIDE + i + 1, :]).astype(BF16)
        acc_lo += jnp.dot(lo, w1_ref[i], preferred_element_type=F32)
        acc_hi += jnp.dot(hi, w1_ref[CMP_STRIDE + i], preferred_element_type=F32)
    shifted = jnp.concatenate([acc_hi[1:], jnp.zeros((1, LANE), F32)], axis=0)
    hid = jax.nn.gelu(acc_lo + shifted)
    out = jnp.dot(hid.astype(BF16), w2_ref[...], preferred_element_type=F32)
    row = lax.broadcasted_iota(jnp.int32, out.shape, 0)
    out_ref[0, 0] = jnp.where(row < n_chunks - 1, out, 0.0).astype(out_ref.dtype)


def _compress(z3, pe, w1, w2):
    b, t_len, _ = z3.shape
    n_chunks = t_len // CMP_STRIDE
    return pl.pallas_call(
        _compress_kernel,
        out_shape=jax.ShapeDtypeStruct((b, NSA_KV_HEADS, n_chunks, LANE), BF16),
        grid=(b, NSA_KV_HEADS),
        in_specs=[pl.BlockSpec((1, t_len, LANE), lambda bi, g: (bi, 0, ZB_KVC + g)),
                  pl.BlockSpec((CMP_BLOCK, LANE), lambda bi, g: (0, 0)),
                  pl.BlockSpec((CMP_BLOCK, LANE, LANE), lambda bi, g: (0, 0, 0)),
                  pl.BlockSpec((LANE, LANE), lambda bi, g: (0, 0))],
        out_specs=pl.BlockSpec((1, 1, n_chunks, LANE), lambda bi, g: (bi, g, 0, 0)),
        scratch_shapes=[pltpu.VMEM((t_len, LANE), F32)],
        compiler_params=pltpu.CompilerParams(dimension_semantics=("parallel", "parallel"),
                                             vmem_limit_bytes=VMEM_LIMIT),
        name="kv_compress",
    )(z3, pe, w1, w2)


def _block_diag2(a, b):
    za = jnp.zeros(a.shape[:-1] + (b.shape[-1],), a.dtype)
    zb = jnp.zeros(b.shape[:-1] + (a.shape[-1],), b.dtype)
    return jnp.concatenate([jnp.concatenate([a, za], axis=-1), jnp.concatenate([zb, b], axis=-1)], axis=-2)


def _stack_heads(x):
    return jnp.concatenate([x[:, r * HEAD_DIM:(r + 1) * HEAD_DIM] for r in range(NSA_GROUP)], axis=0)


def _dot_nt(a, b):
    return lax.dot_general(a, b, (((1,), (1,)), ((), ())), preferred_element_type=F32)


def _eye(n, dtype):
    return jnp.where(lax.broadcasted_iota(jnp.int32, (n, n), 0) == lax.broadcasted_iota(jnp.int32, (n, n), 1),
                     1.0, 0.0).astype(dtype)


NSA_KEY_CHUNK = 256
NSA_V_ROWS = HEAD_DIM + 16


def _nsa_kernel(slopes_ref, *refs, n_sel, n_g):
    q_refs = refs[:n_g]
    (kvc_ref, kvs_ref, kvw_ref, gate_ref, ovl_ref, o_ref, vst_ref, vwt_ref, vct_ref, sel_ref,
     sc0_ref, sc1_ref, p0_ref, p1_ref) = refs[n_g:]
    g_base = pl.program_id(1) * n_g
    qi = pl.program_id(2)
    q0 = qi * Q_BLOCK
    cols = NSA_GROUP * Q_BLOCK
    kc_len = NSA_KEY_CHUNK
    blocks_per_chunk = kc_len // SEL_BLOCK
    n_chunks = kvs_ref.shape[1] // kc_len
    n_cmp_pad = kvc_ref.shape[2]
    n_slc = ovl_ref.shape[0]
    v_rows = vst_ref.shape[2]
    eye_dh = _eye(HEAD_DIM, BF16)
    k_lanes = lambda gg: slice(gg * LANE, gg * LANE + HEAD_DIM)
    v_lanes = lambda gg: slice(gg * LANE + HEAD_DIM, (gg + 1) * LANE)

    @pl.when(qi == 0)
    def _():
        ones_row = jnp.where(lax.broadcasted_iota(jnp.int32, (v_rows - HEAD_DIM, kc_len), 0) == 0,
                             1.0, 0.0).astype(BF16)

        def body(c, carry):
            rows = pl.ds(pl.multiple_of(c * kc_len, kc_len), kc_len)
            for gg in range(n_g):
                vst_ref[gg, c, 0:HEAD_DIM] = _dot_nt(eye_dh, kvs_ref[0, rows, v_lanes(gg)]).astype(BF16)
                vwt_ref[gg, c, 0:HEAD_DIM] = _dot_nt(eye_dh, kvw_ref[0, rows, v_lanes(gg)]).astype(BF16)
                vst_ref[gg, c, HEAD_DIM:v_rows] = ones_row
                vwt_ref[gg, c, HEAD_DIM:v_rows] = ones_row
            return carry

        lax.fori_loop(0, n_chunks, body, 0)
        for gg in range(n_g):
            vct_ref[gg] = _dot_nt(eye_dh, kvc_ref[0, gg, :, HEAD_DIM:2 * HEAD_DIM]).astype(BF16)

    log2e = float(np.log2(np.e))
    lane = lax.broadcasted_iota(jnp.int32, (1, cols), 1)
    head = lane // Q_BLOCK
    tq = q0 + lane % Q_BLOCK
    sub = lax.broadcasted_iota(jnp.int32, (kc_len, cols), 0)
    half = lax.broadcasted_iota(jnp.int32, (1, LANE), 1) // Q_BLOCK
    blk = lax.broadcasted_iota(jnp.int32, (n_slc, LANE), 0)
    blk8 = lax.broadcasted_iota(jnp.int32, (8, LANE), 0)
    cmp_end = lax.broadcasted_iota(jnp.int32, (n_cmp_pad, 1), 0) * CMP_STRIDE + (CMP_BLOCK - 1)
    d_cmp = tq - cmp_end
    m_cmp = d_cmp >= 0
    any_cmp = tq >= CMP_BLOCK - 1
    d_cmp_f = d_cmp.astype(F32)
    forced = (blk == 0) | (blk == qi) | (blk == qi - 1)
    ovl = ovl_ref[...]

    groups = range(n_g)
    qs, slopes = [], []
    for gg in groups:
        q = _stack_heads(q_refs[gg][0]).astype(F32)
        qs.append((q * (HEAD_DIM ** -0.5 * log2e)).astype(BF16))
        slope = jnp.zeros((1, cols), F32)
        for r in range(NSA_GROUP):
            slope = jnp.where(head == r, slopes_ref[(g_base + gg) * NSA_GROUP + r] * log2e, slope)
        slopes.append(slope)
    st = [_dot_nt(kvc_ref[0, gg, :, 0:HEAD_DIM], qs[gg]) for gg in groups]
    p_cmp = []
    for gg in groups:
        s_m = jnp.where(m_cmp, st[gg] - slopes[gg] * d_cmp_f, NEG_INF)
        e = jnp.exp2(s_m - jnp.max(s_m, axis=0, keepdims=True))
        inv = jnp.where(any_cmp, 1.0 / jnp.maximum(jnp.sum(e, axis=0, keepdims=True), 1e-30), 0.0)
        p_cmp.append(e * inv)
    o_cmp = [jnp.dot(vct_ref[gg], p_cmp[gg].astype(BF16), preferred_element_type=F32) for gg in groups]
    parts = []
    for gg in groups:
        y = p_cmp[gg][:, 0:LANE] + p_cmp[gg][:, LANE:2 * LANE]
        parts.append(_split3(y + pltpu.roll(y, Q_BLOCK, 1)))
    pooled = [[jnp.dot(ovl, parts[gg][i], preferred_element_type=F32) for gg in groups] for i in range(3)]
    for gg in groups:
        imp = pooled[0][gg] + pooled[1][gg] + pooled[2][gg]
        imp = jnp.where(forced, jnp.inf, jnp.where(blk > qi, -jnp.inf, imp))
        tiles = [imp[t:t + 8] for t in range(0, n_slc, 8)]
        ranks = [jnp.zeros((8, LANE), F32) for _ in tiles]
        for jp in range(0, n_slc, 2):
            row = jnp.where(half == 0, imp[jp:jp + 1, :], imp[jp + 1:jp + 2, :])
            for ti, tile in enumerate(tiles):
                t0 = ti * 8
                if t0 > jp + 1:
                    hit = jnp.where(row >= tile, 1.0, 0.0)
                elif t0 + 7 <= jp:
                    hit = jnp.where(row > tile, 1.0, 0.0)
                else:
                    hit = jnp.where(blk8 + t0 > jp + half, jnp.where(row >= tile, 1.0, 0.0),
                                    jnp.where(row > tile, 1.0, 0.0))
                ranks[ti] = ranks[ti] + hit
        rank = jnp.concatenate(ranks, axis=0)
        rank = rank + pltpu.roll(rank, Q_BLOCK, 1)
        neg = jnp.where((rank < n_sel) & (blk <= qi), 0.0, NEG_INF)
        neg2 = jnp.concatenate([neg, neg], axis=1)
        for j in range(n_slc):
            sel_ref[gg, j] = jnp.broadcast_to(neg2[j:j + 1, :], (8, cols))
    sub_f = sub.astype(F32)
    fronts = [(qs[gg], slopes[gg], slopes[gg] * sub_f, o_cmp[gg]) for gg in groups]

    init1 = (jnp.full((1, cols), NEG_INF, F32), jnp.zeros((v_rows, cols), F32))

    def block_mask(gg, c):
        tiles = [sel_ref[gg, c * blocks_per_chunk + i] for i in range(blocks_per_chunk)]
        return jnp.concatenate([t for t in tiles for _ in range(SEL_BLOCK // 8)], axis=0)

    streams = {"slc": (kvs_ref, vst_ref), "win": (kvw_ref, vwt_ref)}
    sc_slots = (sc0_ref, sc1_ref)
    p_slots = (p0_ref, p1_ref)

    def score(gg, pos, slot):
        stream, c, _ = pos
        rows = pl.ds(pl.multiple_of(c * kc_len, kc_len), kc_len)
        sc_slots[slot][gg] = _dot_nt(streams[stream][0][0, rows, k_lanes(gg)], fronts[gg][0])

    def softmax(gg, pos, slot, m_prev):
        _, c, mask_fn = pos
        _, slope, bias_local, _ = fronts[gg]
        sc = sc_slots[slot][gg] + bias_local + mask_fn(gg)
        shift = slope * (c * kc_len - q0).astype(F32)
        m_new = jnp.maximum(m_prev, jnp.max(sc, axis=0, keepdims=True) + shift)
        p_slots[slot][gg] = jnp.exp2(sc - (m_new - shift)).astype(BF16)
        return m_new, jnp.exp2(m_prev - m_new)

    def weigh(gg, pos, slot, alpha, acc):
        stream, c, _ = pos
        return alpha * acc + jnp.dot(streams[stream][1][gg, c], p_slots[slot][gg], preferred_element_type=F32)

    def step(pos, slot, nxt, prev, state, alphas):
        if nxt is not None:
            for gg in range(n_g):
                score(gg, nxt, 1 - slot)
        state = {s: list(v) for s, v in state.items()}
        new_alphas = []
        for gg in range(n_g):
            m_prev, acc = state[pos[0]][gg]
            m_new, alpha = softmax(gg, pos, slot, m_prev)
            state[pos[0]][gg] = (m_new, acc)
            new_alphas.append(alpha)
            if prev is not None:
                m_p, acc_p = state[prev[0]][gg]
                state[prev[0]][gg] = (m_p, weigh(gg, prev, 1 - slot, alphas[gg], acc_p))
        return state, new_alphas

    c_cur = qi // blocks_per_chunk
    n_pairs = c_cur // 2
    plain = lambda c: ("slc", c, lambda gg: block_mask(gg, c))

    for gg in range(n_g):
        p1_ref[gg] = jnp.zeros((kc_len, cols), BF16)
        score(gg, plain(0), 0)

    def pair_body(i, carry):
        slc_state, alphas = carry
        k0 = 2 * i
        state, alphas = step(plain(k0), 0, plain(k0 + 1), plain(jnp.maximum(k0 - 1, 0)),
                             {"slc": slc_state}, alphas)
        state, alphas = step(plain(k0 + 1), 1, plain(k0 + 2), plain(k0), state, alphas)
        return tuple(state["slc"]), tuple(alphas)

    slc_state, alphas = lax.fori_loop(0, n_pairs, pair_body,
                                      (tuple(init1 for _ in range(n_g)),
                                       tuple(jnp.ones((1, cols), F32) for _ in range(n_g))))
    k_t = 2 * n_pairs
    odd_neg = jnp.where(c_cur % 2 == 1, 0.0, NEG_INF)
    causal_neg = jnp.where((c_cur * kc_len + sub) <= tq, 0.0, NEG_INF)
    tail = [("slc", k_t, lambda gg: block_mask(gg, k_t) + odd_neg),
            ("slc", c_cur, lambda gg: block_mask(gg, c_cur) + causal_neg)]
    for i in range(WINDOW // kc_len, -1, -1):
        c_raw = c_cur - i
        c_win = jnp.maximum(c_raw, 0)
        dist = tq - (c_win * kc_len + sub)
        band_neg = jnp.where((dist >= 0) & (dist < WINDOW) & (c_raw >= 0), 0.0, NEG_INF)
        tail.append(("win", c_win, lambda gg, band_neg=band_neg: band_neg))
    state = {"slc": list(slc_state), "win": [init1 for _ in range(n_g)]}
    prev = plain(jnp.maximum(k_t - 1, 0))
    for idx, pos in enumerate(tail):
        nxt = tail[idx + 1] if idx + 1 < len(tail) else None
        state, alphas = step(pos, idx % 2, nxt, prev, state, alphas)
        prev = pos
    last_slot = (len(tail) - 1) % 2
    for gg in range(n_g):
        m_p, acc_p = state[prev[0]][gg]
        state[prev[0]][gg] = (m_p, weigh(gg, prev, last_slot, alphas[gg], acc_p))

    eye_q = _eye(Q_BLOCK, BF16)
    eye_lane = _eye(LANE, BF16)
    gates = [jax.nn.sigmoid(_dot_nt(eye_lane, gate_ref[0, :, gg * LANE:(gg + 1) * LANE])) for gg in groups]

    def gate_row(gg, i):
        return jnp.concatenate([gates[gg][i * NSA_GROUP + r:i * NSA_GROUP + r + 1, :] for r in range(NSA_GROUP)],
                               axis=1)

    def normalized(acc):
        return acc[0:HEAD_DIM] / jnp.maximum(acc[HEAD_DIM:HEAD_DIM + 1], 1e-30)

    o_rows = []
    for gg in groups:
        o = (gate_row(gg, 0) * fronts[gg][3] + gate_row(gg, 1) * normalized(state["slc"][gg][1])
             + gate_row(gg, 2) * normalized(state["win"][gg][1])).astype(BF16)
        o_rows.append(jnp.concatenate([o[:, r * Q_BLOCK:(r + 1) * Q_BLOCK] for r in range(NSA_GROUP)], axis=0))
    outs = [_dot_nt(eye_q, o_rows[gg]) for gg in groups]
    for gg in groups:
        o_ref[0, :, gg * cols:(gg + 1) * cols] = outs[gg].astype(o_ref.dtype)


NSA_GROUPS_PER_STEP = 4


def _nsa_attention(z3, kv_cmp):
    b, t_len, _ = z3.shape
    n_q = t_len // Q_BLOCK
    n_slc = t_len // SEL_BLOCK
    n_cmp_pad = kv_cmp.shape[2]
    n_sel = min(N_SELECT, n_slc)
    n_g = NSA_GROUPS_PER_STEP
    cols = NSA_GROUP * Q_BLOCK
    slopes = jnp.exp2(-8.0 * jnp.arange(1, NSA_HEADS + 1, dtype=F32) / NSA_HEADS)
    c_idx = np.arange(n_cmp_pad)
    cmp_start = c_idx * CMP_STRIDE
    cmp_end = cmp_start + CMP_BLOCK - 1
    slc_start = np.arange(n_slc) * SEL_BLOCK
    overlap = np.clip(np.minimum(cmp_end[None, :], slc_start[:, None] + SEL_BLOCK - 1)
                      - np.maximum(cmp_start[None, :], slc_start[:, None]) + 1, 0, None).astype(np.float32)
    overlap[:, c_idx >= t_len // CMP_STRIDE - CMP_BLOCK // CMP_STRIDE + 1] = 0.0
    kernel = functools.partial(_nsa_kernel, n_sel=n_sel, n_g=n_g)
    n_kc = t_len // NSA_KEY_CHUNK
    q_spec = lambda gg: pl.BlockSpec((1, Q_BLOCK, cols),
                                     lambda bi, g, qi: (bi, qi, ZB_Q * LANE // cols + g * n_g + gg))
    slab = lambda zb: pl.BlockSpec((1, t_len, n_g * LANE), lambda bi, g, qi: (bi, 0, zb // n_g + g))
    assert ZB_KVS % n_g == 0 and ZB_KVW % n_g == 0 and ZB_GATE % n_g == 0
    return pl.pallas_call(
        kernel,
        out_shape=jax.ShapeDtypeStruct((b, t_len, NSA_WIDTH), BF16),
        grid=(b, NSA_KV_HEADS // n_g, n_q),
        in_specs=[pl.BlockSpec(memory_space=pltpu.SMEM)] + [q_spec(gg) for gg in range(n_g)] + [
            pl.BlockSpec((1, n_g, n_cmp_pad, LANE), lambda bi, g, qi: (bi, g, 0, 0)),
            slab(ZB_KVS), slab(ZB_KVW),
            pl.BlockSpec((1, Q_BLOCK, n_g * LANE), lambda bi, g, qi: (bi, qi, ZB_GATE // n_g + g)),
            pl.BlockSpec((n_slc, n_cmp_pad), lambda bi, g, qi: (0, 0)),
        ],
        out_specs=pl.BlockSpec((1, Q_BLOCK, n_g * cols), lambda bi, g, qi: (bi, qi, g)),
        scratch_shapes=[pltpu.VMEM((n_g, n_kc, NSA_V_ROWS, NSA_KEY_CHUNK), BF16),
                        pltpu.VMEM((n_g, n_kc, NSA_V_ROWS, NSA_KEY_CHUNK), BF16),
                        pltpu.VMEM((n_g, HEAD_DIM, n_cmp_pad), BF16),
                        pltpu.VMEM((n_g, n_slc, 8, cols), F32),
                        pltpu.VMEM((n_g, NSA_KEY_CHUNK, cols), F32),
                        pltpu.VMEM((n_g, NSA_KEY_CHUNK, cols), F32),
                        pltpu.VMEM((n_g, NSA_KEY_CHUNK, cols), BF16),
                        pltpu.VMEM((n_g, NSA_KEY_CHUNK, cols), BF16)],
        compiler_params=pltpu.CompilerParams(dimension_semantics=("parallel", "parallel", "arbitrary"),
                                             vmem_limit_bytes=VMEM_LIMIT),
        name="nsa_attention",
    )(slopes, *([z3] * n_g), kv_cmp, z3, z3, z3, jnp.asarray(overlap, BF16))


def _nsa_prepare(cmp_pe_k, cmp_w1_k, cmp_w2_k, cmp_pe_v, cmp_w1_v, cmp_w2_v):
    pe = jnp.concatenate([cmp_pe_k, cmp_pe_v], axis=-1)
    w1 = _block_diag2(cmp_w1_k.reshape(CMP_BLOCK, HEAD_DIM, HEAD_DIM),
                      cmp_w1_v.reshape(CMP_BLOCK, HEAD_DIM, HEAD_DIM)).astype(BF16)
    w2 = _block_diag2(cmp_w2_k, cmp_w2_v).astype(BF16)
    return pe, w1, w2


RWKV_CHUNK = 64


def _rwkv_prep_kernel(z_ref, mu_ref, w0_ref, wup_ref, a0_ref, aup_ref, gup_ref, kk_ref, ka_ref,
                      r_ref, k_ref, v_ref, kkr_ref, a_ref, lw_ref, g_ref, carry_ref):
    w = RWKV_WIDTH

    @pl.when(pl.program_id(1) == 0)
    def _():
        carry_ref[...] = jnp.zeros_like(carry_ref)

    z = z_ref[0].astype(F32)
    tc = z.shape[0]
    row = lax.broadcasted_iota(jnp.int32, z.shape, 0)
    prev = jnp.where(row == 0, carry_ref[0:1, :], pltpu.roll(z, 1, 0))
    carry_ref[0:1, :] = z[tc - 1:tc, :]
    zs = z + (prev - z) * mu_ref[...]
    r = zs[:, RB_R * LANE:RB_R * LANE + w]
    k = zs[:, RB_K * LANE:RB_K * LANE + w]
    v = zs[:, RB_V * LANE:RB_V * LANE + w]
    w_lo = zs[:, RB_WLO * LANE:(RB_WLO + 1) * LANE]
    a_lo = zs[:, RB_ALO * LANE:(RB_ALO + 1) * LANE]
    g_lo = zs[:, RB_GLO * LANE:RB_GLO * LANE + GATE_LORA]
    d = w0_ref[...] + jnp.dot(jnp.tanh(w_lo).astype(BF16), wup_ref[...], preferred_element_type=F32)
    w_raw = -jax.nn.softplus(-d) - 0.5
    lw_ref[0] = -jnp.exp(w_raw)
    a = jax.nn.sigmoid(a0_ref[...] + jnp.dot(a_lo.astype(BF16), aup_ref[...], preferred_element_type=F32))
    g = jnp.dot(jax.nn.sigmoid(g_lo).astype(BF16), gup_ref[...], preferred_element_type=F32)
    r_ref[0] = r.astype(r_ref.dtype)
    v_ref[0] = v.astype(v_ref.dtype)
    kkr_ref[0] = (k * kk_ref[...]).astype(kkr_ref.dtype)
    k_ref[0] = (k * (1.0 + (a - 1.0) * ka_ref[...])).astype(k_ref.dtype)
    a_ref[0] = a.astype(a_ref.dtype)
    g_ref[0] = g.astype(g_ref.dtype)


def _rwkv_prep(z3, mu, w0, w_up, a0, a_up, g_up, k_k, k_a, tc=256):
    b, t_len, _ = z3.shape
    w = RWKV_WIDTH
    ncol = RWKV_BLOCKS * LANE
    vec = lambda n: pl.BlockSpec((1, n), lambda bi, ti: (0, 0))
    mat = lambda m, n: pl.BlockSpec((m, n), lambda bi, ti: (0, 0))
    out_bf = jax.ShapeDtypeStruct((b, t_len, w), BF16)
    out_f32 = jax.ShapeDtypeStruct((b, t_len, w), F32)
    out_spec = pl.BlockSpec((1, tc, w), lambda bi, ti: (bi, ti, 0))
    return pl.pallas_call(
        _rwkv_prep_kernel,
        out_shape=(out_bf, out_bf, out_bf, out_bf, out_bf, out_f32, out_bf),
        grid=(b, t_len // tc),
        in_specs=[pl.BlockSpec((1, tc, ncol), lambda bi, ti: (bi, ti, ZB_RWKV)),
                  vec(ncol), vec(w), mat(LANE, w), vec(w), mat(LANE, w), mat(GATE_LORA, w), vec(w), vec(w)],
        out_specs=(out_spec,) * 7,
        scratch_shapes=[pltpu.VMEM((8, ncol), F32)],
        compiler_params=pltpu.CompilerParams(dimension_semantics=("parallel", "arbitrary"),
                                             vmem_limit_bytes=VMEM_LIMIT),
        name="rwkv_prep",
    )(z3, mu, w0, w_up, a0, a_up, g_up, k_k, k_a)


def _pair_blocks(x):
    lane = lax.broadcasted_iota(jnp.int32, x.shape, 1)
    zero = jnp.zeros((), x.dtype)
    return jnp.concatenate([jnp.where(lane < HEAD_DIM, x, zero), jnp.where(lane >= HEAD_DIM, x, zero)], axis=0)


def _fold_pair(x):
    n = x.shape[0] // 2
    return x[:n] + x[n:]


def _split3(x):
    hi = x.astype(BF16)
    r1 = x - hi.astype(F32)
    mid = r1.astype(BF16)
    return hi, mid, (r1 - mid.astype(F32)).astype(BF16)


def _dot_split_rhs(a_bf, x):
    hi, mid, lo = _split3(x)
    return (jnp.dot(a_bf, hi, preferred_element_type=F32) + jnp.dot(a_bf, mid, preferred_element_type=F32)
            + jnp.dot(a_bf, lo, preferred_element_type=F32))


RWKV_PAIRS_PER_STEP = 4


def _rwkv_scan_kernel(r_ref, k_ref, v_ref, kkr_ref, a_ref, lw_ref, g_ref, rk_ref, lng_ref, lnb_ref,
                      o_ref, s_ref, *, n_chunks, n_pairs):
    L = RWKV_CHUNK
    L2 = 2 * L

    @pl.when(pl.program_id(2) == 0)
    def _():
        s_ref[...] = jnp.zeros_like(s_ref)

    ri = lax.broadcasted_iota(jnp.int32, (L, L), 0)
    ci = lax.broadcasted_iota(jnp.int32, (L, L), 1)
    tri_incl = jnp.where(ri >= ci, 1.0, 0.0).astype(BF16)
    r2 = lax.broadcasted_iota(jnp.int32, (L2, L2), 0)
    c2 = lax.broadcasted_iota(jnp.int32, (L2, L2), 1)
    same_head = (r2 // L) == (c2 // L)
    strict2 = same_head & (r2 > c2)
    incl2 = same_head & (r2 >= c2)
    eye2 = jnp.where(r2 == c2, 1.0, 0.0).astype(F32)
    h_r = lax.broadcasted_iota(jnp.int32, (LANE, LANE), 0) // HEAD_DIM
    h_c = lax.broadcasted_iota(jnp.int32, (LANE, LANE), 1) // HEAD_DIM
    head_mask = h_r == h_c
    head_ones = jnp.where(head_mask, 1.0, 0.0).astype(BF16)

    def head_sum(x):
        hi, mid, _ = _split3(x)
        return (jnp.dot(hi, head_ones, preferred_element_type=F32)
                + jnp.dot(mid, head_ones, preferred_element_type=F32))

    def mm(a, b):
        return jnp.dot(a.astype(BF16), b.astype(BF16), preferred_element_type=F32)

    chains = [(hp, c) for hp in range(n_pairs) for c in range(n_chunks)]
    lanes_of = lambda hp: slice(hp * LANE, (hp + 1) * LANE)
    rows_of = lambda c: pl.ds(c * L, L)
    each = lambda fn: {ch: fn(ch) for ch in chains}

    load = lambda ref: each(lambda ch: ref[0, rows_of(ch[1]), lanes_of(ch[0])])
    r, k, v, a = (each(lambda ch, d=d: d[ch].astype(F32)) for d in (load(r_ref), load(k_ref), load(v_ref), load(a_ref)))
    kkr = each(lambda ch, d=load(kkr_ref): d[ch].astype(F32))
    lw = load(lw_ref)
    kk_sq = each(lambda ch: head_sum(kkr[ch] * kkr[ch]))
    cl = each(lambda ch: _dot_split_rhs(tri_incl, lw[ch]))
    kk = each(lambda ch: kkr[ch] / jnp.maximum(jnp.sqrt(kk_sq[ch]), 1e-12))
    p_incl = each(lambda ch: jnp.exp(cl[ch]))
    p_inv = each(lambda ch: jnp.exp(-cl[ch]))
    rt = each(lambda ch: (r[ch] * p_incl[ch]).astype(BF16))
    at = each(lambda ch: (-kk[ch] * jnp.exp(cl[ch] - lw[ch])).astype(BF16))
    kt = each(lambda ch: k[ch] * p_inv[ch])
    bt = each(lambda ch: kk[ch] * a[ch] * p_inv[ch])
    gram = each(lambda ch: _dot_nt(
        jnp.concatenate([_pair_blocks(at[ch]), _pair_blocks(rt[ch])], axis=0),
        jnp.concatenate([_pair_blocks(bt[ch]), _pair_blocks(kt[ch])], axis=0).astype(BF16)))
    a_ab = each(lambda ch: jnp.where(strict2, gram[ch][:L2, :L2], 0.0))
    a_ak = each(lambda ch: jnp.where(strict2, gram[ch][:L2, L2:], 0.0).astype(BF16))
    a_rb = each(lambda ch: jnp.where(incl2, gram[ch][L2:, :L2], 0.0).astype(BF16))
    a_rk = each(lambda ch: jnp.where(incl2, gram[ch][L2:, L2:], 0.0).astype(BF16))
    tinv = each(lambda ch: eye2 + a_ab[ch])
    apow = a_ab
    for _ in range(int(np.log2(L)) - 1):
        apow = each(lambda ch: mm(apow[ch], apow[ch]))
        tinv = each(lambda ch: tinv[ch] + mm(tinv[ch], apow[ch]))
    tinv = each(lambda ch: tinv[ch].astype(BF16))
    v_bf = each(lambda ch: v[ch].astype(BF16))
    v2 = each(lambda ch: _pair_blocks(v_bf[ch]))
    akv = each(lambda ch: _fold_pair(jnp.dot(a_ak[ch], v2[ch], preferred_element_type=F32)))
    rkv = each(lambda ch: _fold_pair(jnp.dot(a_rk[ch], v2[ch], preferred_element_type=F32)))
    p_last = each(lambda ch: p_incl[ch][L - 1:L, :])
    wts = each(lambda ch: jnp.concatenate([bt[ch] * p_last[ch], kt[ch] * p_last[ch]], axis=0).astype(BF16))
    bonus = each(lambda ch: head_sum(r[ch] * k[ch] * rk_ref[:, lanes_of(ch[0])]) * v[ch])

    pairs = range(n_pairs)
    states = [s_ref[hp] for hp in pairs]
    for c in range(n_chunks):
        s_bf = [states[hp].astype(BF16) for hp in pairs]
        m = [_dot_nt(at[hp, c], s_bf[hp]) + akv[hp, c] for hp in pairs]
        y = [_dot_nt(rt[hp, c], s_bf[hp]) + rkv[hp, c] for hp in pairs]
        u = [_fold_pair(jnp.dot(tinv[hp, c], _pair_blocks(m[hp].astype(BF16)), preferred_element_type=F32))
             for hp in pairs]
        u_bf = [u[hp].astype(BF16) for hp in pairs]
        upd = [lax.dot_general(jnp.concatenate([u_bf[hp], v_bf[hp, c]], axis=0), wts[hp, c],
                               (((0,), (0,)), ((), ())), preferred_element_type=F32) for hp in pairs]
        states = [states[hp] * p_last[hp, c] + jnp.where(head_mask, upd[hp], 0.0) for hp in pairs]
        y = [y[hp] + _fold_pair(jnp.dot(a_rb[hp, c], _pair_blocks(u_bf[hp]), preferred_element_type=F32))
             for hp in pairs]
        mean = [head_sum(y[hp]) * (1.0 / HEAD_DIM) for hp in pairs]
        yc = [y[hp] - mean[hp] for hp in pairs]
        var = [head_sum(yc[hp] * yc[hp]) * (1.0 / HEAD_DIM) for hp in pairs]
        for hp in pairs:
            yn = yc[hp] * lax.rsqrt(var[hp] + GN_EPS) * lng_ref[:, lanes_of(hp)] + lnb_ref[:, lanes_of(hp)]
            o_ref[0, rows_of(c), lanes_of(hp)] = (
                (yn + bonus[hp, c]) * g_ref[0, rows_of(c), lanes_of(hp)].astype(F32)).astype(o_ref.dtype)
    for hp in pairs:
        s_ref[hp] = states[hp]


def _rwkv_scan(r, k, v, kkr, a, lw, g, r_k, ln_g, ln_b, tt=256):
    b, t_len, w = r.shape
    n_p = RWKV_PAIRS_PER_STEP
    width = n_p * LANE
    tile = pl.BlockSpec((1, tt, width), lambda bi, hp, ti: (bi, ti, hp))
    vec = pl.BlockSpec((1, width), lambda bi, hp, ti: (0, hp))
    kernel = functools.partial(_rwkv_scan_kernel, n_chunks=tt // RWKV_CHUNK, n_pairs=n_p)
    return pl.pallas_call(
        kernel,
        out_shape=jax.ShapeDtypeStruct((b, t_len, w), BF16),
        grid=(b, w // width, t_len // tt),
        in_specs=[tile] * 7 + [vec] * 3,
        out_specs=tile,
        scratch_shapes=[pltpu.VMEM((n_p, LANE, LANE), F32)],
        compiler_params=pltpu.CompilerParams(dimension_semantics=("parallel", "parallel", "arbitrary"),
                                             vmem_limit_bytes=VMEM_LIMIT),
        name="rwkv_scan",
    )(r, k, v, kkr, a, lw, g, r_k, ln_g, ln_b)


def _rwkv_time_mix(z3, mu, w0, w_up, a0, a_up, g_up, k_k, k_a, r_k, ln_g, ln_b):
    row = lambda u: u.reshape(1, -1)
    mu_p = row(_take_columns(mu, _rwkv_source_columns()))
    r, k, v, kkr, a, lw, g = _rwkv_prep(z3, mu_p, row(w0), _pad_rows(w_up, LANE).astype(BF16), row(a0),
                                        _pad_rows(a_up, LANE).astype(BF16), g_up.astype(BF16), row(k_k), row(k_a))
    return _rwkv_scan(r, k, v, kkr, a, lw, g, row(r_k), row(ln_g), row(ln_b))


def _layer_norm(x, g, b):
    mean = jnp.mean(x, axis=-1, keepdims=True)
    xc = x - mean
    var = jnp.mean(xc * xc, axis=-1, keepdims=True)
    return xc * lax.rsqrt(var + LN_EPS) * g + b


def _out_proj_kernel(on_ref, or_ref, x_ref, w_ref, g_ref, b_ref, rwh_ref, rwm_ref, rb_ref,
                     h_ref, idx_ref, wgt_ref, *, alpha):
    o = jnp.concatenate([on_ref[...], or_ref[...]], axis=1)
    mix = jnp.dot(o, w_ref[...], preferred_element_type=F32)
    h = _layer_norm(alpha * x_ref[...] + mix, g_ref[...], b_ref[...])
    h_ref[...] = h
    h_hi, h_mid, _ = _split3(h)
    logits = (jnp.dot(h_hi, rwh_ref[...], preferred_element_type=F32)
              + jnp.dot(h_hi, rwm_ref[...], preferred_element_type=F32)
              + jnp.dot(h_mid, rwh_ref[...], preferred_element_type=F32)) + rb_ref[...]
    lane = lax.broadcasted_iota(jnp.int32, logits.shape, 1)
    logits = jnp.where(lane < N_EXPERTS, logits, -jnp.inf)
    idx_out = jnp.zeros(logits.shape, jnp.int32)
    val_out = jnp.full(logits.shape, -jnp.inf, F32)
    for k in range(TOP_K):
        best = jnp.max(logits, axis=-1, keepdims=True)
        first = jnp.min(jnp.where(logits == best, lane, LANE), axis=-1, keepdims=True)
        idx_out = jnp.where(lane == k, first, idx_out)
        val_out = jnp.where(lane == k, best, val_out)
        logits = jnp.where(lane == first, -jnp.inf, logits)
    e = jnp.exp(val_out - jnp.max(val_out, axis=-1, keepdims=True))
    idx_ref[...] = idx_out
    wgt_ref[...] = e / jnp.sum(e, axis=-1, keepdims=True)


def _out_proj_router(o_nsa, o_rwkv, x2, w_out, ln_g, ln_b, router_w, router_b, alpha, tm=512):
    n_tok, d = x2.shape
    half = o_nsa.shape[1]
    rw = jnp.pad(router_w, ((0, 0), (0, LANE - N_EXPERTS)))
    rw_hi = rw.astype(BF16)
    rw_mid = (rw - rw_hi.astype(F32)).astype(BF16)
    rb = jnp.pad(router_b, (0, LANE - N_EXPERTS)).reshape(1, LANE)
    row_blk = lambda n: pl.BlockSpec((tm, n), lambda i: (i, 0))
    full = lambda m, n: pl.BlockSpec((m, n), lambda i: (0, 0))
    return pl.pallas_call(
        functools.partial(_out_proj_kernel, alpha=alpha),
        out_shape=(jax.ShapeDtypeStruct((n_tok, d), F32), jax.ShapeDtypeStruct((n_tok, LANE), jnp.int32),
                   jax.ShapeDtypeStruct((n_tok, LANE), F32)),
        grid=(n_tok // tm,),
        in_specs=[row_blk(half), row_blk(half), row_blk(d), full(2 * half, d), full(1, d), full(1, d),
                  full(d, LANE), full(d, LANE), full(1, LANE)],
        out_specs=(row_blk(d), row_blk(LANE), row_blk(LANE)),
        compiler_params=pltpu.CompilerParams(dimension_semantics=("parallel",), vmem_limit_bytes=VMEM_LIMIT),
        name="out_proj_router",
    )(o_nsa, o_rwkv, x2, w_out.astype(BF16), ln_g.reshape(1, d), ln_b.reshape(1, d), rw_hi, rw_mid, rb)


MOE_ITEM_ROWS = 1280
MOE_SUB_ROWS = 256
MOE_F_TILE = 256
MOE_TOKEN_TILE = 256


MOE_RANK_BLOCK = 512


def _moe_rank_kernel(e_ref, rank_ref, cnt_ref, carry_ref):
    @pl.when(pl.program_id(0) == 0)
    def _():
        carry_ref[...] = jnp.zeros_like(carry_ref)

    blk = e_ref.shape[0]
    lane = lax.broadcasted_iota(jnp.int32, (blk, LANE), 1)
    onehot = jnp.where(e_ref[...] == lane, 1.0, 0.0)
    tri = jnp.where(lax.broadcasted_iota(jnp.int32, (blk, blk), 0) >= lax.broadcasted_iota(jnp.int32, (blk, blk), 1),
                    1.0, 0.0).astype(BF16)
    csum = jnp.dot(tri, onehot.astype(BF16), preferred_element_type=F32)
    carry = carry_ref[...]
    rank_ref[...] = (jnp.sum((csum + carry) * onehot, axis=1, keepdims=True) - 1.0).astype(jnp.int32)
    carry_ref[...] = carry + csum[blk - 1:blk, :]
    cnt_ref[...] = carry_ref[...].astype(jnp.int32)


def _moe_rank(flat_e):
    n_assign = flat_e.shape[0]
    return pl.pallas_call(
        _moe_rank_kernel,
        out_shape=(jax.ShapeDtypeStruct((n_assign, 1), jnp.int32), jax.ShapeDtypeStruct((1, LANE), jnp.int32)),
        grid=(n_assign // MOE_RANK_BLOCK,),
        in_specs=[pl.BlockSpec((MOE_RANK_BLOCK, 1), lambda i: (i, 0))],
        out_specs=(pl.BlockSpec((MOE_RANK_BLOCK, 1), lambda i: (i, 0)), pl.BlockSpec((1, LANE), lambda i: (0, 0))),
        scratch_shapes=[pltpu.VMEM((1, LANE), F32)],
        compiler_params=pltpu.CompilerParams(dimension_semantics=("arbitrary",), vmem_limit_bytes=VMEM_LIMIT),
        name="moe_rank",
    )(flat_e.reshape(n_assign, 1))


def _moe_tables(top_idx, n_items):
    n_tok = top_idx.shape[0]
    flat_e = top_idx.reshape(-1)
    rank, counts = _moe_rank(flat_e)
    counts = counts[0, :N_EXPERTS]
    items_e = (counts + MOE_ITEM_ROWS - 1) // MOE_ITEM_ROWS
    items_end = jnp.cumsum(items_e)
    item_start_e = items_end - items_e
    dest = jnp.take(item_start_e * MOE_ITEM_ROWS, flat_e) + rank[:, 0]
    item = jnp.arange(n_items, dtype=jnp.int32)
    valid = item < items_end[-1]
    last_e = jnp.max(jnp.where(counts > 0, jnp.arange(N_EXPERTS, dtype=jnp.int32), 0))
    item_e = jnp.minimum(jnp.sum(items_end[None, :] <= item[:, None], axis=1).astype(jnp.int32), N_EXPERTS - 1)
    item_e = jnp.where(valid, item_e, last_e)
    item_nv = jnp.where(valid, jnp.clip(counts[item_e] - (item - item_start_e[item_e]) * MOE_ITEM_ROWS,
                                        0, MOE_ITEM_ROWS), 0).astype(jnp.int32)
    return dest.reshape(n_tok, TOP_K), item_e, item_nv


DISPATCH_ISSUE_UNROLL = 8


def _dispatch_kernel(nv_ref, dest_ref, h_ref, xg_ref, stage_ref, zero_ref, sem_ref, zsem_ref):
    tm = h_ref.shape[0]
    i = pl.program_id(0)
    slot = i % 2
    subs_per_item = MOE_ITEM_ROWS // MOE_SUB_ROWS

    @pl.when(i == 0)
    def _():
        zero_ref[...] = jnp.zeros_like(zero_ref)

        def fill(sub):
            return pltpu.make_async_copy(
                zero_ref, xg_ref.at[pl.ds(pl.multiple_of(sub * MOE_SUB_ROWS, MOE_SUB_ROWS), MOE_SUB_ROWS)], zsem_ref)

        def not_full(sub):
            return nv_ref[sub // subs_per_item] < (sub % subs_per_item + 1) * MOE_SUB_ROWS

        def start_body(sub, carry):
            @pl.when(not_full(sub))
            def _():
                fill(sub).start()
            return carry

        def wait_body(sub, carry):
            @pl.when(not_full(sub))
            def _():
                fill(sub).wait()
            return carry

        n_subs = xg_ref.shape[0] // MOE_SUB_ROWS
        lax.fori_loop(0, n_subs, start_body, 0)
        lax.fori_loop(0, n_subs, wait_body, 0)

    def wait_slot(s):
        for _ in range(TOP_K):
            pltpu.make_async_copy(stage_ref.at[s], xg_ref.at[pl.ds(0, tm)], sem_ref.at[s]).wait()

    @pl.when(i >= 2)
    def _():
        wait_slot(slot)

    stage_ref[slot] = h_ref[...]

    def body(j, carry):
        for u in range(DISPATCH_ISSUE_UNROLL):
            t = j * DISPATCH_ISSUE_UNROLL + u
            for k in range(TOP_K):
                pltpu.make_async_copy(stage_ref.at[slot, pl.ds(t, 1)], xg_ref.at[pl.ds(dest_ref[0, 0, k * tm + t], 1)],
                                      sem_ref.at[slot]).start(priority=k % 2)
        return carry

    lax.fori_loop(0, tm // DISPATCH_ISSUE_UNROLL, body, 0)

    @pl.when(i == pl.num_programs(0) - 1)
    def _():
        wait_slot(slot)

        @pl.when(i >= 1)
        def _():
            wait_slot(1 - slot)


def _dispatch(h1, dest_tiles, item_nv, n_rows, tm):
    n_tok, d = h1.shape
    return pl.pallas_call(
        _dispatch_kernel,
        out_shape=jax.ShapeDtypeStruct((n_rows, d), F32),
        grid=(n_tok // tm,),
        in_specs=[pl.BlockSpec(memory_space=pltpu.SMEM),
                  pl.BlockSpec((1, 1, TOP_K * tm), lambda i: (i, 0, 0), memory_space=pltpu.SMEM),
                  pl.BlockSpec((tm, d), lambda i: (i, 0))],
        out_specs=pl.BlockSpec(memory_space=pl.ANY),
        scratch_shapes=[pltpu.VMEM((2, tm, d), F32), pltpu.VMEM((MOE_SUB_ROWS, d), F32),
                        pltpu.SemaphoreType.DMA((2,)), pltpu.SemaphoreType.DMA(())],
        compiler_params=pltpu.CompilerParams(dimension_semantics=("arbitrary",), vmem_limit_bytes=VMEM_LIMIT),
        name="moe_dispatch",
    )(item_nv, dest_tiles, h1)


def _moe_kernel(item_e_ref, item_nv_ref, xg_ref, wg_ref, bg_ref, wu_ref, bu_ref, wd_ref, bd_ref, o_ref,
                stage_ref, xb_ref, sem_ref):
    i = pl.program_id(0)
    f = pl.program_id(1)
    nv = item_nv_ref[i]
    n_sub_max = MOE_ITEM_ROWS // MOE_SUB_ROWS
    sub_count = lambda rows_used: (rows_used + MOE_SUB_ROWS - 1) // MOE_SUB_ROWS
    n_sub = sub_count(nv)
    rows = [pl.ds(sb * MOE_SUB_ROWS, MOE_SUB_ROWS) for sb in range(n_sub_max)]
    cur = i % 2

    def fetch(item, sb):
        first = pl.multiple_of(item * MOE_ITEM_ROWS + sb * MOE_SUB_ROWS, MOE_SUB_ROWS)
        return pltpu.make_async_copy(xg_ref.at[pl.ds(first, MOE_SUB_ROWS)], stage_ref.at[sb % 2], sem_ref.at[sb % 2])

    @pl.when((i == 0) & (f == 0))
    def _():
        for sb in range(n_sub_max):
            @pl.when(sb < n_sub)
            def _(sb=sb):
                copy = fetch(0, sb)
                copy.start()
                copy.wait()
                xb_ref[0, rows[sb], :] = stage_ref[sb % 2].astype(BF16)

    has_next = i + 1 < pl.num_programs(0)
    n_sub_next = jnp.where(has_next, sub_count(item_nv_ref[jnp.minimum(i + 1, pl.num_programs(0) - 1)]), 0)
    for sb in range(n_sub_max):
        @pl.when((f == sb + 1) & (sb < n_sub_next))
        def _(sb=sb):
            fetch(i + 1, sb).wait()
            xb_ref[1 - cur, rows[sb], :] = stage_ref[sb % 2].astype(BF16)

        @pl.when((f == sb) & (sb < n_sub_next))
        def _(sb=sb):
            fetch(i + 1, sb).start()

    @pl.when(f == 0)
    def _():
        for sb in range(n_sub_max):
            bias = jnp.broadcast_to(bd_ref[0], (MOE_SUB_ROWS, o_ref.shape[1]))
            o_ref[rows[sb], :] = jnp.where(sb * MOE_SUB_ROWS < nv, bias, 0.0)

    for n in range(1, n_sub_max + 1):
        @pl.when(n_sub == n)
        def _(n=n):
            wg = wg_ref[0].astype(BF16)
            wu = wu_ref[0].astype(BF16)
            wd = wd_ref[0].astype(BF16)
            xs = [xb_ref[cur, rows[sb], :] for sb in range(n)]
            gates = [jnp.dot(xs[sb], wg, preferred_element_type=F32) for sb in range(n)]
            ups = [jnp.dot(xs[sb], wu, preferred_element_type=F32) for sb in range(n)]
            for sb in range(n):
                gate = jnp.minimum(gates[sb] + bg_ref[0], SWIGLU_LIMIT)
                up = jnp.clip(ups[sb] + bu_ref[0], -SWIGLU_LIMIT, SWIGLU_LIMIT)
                h = gate * jax.nn.sigmoid(SWIGLU_ALPHA * gate) * (up + 1.0)
                o_ref[rows[sb], :] += jnp.dot(h.astype(BF16), wd, preferred_element_type=F32)


def _moe_experts(xg, item_e, item_nv, w_gate, b_gate, w_up, b_up, w_down, b_down):
    n_rows = xg.shape[0]
    d = w_gate.shape[1]
    n_items = n_rows // MOE_ITEM_ROWS
    n_e, _, d_ff = w_gate.shape
    n_f = d_ff // MOE_F_TILE

    def f_idx(i, f, nv):
        return jnp.where(nv[i] > 0, f, n_f - 1)

    grid_spec = pltpu.PrefetchScalarGridSpec(
        num_scalar_prefetch=2,
        grid=(n_items, n_f),
        in_specs=[
            pl.BlockSpec(memory_space=pl.ANY),
            pl.BlockSpec((1, d, MOE_F_TILE), lambda i, f, e, nv: (e[i], 0, f_idx(i, f, nv))),
            pl.BlockSpec((1, 1, MOE_F_TILE), lambda i, f, e, nv: (e[i], 0, f_idx(i, f, nv))),
            pl.BlockSpec((1, d, MOE_F_TILE), lambda i, f, e, nv: (e[i], 0, f_idx(i, f, nv))),
            pl.BlockSpec((1, 1, MOE_F_TILE), lambda i, f, e, nv: (e[i], 0, f_idx(i, f, nv))),
            pl.BlockSpec((1, MOE_F_TILE, d), lambda i, f, e, nv: (e[i], f_idx(i, f, nv), 0)),
            pl.BlockSpec((1, 1, d), lambda i, f, e, nv: (e[i], 0, 0)),
        ],
        out_specs=pl.BlockSpec((MOE_ITEM_ROWS, d), lambda i, f, e, nv: (i, 0)),
        scratch_shapes=[pltpu.VMEM((2, MOE_SUB_ROWS, d), F32), pltpu.VMEM((2, MOE_ITEM_ROWS, d), BF16),
                        pltpu.SemaphoreType.DMA((2,))],
    )
    assert MOE_ITEM_ROWS // MOE_SUB_ROWS < n_f
    return pl.pallas_call(
        _moe_kernel,
        out_shape=jax.ShapeDtypeStruct((n_rows, d), F32),
        grid_spec=grid_spec,
        compiler_params=pltpu.CompilerParams(dimension_semantics=("arbitrary", "arbitrary"),
                                             vmem_limit_bytes=VMEM_LIMIT),
        name="moe_experts",
    )(item_e, item_nv, xg, w_gate, b_gate.reshape(n_e, 1, d_ff), w_up, b_up.reshape(n_e, 1, d_ff), w_down,
      b_down.reshape(n_e, 1, d))


COMBINE_ISSUE_UNROLL = 16


def _final_kernel(dest_ref, dest_next_ref, h_ref, eo_ref, wgt_ref, p_ref, g_ref, b_ref, wgate_ref, wple_ref,
                  o_ref, rows_ref, sem_ref, *, alpha):
    tm, d = h_ref.shape
    n_rows = TOP_K * tm
    i = pl.program_id(0)
    slot = i % 2

    def row_copy(idx_ref, r, s):
        return pltpu.make_async_copy(eo_ref.at[pl.ds(idx_ref[0, 0, r], 1)], rows_ref.at[s, pl.ds(r, 1)],
                                     sem_ref.at[s])

    def issue(idx_ref, s):
        def body(j, carry):
            for u in range(COMBINE_ISSUE_UNROLL):
                row_copy(idx_ref, j * COMBINE_ISSUE_UNROLL + u, s).start(priority=u % 2)
            return carry

        lax.fori_loop(0, n_rows // COMBINE_ISSUE_UNROLL, body, 0)

    @pl.when(i == 0)
    def _():
        issue(dest_ref, 0)

    @pl.when(i + 1 < pl.num_programs(0))
    def _():
        issue(dest_next_ref, 1 - slot)

    pltpu.make_async_copy(eo_ref.at[pl.ds(0, n_rows)], rows_ref.at[slot], sem_ref.at[slot]).wait()
    wgt = wgt_ref[...]
    ffn = wgt[:, 0:1] * rows_ref[slot, 0:tm, :]
    for k in range(1, TOP_K):
        ffn = ffn + wgt[:, k:k + 1] * rows_ref[slot, k * tm:(k + 1) * tm, :]
    h = _layer_norm(alpha * h_ref[...] + ffn, g_ref[...], b_ref[...])
    gate = jax.nn.sigmoid(jnp.dot(h.astype(BF16), wgate_ref[...], preferred_element_type=F32))
    ple = jnp.dot(p_ref[...].astype(BF16), wple_ref[...], preferred_element_type=F32)
    o_ref[...] = h + gate * ple


def _dest_tiles(dest, tm):
    n_tiles = dest.shape[0] // tm
    return dest.reshape(n_tiles, tm, TOP_K).transpose(0, 2, 1).reshape(n_tiles, 1, TOP_K * tm)


def _final(h1, eo, dest_tiles, top_w, p2, ln_g, ln_b, ple_gate_w, ple_w, alpha, tm):
    n_tok, d = h1.shape
    n_tiles = n_tok // tm
    row_blk = lambda n: pl.BlockSpec((tm, n), lambda i: (i, 0))
    full = lambda m, n: pl.BlockSpec((m, n), lambda i: (0, 0))
    idx_blk = lambda fn: pl.BlockSpec((1, 1, TOP_K * tm), fn, memory_space=pltpu.SMEM)
    return pl.pallas_call(
        functools.partial(_final_kernel, alpha=alpha),
        out_shape=jax.ShapeDtypeStruct((n_tok, d), F32),
        grid=(n_tiles,),
        in_specs=[idx_blk(lambda i: (i, 0, 0)), idx_blk(lambda i: (jnp.minimum(i + 1, n_tiles - 1), 0, 0)),
                  row_blk(d), pl.BlockSpec(memory_space=pl.ANY), row_blk(LANE), row_blk(p2.shape[1]),
                  full(1, d), full(1, d), full(d, d), full(p2.shape[1], d)],
        out_specs=row_blk(d),
        scratch_shapes=[pltpu.VMEM((2, TOP_K * tm, d), F32), pltpu.SemaphoreType.DMA((2,))],
        compiler_params=pltpu.CompilerParams(dimension_semantics=("arbitrary",), vmem_limit_bytes=VMEM_LIMIT),
        name="combine_ln_ple",
    )(dest_tiles, dest_tiles, h1, eo, top_w, p2, ln_g.reshape(1, d), ln_b.reshape(1, d),
      ple_gate_w.astype(BF16), ple_w.astype(BF16))


def kernel(x, p, w_in, cmp_pe_k, cmp_w1_k, cmp_w2_k, cmp_pe_v, cmp_w1_v, cmp_w2_v, rwkv_mu, rwkv_w0, rwkv_w_up, rwkv_a0, rwkv_a_up, rwkv_g_up, rwkv_k_k, rwkv_k_a, rwkv_r_k, rwkv_ln_g, rwkv_ln_b, w_out, ln1_g, ln1_b, router_w, router_b, exp_w_gate, exp_b_gate, exp_w_up, exp_b_up, exp_w_down, exp_b_down, ln2_g, ln2_b, ple_w, ple_gate_w):
    b, t_len, d = x.shape
    depth = w_in.shape[0]
    alpha = float((2 * depth) ** 0.25)
    n_tok = b * t_len
    n_items = (n_tok * TOP_K) // MOE_ITEM_ROWS + N_EXPERTS
    h = x.reshape(n_tok, d)
    for i in range(depth):
        w_bf = _take_columns(w_in[i].astype(BF16), _z_source_columns())
        z3 = _in_proj(h, w_bf).reshape(b, t_len, Z_COLS)
        pe, w1, w2 = _nsa_prepare(cmp_pe_k[i], cmp_w1_k[i], cmp_w2_k[i], cmp_pe_v[i], cmp_w1_v[i], cmp_w2_v[i])
        kv_cmp = _compress(z3, pe, w1, w2)
        o_nsa = _nsa_attention(z3, kv_cmp)
        o_rwkv = _rwkv_time_mix(z3, rwkv_mu[i], rwkv_w0[i], rwkv_w_up[i], rwkv_a0[i], rwkv_a_up[i], rwkv_g_up[i],
                                rwkv_k_k[i], rwkv_k_a[i], rwkv_r_k[i].reshape(-1), rwkv_ln_g[i], rwkv_ln_b[i])
        h1, top_idx, top_w = _out_proj_router(o_nsa.reshape(n_tok, -1), o_rwkv.reshape(n_tok, -1), h, w_out[i],
                                              ln1_g[i], ln1_b[i], router_w[i], router_b[i], alpha)
        dest, item_e, item_nv = _moe_tables(top_idx[:, :TOP_K], n_items)
        dest_tiles = _dest_tiles(dest, MOE_TOKEN_TILE)
        xg = _dispatch(h1, dest_tiles, item_nv, n_items * MOE_ITEM_ROWS, MOE_TOKEN_TILE)
        eo = _moe_experts(xg, item_e, item_nv, exp_w_gate[i], exp_b_gate[i], exp_w_up[i], exp_b_up[i],
                          exp_w_down[i], exp_b_down[i])
        h = _final(h1, eo, dest_tiles, top_w, p[i].reshape(n_tok, -1), ln2_g[i], ln2_b[i], ple_gate_w[i],
                   ple_w[i], alpha, MOE_TOKEN_TILE)
    return h.reshape(b, t_len, d)
```

```python
import functools

import numpy as np
import jax
import jax.numpy as jnp
from jax import lax
from jax.experimental import pallas as pl
from jax.experimental.pallas import tpu as pltpu

F32 = jnp.float32
BF16 = jnp.bfloat16

LANE = 128
D_MODEL = 2048
HEAD_DIM = 64
NSA_HEADS = 16
NSA_KV_HEADS = 4
NSA_GROUP = NSA_HEADS // NSA_KV_HEADS
NSA_WIDTH = NSA_HEADS * HEAD_DIM
NSA_KV_WIDTH = NSA_KV_HEADS * HEAD_DIM
CMP_BLOCK = 32
CMP_STRIDE = 16
SEL_BLOCK = 64
N_SELECT = 16
WINDOW = 512
N_GATES = 3
Q_BLOCK = 64
RWKV_HEADS = 16
RWKV_WIDTH = RWKV_HEADS * HEAD_DIM
DECAY_LORA = 96
ICLR_LORA = 96
GATE_LORA = 256
GN_EPS = 64e-5
N_EXPERTS = 32
TOP_K = 4
SWIGLU_ALPHA = 1.702
SWIGLU_LIMIT = 7.0
PLE_DIM = 256
LN_EPS = 1e-5
NEG_INF = -1e30

NSA_COLS = NSA_WIDTH + 6 * NSA_KV_WIDTH + NSA_HEADS * N_GATES
RWKV_COLS = 3 * RWKV_WIDTH + DECAY_LORA + ICLR_LORA + GATE_LORA

RB_R = 0
RB_K = RB_R + RWKV_WIDTH // LANE
RB_V = RB_K + RWKV_WIDTH // LANE
RB_WLO = RB_V + RWKV_WIDTH // LANE
RB_ALO = RB_WLO + 1
RB_GLO = RB_ALO + 1
RWKV_BLOCKS = RB_GLO + GATE_LORA // LANE
ZB_RWKV = 0
ZB_Q = ZB_RWKV + RWKV_BLOCKS
ZB_KVC = ZB_Q + NSA_WIDTH // LANE
ZB_KVS = ZB_KVC + NSA_KV_HEADS
ZB_KVW = ZB_KVS + NSA_KV_HEADS
ZB_GATE = ZB_KVW + NSA_KV_HEADS
Z_BLOCKS = ZB_GATE + NSA_KV_HEADS
Z_COLS = Z_BLOCKS * LANE
assert (ZB_Q * LANE) % (NSA_GROUP * HEAD_DIM) == 0

VMEM_LIMIT = 56 * 1024 * 1024


def _z_source_columns():
    src = np.full((Z_COLS,), -1, np.int64)
    src[ZB_Q * LANE:ZB_Q * LANE + NSA_WIDTH] = np.arange(NSA_WIDTH)
    for branch in range(3):
        k0 = NSA_WIDTH + 2 * branch * NSA_KV_WIDTH
        v0 = k0 + NSA_KV_WIDTH
        for g in range(NSA_KV_HEADS):
            base = (ZB_KVC + branch * NSA_KV_HEADS + g) * LANE
            src[base:base + HEAD_DIM] = k0 + g * HEAD_DIM + np.arange(HEAD_DIM)
            src[base + HEAD_DIM:base + 2 * HEAD_DIM] = v0 + g * HEAD_DIM + np.arange(HEAD_DIM)
    g0 = NSA_WIDTH + 6 * NSA_KV_WIDTH
    for g in range(NSA_KV_HEADS):
        base = (ZB_GATE + g) * LANE
        for i in range(N_GATES):
            for r in range(NSA_GROUP):
                src[base + i * NSA_GROUP + r] = g0 + (g * NSA_GROUP + r) * N_GATES + i
    rwkv = _rwkv_source_columns()
    src[ZB_RWKV * LANE:(ZB_RWKV + RWKV_BLOCKS) * LANE] = np.where(rwkv >= 0, NSA_COLS + rwkv, -1)
    return src


def _rwkv_source_columns():
    src = np.full((RWKV_BLOCKS * LANE,), -1, np.int64)
    src[:3 * RWKV_WIDTH] = np.arange(3 * RWKV_WIDTH)
    src[RB_WLO * LANE:RB_WLO * LANE + DECAY_LORA] = 3 * RWKV_WIDTH + np.arange(DECAY_LORA)
    src[RB_ALO * LANE:RB_ALO * LANE + ICLR_LORA] = 3 * RWKV_WIDTH + DECAY_LORA + np.arange(ICLR_LORA)
    src[RB_GLO * LANE:] = 3 * RWKV_WIDTH + DECAY_LORA + ICLR_LORA + np.arange(GATE_LORA)
    return src


def _take_columns(w, src):
    pieces, i, n = [], 0, len(src)
    while i < n:
        j = i + 1
        if src[i] < 0:
            while j < n and src[j] < 0:
                j += 1
            pieces.append(jnp.zeros(w.shape[:-1] + (j - i,), w.dtype))
        else:
            while j < n and src[j] == src[i] + (j - i):
                j += 1
            pieces.append(w[..., int(src[i]):int(src[i]) + j - i])
        i = j
    return jnp.concatenate(pieces, axis=-1)


def _pad_rows(w, rows):
    return jnp.pad(w, ((0, rows - w.shape[0]), (0, 0)))


def _in_proj_kernel(x_ref, w_ref, z_ref, xb_ref):
    @pl.when(pl.program_id(1) == 0)
    def _():
        xb_ref[...] = x_ref[...].astype(BF16)

    z_ref[...] = jnp.dot(xb_ref[...], w_ref[...], preferred_element_type=F32).astype(z_ref.dtype)


def _in_proj(x2, w_bf, tm=1024, tn=1664):
    n_tok, d = x2.shape
    n_cols = w_bf.shape[1]
    return pl.pallas_call(
        _in_proj_kernel,
        out_shape=jax.ShapeDtypeStruct((n_tok, n_cols), BF16),
        grid=(n_tok // tm, n_cols // tn),
        in_specs=[pl.BlockSpec((tm, d), lambda i, j: (i, 0)),
                  pl.BlockSpec((d, tn), lambda i, j: (0, j))],
        out_specs=pl.BlockSpec((tm, tn), lambda i, j: (i, j)),
        scratch_shapes=[pltpu.VMEM((tm, d), BF16)],
        compiler_params=pltpu.CompilerParams(dimension_semantics=("parallel", "arbitrary"),
                                             vmem_limit_bytes=VMEM_LIMIT),
        name="in_proj",
    )(x2, w_bf)


def _compress_kernel(kv_ref, pe_ref, w1_ref, w2_ref, out_ref, kv32_ref):
    n_chunks = kv_ref.shape[1] // CMP_STRIDE
    kv32_ref[...] = kv_ref[0].astype(F32)
    acc_lo = jnp.zeros((n_chunks, LANE), F32)
    acc_hi = jnp.zeros((n_chunks, LANE), F32)
    for i in range(CMP_STRIDE):
        rows = kv32_ref[pl.ds(i, n_chunks, stride=CMP_STRIDE), :]
        lo = (rows + pe_ref[i:i + 1, :]).astype(BF16)
        hi = (rows + pe_ref[CMP_STRIDE + i:CMP_STRIDE + i + 1, :]).astype(BF16)
        acc_lo += jnp.dot(lo, w1_ref[i], preferred_element_type=F32)
        acc_hi += jnp.dot(hi, w1_ref[CMP_STRIDE + i], preferred_element_type=F32)
    shifted = jnp.concatenate([acc_hi[1:], jnp.zeros((1, LANE), F32)], axis=0)
    hid = jax.nn.gelu(acc_lo + shifted)
    out = jnp.dot(hid.astype(BF16), w2_ref[...], preferred_element_type=F32)
    row = lax.broadcasted_iota(jnp.int32, out.shape, 0)
    out_ref[0, 0] = jnp.where(row < n_chunks - 1, out, 0.0).astype(out_ref.dtype)


def _compress(z3, pe, w1, w2):
    b, t_len, _ = z3.shape
    n_chunks = t_len // CMP_STRIDE
    return pl.pallas_call(
        _compress_kernel,
        out_shape=jax.ShapeDtypeStruct((b, NSA_KV_HEADS, n_chunks, LANE), BF16),
        grid=(b, NSA_KV_HEADS),
        in_specs=[pl.BlockSpec((1, t_len, LANE), lambda bi, g: (bi, 0, ZB_KVC + g)),
                  pl.BlockSpec((CMP_BLOCK, LANE), lambda bi, g: (0, 0)),
                  pl.BlockSpec((CMP_BLOCK, LANE, LANE), lambda bi, g: (0, 0, 0)),
                  pl.BlockSpec((LANE, LANE), lambda bi, g: (0, 0))],
        out_specs=pl.BlockSpec((1, 1, n_chunks, LANE), lambda bi, g: (bi, g, 0, 0)),
        scratch_shapes=[pltpu.VMEM((t_len, LANE), F32)],
        compiler_params=pltpu.CompilerParams(dimension_semantics=("parallel", "parallel"),
                                             vmem_limit_bytes=VMEM_LIMIT),
        name="kv_compress",
    )(z3, pe, w1, w2)


def _block_diag2(a, b):
    za = jnp.zeros(a.shape[:-1] + (b.shape[-1],), a.dtype)
    zb = jnp.zeros(b.shape[:-1] + (a.shape[-1],), b.dtype)
    return jnp.concatenate([jnp.concatenate([a, za], axis=-1), jnp.concatenate([zb, b], axis=-1)], axis=-2)


def _stack_heads(x):
    return jnp.concatenate([x[:, r * HEAD_DIM:(r + 1) * HEAD_DIM] for r in range(NSA_GROUP)], axis=0)


def _dot_nt(a, b):
    return lax.dot_general(a, b, (((1,), (1,)), ((), ())), preferred_element_type=F32)


def _eye(n, dtype):
    return jnp.where(lax.broadcasted_iota(jnp.int32, (n, n), 0) == lax.broadcasted_iota(jnp.int32, (n, n), 1),
                     1.0, 0.0).astype(dtype)


NSA_KEY_CHUNK = 256
NSA_V_ROWS = HEAD_DIM + 16


def _nsa_kernel(slopes_ref, *refs, n_sel, n_g):
    q_refs = refs[:n_g]
    (kvc_ref, kvs_ref, kvw_ref, gate_ref, ovl_ref, o_ref, vst_ref, vwt_ref, vct_ref, sel_ref,
     sc0_ref, sc1_ref, p0_ref, p1_ref) = refs[n_g:]
    g_base = pl.program_id(1) * n_g
    qi = pl.program_id(2)
    q0 = qi * Q_BLOCK
    cols = NSA_GROUP * Q_BLOCK
    kc_len = NSA_KEY_CHUNK
    blocks_per_chunk = kc_len // SEL_BLOCK
    n_chunks = kvs_ref.shape[1] // kc_len
    n_cmp_pad = kvc_ref.shape[2]
    n_slc = ovl_ref.shape[0]
    v_rows = vst_ref.shape[2]
    eye_dh = _eye(HEAD_DIM, BF16)
    k_lanes = lambda gg: slice(gg * LANE, gg * LANE + HEAD_DIM)
    v_lanes = lambda gg: slice(gg * LANE + HEAD_DIM, (gg + 1) * LANE)

    @pl.when(qi == 0)
    def _():
        ones_row = jnp.where(lax.broadcasted_iota(jnp.int32, (v_rows - HEAD_DIM, kc_len), 0) == 0,
                             1.0, 0.0).astype(BF16)

        def body(c, carry):
            rows = pl.ds(pl.multiple_of(c * kc_len, kc_len), kc_len)
            for gg in range(n_g):
                vst_ref[gg, c, 0:HEAD_DIM] = _dot_nt(eye_dh, kvs_ref[0, rows, v_lanes(gg)]).astype(BF16)
                vwt_ref[gg, c, 0:HEAD_DIM] = _dot_nt(eye_dh, kvw_ref[0, rows, v_lanes(gg)]).astype(BF16)
                vst_ref[gg, c, HEAD_DIM:v_rows] = ones_row
                vwt_ref[gg, c, HEAD_DIM:v_rows] = ones_row
            return carry

        lax.fori_loop(0, n_chunks, body, 0)
        for gg in range(n_g):
            vct_ref[gg] = _dot_nt(eye_dh, kvc_ref[0, gg, :, HEAD_DIM:2 * HEAD_DIM]).astype(BF16)

    log2e = float(np.log2(np.e))
    lane = lax.broadcasted_iota(jnp.int32, (1, cols), 1)
    head = lane // Q_BLOCK
    tq = q0 + lane % Q_BLOCK
    sub = lax.broadcasted_iota(jnp.int32, (kc_len, cols), 0)
    half = lax.broadcasted_iota(jnp.int32, (1, LANE), 1) // Q_BLOCK
    blk = lax.broadcasted_iota(jnp.int32, (n_slc, LANE), 0)
    blk8 = lax.broadcasted_iota(jnp.int32, (8, LANE), 0)
    cmp_end = lax.broadcasted_iota(jnp.int32, (n_cmp_pad, 1), 0) * CMP_STRIDE + (CMP_BLOCK - 1)
    d_cmp = tq - cmp_end
    m_cmp = d_cmp >= 0
    any_cmp = tq >= CMP_BLOCK - 1
    d_cmp_f = d_cmp.astype(F32)
    forced = (blk == 0) | (blk == qi) | (blk == qi - 1)
    ovl = ovl_ref[...]

    groups = range(n_g)
    qs, slopes = [], []
    for gg in groups:
        q = _stack_heads(q_refs[gg][0]).astype(F32)
        qs.append((q * (HEAD_DIM ** -0.5 * log2e)).astype(BF16))
        slope = jnp.zeros((1, cols), F32)
        for r in range(NSA_GROUP):
            slope = jnp.where(head == r, slopes_ref[(g_base + gg) * NSA_GROUP + r] * log2e, slope)
        slopes.append(slope)
    st = [_dot_nt(kvc_ref[0, gg, :, 0:HEAD_DIM], qs[gg]) for gg in groups]
    p_cmp = []
    for gg in groups:
        s_m = jnp.where(m_cmp, st[gg] - slopes[gg] * d_cmp_f, NEG_INF)
        e = jnp.exp2(s_m - jnp.max(s_m, axis=0, keepdims=True))
        inv = jnp.where(any_cmp, 1.0 / jnp.maximum(jnp.sum(e, axis=0, keepdims=True), 1e-30), 0.0)
        p_cmp.append(e * inv)
    o_cmp = [jnp.dot(vct_ref[gg], p_cmp[gg].astype(BF16), preferred_element_type=F32) for gg in groups]
    parts = []
    for gg in groups:
        y = p_cmp[gg][:, 0:LANE] + p_cmp[gg][:, LANE:2 * LANE]
        parts.append(_split3(y + pltpu.roll(y, Q_BLOCK, 1)))
    pooled = [[jnp.dot(ovl, parts[gg][i], preferred_element_type=F32) for gg in groups] for i in range(3)]
    for gg in groups:
        imp = pooled[0][gg] + pooled[1][gg] + pooled[2][gg]
        imp = jnp.where(forced, jnp.inf, jnp.where(blk > qi, -jnp.inf, imp))
        tiles = [imp[t:t + 8] for t in range(0, n_slc, 8)]
        ranks = [jnp.zeros((8, LANE), F32) for _ in tiles]
        for jp in range(0, n_slc, 2):
            row = jnp.where(half == 0, imp[jp:jp + 1, :], imp[jp + 1:jp + 2, :])
            for ti, tile in enumerate(tiles):
                t0 = ti * 8
                if t0 > jp + 1:
                    hit = jnp.where(row >= tile, 1.0, 0.0)
                elif t0 + 7 <= jp:
                    hit = jnp.where(row > tile, 1.0, 0.0)
                else:
                    hit = jnp.where(blk8 + t0 > jp + half, jnp.where(row >= tile, 1.0, 0.0),
                                    jnp.where(row > tile, 1.0, 0.0))
                ranks[ti] = ranks[ti] + hit
        rank = jnp.concatenate(ranks, axis=0)
        rank = rank + pltpu.roll(rank, Q_BLOCK, 1)
        neg = jnp.where((rank < n_sel) & (blk <= qi), 0.0, NEG_INF)
        neg2 = jnp.concatenate([neg, neg], axis=1)
        for j in range(n_slc):
            sel_ref[gg, j] = jnp.broadcast_to(neg2[j:j + 1, :], (8, cols))
    sub_f = sub.astype(F32)
    fronts = [(qs[gg], slopes[gg], slopes[gg] * sub_f, o_cmp[gg]) for gg in groups]

    init1 = (jnp.full((1, cols), NEG_INF, F32), jnp.zeros((v_rows, cols), F32))

    def block_mask(gg, c):
        tiles = [sel_ref[gg, c * blocks_per_chunk + i] for i in range(blocks_per_chunk)]
        return jnp.concatenate([t for t in tiles for _ in range(SEL_BLOCK // 8)], axis=0)

    streams = {"slc": (kvs_ref, vst_ref), "win": (kvw_ref, vwt_ref)}
    sc_slots = (sc0_ref, sc1_ref)
    p_slots = (p0_ref, p1_ref)

    def score(gg, pos, slot):
        stream, c, _ = pos
        rows = pl.ds(pl.multiple_of(c * kc_len, kc_len), kc_len)
        sc_slots[slot][gg] = _dot_nt(streams[stream][0][0, rows, k_lanes(gg)], fronts[gg][0])

    def softmax(gg, pos, slot, m_prev):
        _, c, mask_fn = pos
        _, slope, bias_local, _ = fronts[gg]
        sc = sc_slots[slot][gg] + bias_local + mask_fn(gg)
        shift = slope * (c * kc_len - q0).astype(F32)
        m_new = jnp.maximum(m_prev, jnp.max(sc, axis=0, keepdims=True) + shift)
        p_slots[slot][gg] = jnp.exp2(sc - (m_new - shift)).astype(BF16)
        return m_new, jnp.exp2(m_prev - m_new)

    def weigh(gg, pos, slot, alpha, acc):
        stream, c, _ = pos
        return alpha * acc + jnp.dot(streams[stream][1][gg, c], p_slots[slot][gg], preferred_element_type=F32)

    def step(pos, slot, nxt, prev, state, alphas):
        if nxt is not None:
            for gg in range(n_g):
                score(gg, nxt, 1 - slot)
        state = {s: list(v) for s, v in state.items()}
        new_alphas = []
        for gg in range(n_g):
            m_prev, acc = state[pos[0]][gg]
            m_new, alpha = softmax(gg, pos, slot, m_prev)
            state[pos[0]][gg] = (m_new, acc)
            new_alphas.append(alpha)
            if prev is not None:
                m_p, acc_p = state[prev[0]][gg]
                state[prev[0]][gg] = (m_p, weigh(gg, prev, 1 - slot, alphas[gg], acc_p))
        return state, new_alphas

    c_cur = qi // blocks_per_chunk
    n_pairs = c_cur // 2
    plain = lambda c: ("slc", c, lambda gg: block_mask(gg, c))

    for gg in range(n_g):
        p1_ref[gg] = jnp.zeros((kc_len, cols), BF16)
        score(gg, plain(0), 0)

    def pair_body(i, carry):
        slc_state, alphas = carry
        k0 = 2 * i
        state, alphas = step(plain(k0), 0, plain(k0 + 1), plain(jnp.maximum(k0 - 1, 0)),
                             {"slc": slc_state}, alphas)
        state, alphas = step(plain(k0 + 1), 1, plain(k0 + 2), plain(k0), state, alphas)
        return tuple(state["slc"]), tuple(alphas)

    slc_state, alphas = lax.fori_loop(0, n_pairs, pair_body,
                                      (tuple(init1 for _ in range(n_g)),
                                       tuple(jnp.ones((1, cols), F32) for _ in range(n_g))))
    k_t = 2 * n_pairs
    odd_neg = jnp.where(c_cur % 2 == 1, 0.0, NEG_INF)
    causal_neg = jnp.where((c_cur * kc_len + sub) <= tq, 0.0, NEG_INF)
    tail = [("slc", k_t, lambda gg: block_mask(gg, k_t) + odd_neg),
            ("slc", c_cur, lambda gg: block_mask(gg, c_cur) + causal_neg)]
    for i in range(WINDOW // kc_len, -1, -1):
        c_raw = c_cur - i
        c_win = jnp.maximum(c_raw, 0)
        dist = tq - (c_win * kc_len + sub)
        band_neg = jnp.where((dist >= 0) & (dist < WINDOW) & (c_raw >= 0), 0.0, NEG_INF)
        tail.append(("win", c_win, lambda gg, band_neg=band_neg: band_neg))
    state = {"slc": list(slc_state), "win": [init1 for _ in range(n_g)]}
    prev = plain(jnp.maximum(k_t - 1, 0))
    for idx, pos in enumerate(tail):
        nxt = tail[idx + 1] if idx + 1 < len(tail) else None
        state, alphas = step(pos, idx % 2, nxt, prev, state, alphas)
        prev = pos
    last_slot = (len(tail) - 1) % 2
    for gg in range(n_g):
        m_p, acc_p = state[prev[0]][gg]
        state[prev[0]][gg] = (m_p, weigh(gg, prev, last_slot, alphas[gg], acc_p))

    eye_q = _eye(Q_BLOCK, BF16)
    eye_lane = _eye(LANE, BF16)
    gates = [jax.nn.sigmoid(_dot_nt(eye_lane, gate_ref[0, :, gg * LANE:(gg + 1) * LANE])) for gg in groups]

    def gate_row(gg, i):
        return jnp.concatenate([gates[gg][i * NSA_GROUP + r:i * NSA_GROUP + r + 1, :] for r in range(NSA_GROUP)],
                               axis=1)

    def normalized(acc):
        return acc[0:HEAD_DIM] / jnp.maximum(acc[HEAD_DIM:HEAD_DIM + 1], 1e-30)

    o_rows = []
    for gg in groups:
        o = (gate_row(gg, 0) * fronts[gg][3] + gate_row(gg, 1) * normalized(state["slc"][gg][1])
             + gate_row(gg, 2) * normalized(state["win"][gg][1])).astype(BF16)
        o_rows.append(jnp.concatenate([o[:, r * Q_BLOCK:(r + 1) * Q_BLOCK] for r in range(NSA_GROUP)], axis=0))
    outs = [_dot_nt(eye_q, o_rows[gg]) for gg in groups]
    for gg in groups:
        o_ref[0, :, gg * cols:(gg + 1) * cols] = outs[gg].astype(o_ref.dtype)


NSA_GROUPS_PER_STEP = 4


def _nsa_attention(z3, kv_cmp):
    b, t_len, _ = z3.shape
    n_q = t_len // Q_BLOCK
    n_slc = t_len // SEL_BLOCK
    n_cmp_pad = kv_cmp.shape[2]
    n_sel = min(N_SELECT, n_slc)
    n_g = NSA_GROUPS_PER_STEP
    cols = NSA_GROUP * Q_BLOCK
    slopes = jnp.exp2(-8.0 * jnp.arange(1, NSA_HEADS + 1, dtype=F32) / NSA_HEADS)
    c_idx = np.arange(n_cmp_pad)
    cmp_start = c_idx * CMP_STRIDE
    cmp_end = cmp_start + CMP_BLOCK - 1
    slc_start = np.arange(n_slc) * SEL_BLOCK
    overlap = np.clip(np.minimum(cmp_end[None, :], slc_start[:, None] + SEL_BLOCK - 1)
                      - np.maximum(cmp_start[None, :], slc_start[:, None]) + 1, 0, None).astype(np.float32)
    overlap[:, c_idx >= t_len // CMP_STRIDE - CMP_BLOCK // CMP_STRIDE + 1] = 0.0
    kernel = functools.partial(_nsa_kernel, n_sel=n_sel, n_g=n_g)
    n_kc = t_len // NSA_KEY_CHUNK
    q_spec = lambda gg: pl.BlockSpec((1, Q_BLOCK, cols),
                                     lambda bi, g, qi: (bi, qi, ZB_Q * LANE // cols + g * n_g + gg))
    slab = lambda zb: pl.BlockSpec((1, t_len, n_g * LANE), lambda bi, g, qi: (bi, 0, zb // n_g + g))
    assert ZB_KVS % n_g == 0 and ZB_KVW % n_g == 0 and ZB_GATE % n_g == 0
    return pl.pallas_call(
        kernel,
        out_shape=jax.ShapeDtypeStruct((b, t_len, NSA_WIDTH), BF16),
        grid=(b, NSA_KV_HEADS // n_g, n_q),
        in_specs=[pl.BlockSpec(memory_space=pltpu.SMEM)] + [q_spec(gg) for gg in range(n_g)] + [
            pl.BlockSpec((1, n_g, n_cmp_pad, LANE), lambda bi, g, qi: (bi, g, 0, 0)),
            slab(ZB_KVS), slab(ZB_KVW),
            pl.BlockSpec((1, Q_BLOCK, n_g * LANE), lambda bi, g, qi: (bi, qi, ZB_GATE // n_g + g)),
            pl.BlockSpec((n_slc, n_cmp_pad), lambda bi, g, qi: (0, 0)),
        ],
        out_specs=pl.BlockSpec((1, Q_BLOCK, n_g * cols), lambda bi, g, qi: (bi, qi, g)),
        scratch_shapes=[pltpu.VMEM((n_g, n_kc, NSA_V_ROWS, NSA_KEY_CHUNK), BF16),
                        pltpu.VMEM((n_g, n_kc, NSA_V_ROWS, NSA_KEY_CHUNK), BF16),
                        pltpu.VMEM((n_g, HEAD_DIM, n_cmp_pad), BF16),
                        pltpu.VMEM((n_g, n_slc, 8, cols), F32),
                        pltpu.VMEM((n_g, NSA_KEY_CHUNK, cols), F32),
                        pltpu.VMEM((n_g, NSA_KEY_CHUNK, cols), F32),
                        pltpu.VMEM((n_g, NSA_KEY_CHUNK, cols), BF16),
                        pltpu.VMEM((n_g, NSA_KEY_CHUNK, cols), BF16)],
        compiler_params=pltpu.CompilerParams(dimension_semantics=("parallel", "parallel", "arbitrary"),
                                             vmem_limit_bytes=VMEM_LIMIT),
        name="nsa_attention",
    )(slopes, *([z3] * n_g), kv_cmp, z3, z3, z3, jnp.asarray(overlap, BF16))


def _nsa_prepare(cmp_pe_k, cmp_w1_k, cmp_w2_k, cmp_pe_v, cmp_w1_v, cmp_w2_v):
    pe = jnp.concatenate([cmp_pe_k, cmp_pe_v], axis=-1)
    w1 = _block_diag2(cmp_w1_k.reshape(CMP_BLOCK, HEAD_DIM, HEAD_DIM),
                      cmp_w1_v.reshape(CMP_BLOCK, HEAD_DIM, HEAD_DIM)).astype(BF16)
    w2 = _block_diag2(cmp_w2_k, cmp_w2_v).astype(BF16)
    return pe, w1, w2


RWKV_CHUNK = 64


def _rwkv_prep_kernel(z_ref, mu_ref, w0_ref, wup_ref, a0_ref, aup_ref, gup_ref, kk_ref, ka_ref,
                      r_ref, k_ref, v_ref, kkr_ref, a_ref, lw_ref, g_ref, carry_ref):
    w = RWKV_WIDTH

    @pl.when(pl.program_id(1) == 0)
    def _():
        carry_ref[...] = jnp.zeros_like(carry_ref)

    z = z_ref[0].astype(F32)
    tc = z.shape[0]
    row = lax.broadcasted_iota(jnp.int32, z.shape, 0)
    prev = jnp.where(row == 0, carry_ref[0:1, :], pltpu.roll(z, 1, 0))
    carry_ref[0:1, :] = z[tc - 1:tc, :]
    zs = z + (prev - z) * mu_ref[...]
    r = zs[:, RB_R * LANE:RB_R * LANE + w]
    k = zs[:, RB_K * LANE:RB_K * LANE + w]
    v = zs[:, RB_V * LANE:RB_V * LANE + w]
    w_lo = zs[:, RB_WLO * LANE:(RB_WLO + 1) * LANE]
    a_lo = zs[:, RB_ALO * LANE:(RB_ALO + 1) * LANE]
    g_lo = zs[:, RB_GLO * LANE:RB_GLO * LANE + GATE_LORA]
    d = w0_ref[...] + jnp.dot(jnp.tanh(w_lo).astype(BF16), wup_ref[...], preferred_element_type=F32)
    w_raw = -jax.nn.softplus(-d) - 0.5
    lw_ref[0] = -jnp.exp(w_raw)
    a = jax.nn.sigmoid(a0_ref[...] + jnp.dot(a_lo.astype(BF16), aup_ref[...], preferred_element_type=F32))
    g = jnp.dot(jax.nn.sigmoid(g_lo).astype(BF16), gup_ref[...], preferred_element_type=F32)
    r_ref[0] = r.astype(r_ref.dtype)
    v_ref[0] = v.astype(v_ref.dtype)
    kkr_ref[0] = (k * kk_ref[...]).astype(kkr_ref.dtype)
    k_ref[0] = (k * (1.0 + (a - 1.0) * ka_ref[...])).astype(k_ref.dtype)
    a_ref[0] = a.astype(a_ref.dtype)
    g_ref[0] = g.astype(g_ref.dtype)


def _rwkv_prep(z3, mu, w0, w_up, a0, a_up, g_up, k_k, k_a, tc=256):
    b, t_len, _ = z3.shape
    w = RWKV_WIDTH
    ncol = RWKV_BLOCKS * LANE
    vec = lambda n: pl.BlockSpec((1, n), lambda bi, ti: (0, 0))
    mat = lambda m, n: pl.BlockSpec((m, n), lambda bi, ti: (0, 0))
    out_bf = jax.ShapeDtypeStruct((b, t_len, w), BF16)
    out_f32 = jax.ShapeDtypeStruct((b, t_len, w), F32)
    out_spec = pl.BlockSpec((1, tc, w), lambda bi, ti: (bi, ti, 0))
    return pl.pallas_call(
        _rwkv_prep_kernel,
        out_shape=(out_bf, out_bf, out_bf, out_bf, out_bf, out_f32, out_bf),
        grid=(b, t_len // tc),
        in_specs=[pl.BlockSpec((1, tc, ncol), lambda bi, ti: (bi, ti, ZB_RWKV)),
                  vec(ncol), vec(w), mat(LANE, w), vec(w), mat(LANE, w), mat(GATE_LORA, w), vec(w), vec(w)],
        out_specs=(out_spec,) * 7,
        scratch_shapes=[pltpu.VMEM((8, ncol), F32)],
        compiler_params=pltpu.CompilerParams(dimension_semantics=("parallel", "arbitrary"),
                                             vmem_limit_bytes=VMEM_LIMIT),
        name="rwkv_prep",
    )(z3, mu, w0, w_up, a0, a_up, g_up, k_k, k_a)


def _pair_blocks(x):
    lane = lax.broadcasted_iota(jnp.int32, x.shape, 1)
    zero = jnp.zeros((), x.dtype)
    return jnp.concatenate([jnp.where(lane < HEAD_DIM, x, zero), jnp.where(lane >= HEAD_DIM, x, zero)], axis=0)


def _fold_pair(x):
    n = x.shape[0] // 2
    return x[:n] + x[n:]


def _split3(x):
    hi = x.astype(BF16)
    r1 = x - hi.astype(F32)
    mid = r1.astype(BF16)
    return hi, mid, (r1 - mid.astype(F32)).astype(BF16)


def _dot_split_rhs(a_bf, x):
    hi, mid, lo = _split3(x)
    return (jnp.dot(a_bf, hi, preferred_element_type=F32) + jnp.dot(a_bf, mid, preferred_element_type=F32)
            + jnp.dot(a_bf, lo, preferred_element_type=F32))


RWKV_PAIRS_PER_STEP = 8


def _rwkv_scan_kernel(r_ref, k_ref, v_ref, kkr_ref, a_ref, lw_ref, g_ref, rk_ref, lng_ref, lnb_ref,
                      o_ref, s_ref, *, n_chunks, n_pairs):
    L = RWKV_CHUNK
    L2 = 2 * L

    @pl.when(pl.program_id(2) == 0)
    def _():
        s_ref[...] = jnp.zeros_like(s_ref)

    ri = lax.broadcasted_iota(jnp.int32, (L, L), 0)
    ci = lax.broadcasted_iota(jnp.int32, (L, L), 1)
    tri_incl = jnp.where(ri >= ci, 1.0, 0.0).astype(BF16)
    r2 = lax.broadcasted_iota(jnp.int32, (L2, L2), 0)
    c2 = lax.broadcasted_iota(jnp.int32, (L2, L2), 1)
    same_head = (r2 // L) == (c2 // L)
    strict2 = same_head & (r2 > c2)
    incl2 = same_head & (r2 >= c2)
    eye2 = jnp.where(r2 == c2, 1.0, 0.0).astype(F32)
    h_r = lax.broadcasted_iota(jnp.int32, (LANE, LANE), 0) // HEAD_DIM
    h_c = lax.broadcasted_iota(jnp.int32, (LANE, LANE), 1) // HEAD_DIM
    head_mask = h_r == h_c
    head_ones = jnp.where(head_mask, 1.0, 0.0).astype(BF16)

    def head_sum(x):
        hi, mid, _ = _split3(x)
        return (jnp.dot(hi, head_ones, preferred_element_type=F32)
                + jnp.dot(mid, head_ones, preferred_element_type=F32))

    def mm(a, b):
        return jnp.dot(a.astype(BF16), b.astype(BF16), preferred_element_type=F32)

    chains = [(hp, c) for hp in range(n_pairs) for c in range(n_chunks)]
    lanes_of = lambda hp: slice(hp * LANE, (hp + 1) * LANE)
    rows_of = lambda c: pl.ds(c * L, L)
    each = lambda fn: {ch: fn(ch) for ch in chains}

    load = lambda ref: each(lambda ch: ref[0, rows_of(ch[1]), lanes_of(ch[0])])
    r, k, v, a = (each(lambda ch, d=d: d[ch].astype(F32)) for d in (load(r_ref), load(k_ref), load(v_ref), load(a_ref)))
    kkr = each(lambda ch, d=load(kkr_ref): d[ch].astype(F32))
    lw = load(lw_ref)
    kk_sq = each(lambda ch: head_sum(kkr[ch] * kkr[ch]))
    cl = each(lambda ch: _dot_split_rhs(tri_incl, lw[ch]))
    kk = each(lambda ch: kkr[ch] / jnp.maximum(jnp.sqrt(kk_sq[ch]), 1e-12))
    p_incl = each(lambda ch: jnp.exp(cl[ch]))
    p_inv = each(lambda ch: jnp.exp(-cl[ch]))
    rt = each(lambda ch: (r[ch] * p_incl[ch]).astype(BF16))
    at = each(lambda ch: (-kk[ch] * jnp.exp(cl[ch] - lw[ch])).astype(BF16))
    kt = each(lambda ch: k[ch] * p_inv[ch])
    bt = each(lambda ch: kk[ch] * a[ch] * p_inv[ch])
    gram = each(lambda ch: _dot_nt(
        jnp.concatenate([_pair_blocks(at[ch]), _pair_blocks(rt[ch])], axis=0),
        jnp.concatenate([_pair_blocks(bt[ch]), _pair_blocks(kt[ch])], axis=0).astype(BF16)))
    a_ab = each(lambda ch: jnp.where(strict2, gram[ch][:L2, :L2], 0.0))
    a_ak = each(lambda ch: jnp.where(strict2, gram[ch][:L2, L2:], 0.0).astype(BF16))
    a_rb = each(lambda ch: jnp.where(incl2, gram[ch][L2:, :L2], 0.0).astype(BF16))
    a_rk = each(lambda ch: jnp.where(incl2, gram[ch][L2:, L2:], 0.0).astype(BF16))
    tinv = each(lambda ch: eye2 + a_ab[ch])
    apow = a_ab
    for _ in range(int(np.log2(L)) - 1):
        apow = each(lambda ch: mm(apow[ch], apow[ch]))
        tinv = each(lambda ch: tinv[ch] + mm(tinv[ch], apow[ch]))
    tinv = each(lambda ch: tinv[ch].astype(BF16))
    v_bf = each(lambda ch: v[ch].astype(BF16))
    v2 = each(lambda ch: _pair_blocks(v_bf[ch]))
    akv = each(lambda ch: _fold_pair(jnp.dot(a_ak[ch], v2[ch], preferred_element_type=F32)))
    rkv = each(lambda ch: _fold_pair(jnp.dot(a_rk[ch], v2[ch], preferred_element_type=F32)))
    p_last = each(lambda ch: p_incl[ch][L - 1:L, :])
    wts = each(lambda ch: jnp.concatenate([bt[ch] * p_last[ch], kt[ch] * p_last[ch]], axis=0).astype(BF16))
    bonus = each(lambda ch: head_sum(r[ch] * k[ch] * rk_ref[:, lanes_of(ch[0])]) * v[ch])

    pairs = range(n_pairs)
    states = [s_ref[hp] for hp in pairs]
    for c in range(n_chunks):
        s_bf = [states[hp].astype(BF16) for hp in pairs]
        m = [_dot_nt(at[hp, c], s_bf[hp]) + akv[hp, c] for hp in pairs]
        y = [_dot_nt(rt[hp, c], s_bf[hp]) + rkv[hp, c] for hp in pairs]
        u = [_fold_pair(jnp.dot(tinv[hp, c], _pair_blocks(m[hp].astype(BF16)), preferred_element_type=F32))
             for hp in pairs]
        u_bf = [u[hp].astype(BF16) for hp in pairs]
        upd = [lax.dot_general(jnp.concatenate([u_bf[hp], v_bf[hp, c]], axis=0), wts[hp, c],
                               (((0,), (0,)), ((), ())), preferred_element_type=F32) for hp in pairs]
        states = [states[hp] * p_last[hp, c] + jnp.where(head_mask, upd[hp], 0.0) for hp in pairs]
        y = [y[hp] + _fold_pair(jnp.dot(a_rb[hp, c], _pair_blocks(u_bf[hp]), preferred_element_type=F32))
             for hp in pairs]
        mean = [head_sum(y[hp]) * (1.0 / HEAD_DIM) for hp in pairs]
        yc = [y[hp] - mean[hp] for hp in pairs]
        var = [head_sum(yc[hp] * yc[hp]) * (1.0 / HEAD_DIM) for hp in pairs]
        for hp in pairs:
            yn = yc[hp] * lax.rsqrt(var[hp] + GN_EPS) * lng_ref[:, lanes_of(hp)] + lnb_ref[:, lanes_of(hp)]
            o_ref[0, rows_of(c), lanes_of(hp)] = (
                (yn + bonus[hp, c]) * g_ref[0, rows_of(c), lanes_of(hp)].astype(F32)).astype(o_ref.dtype)
    for hp in pairs:
        s_ref[hp] = states[hp]


def _rwkv_scan(r, k, v, kkr, a, lw, g, r_k, ln_g, ln_b, tt=256):
    b, t_len, w = r.shape
    n_p = RWKV_PAIRS_PER_STEP
    width = n_p * LANE
    tile = pl.BlockSpec((1, tt, width), lambda bi, hp, ti: (bi, ti, hp))
    vec = pl.BlockSpec((1, width), lambda bi, hp, ti: (0, hp))
    kernel = functools.partial(_rwkv_scan_kernel, n_chunks=tt // RWKV_CHUNK, n_pairs=n_p)
    return pl.pallas_call(
        kernel,
        out_shape=jax.ShapeDtypeStruct((b, t_len, w), BF16),
        grid=(b, w // width, t_len // tt),
        in_specs=[tile] * 7 + [vec] * 3,
        out_specs=tile,
        scratch_shapes=[pltpu.VMEM((n_p, LANE, LANE), F32)],
        compiler_params=pltpu.CompilerParams(dimension_semantics=("parallel", "parallel", "arbitrary"),
                                             vmem_limit_bytes=VMEM_LIMIT),
        name="rwkv_scan",
    )(r, k, v, kkr, a, lw, g, r_k, ln_g, ln_b)


def _rwkv_time_mix(z3, mu, w0, w_up, a0, a_up, g_up, k_k, k_a, r_k, ln_g, ln_b):
    row = lambda u: u.reshape(1, -1)
    mu_p = row(_take_columns(mu, _rwkv_source_columns()))
    r, k, v, kkr, a, lw, g = _rwkv_prep(z3, mu_p, row(w0), _pad_rows(w_up, LANE).astype(BF16), row(a0),
                                        _pad_rows(a_up, LANE).astype(BF16), g_up.astype(BF16), row(k_k), row(k_a))
    return _rwkv_scan(r, k, v, kkr, a, lw, g, row(r_k), row(ln_g), row(ln_b))


def _layer_norm(x, g, b):
    mean = jnp.mean(x, axis=-1, keepdims=True)
    xc = x - mean
    var = jnp.mean(xc * xc, axis=-1, keepdims=True)
    return xc * lax.rsqrt(var + LN_EPS) * g + b


def _out_proj_kernel(on_ref, or_ref, x_ref, w_ref, g_ref, b_ref, rwh_ref, rwm_ref, rb_ref,
                     h_ref, idx_ref, wgt_ref, *, alpha):
    o = jnp.concatenate([on_ref[...], or_ref[...]], axis=1)
    mix = jnp.dot(o, w_ref[...], preferred_element_type=F32)
    h = _layer_norm(alpha * x_ref[...] + mix, g_ref[...], b_ref[...])
    h_ref[...] = h
    h_hi, h_mid, _ = _split3(h)
    logits = (jnp.dot(h_hi, rwh_ref[...], preferred_element_type=F32)
              + jnp.dot(h_hi, rwm_ref[...], preferred_element_type=F32)
              + jnp.dot(h_mid, rwh_ref[...], preferred_element_type=F32)) + rb_ref[...]
    lane = lax.broadcasted_iota(jnp.int32, logits.shape, 1)
    logits = jnp.where(lane < N_EXPERTS, logits, -jnp.inf)
    idx_out = jnp.zeros(logits.shape, jnp.int32)
    val_out = jnp.full(logits.shape, -jnp.inf, F32)
    for k in range(TOP_K):
        best = jnp.max(logits, axis=-1, keepdims=True)
        first = jnp.min(jnp.where(logits == best, lane, LANE), axis=-1, keepdims=True)
        idx_out = jnp.where(lane == k, first, idx_out)
        val_out = jnp.where(lane == k, best, val_out)
        logits = jnp.where(lane == first, -jnp.inf, logits)
    e = jnp.exp(val_out - jnp.max(val_out, axis=-1, keepdims=True))
    idx_ref[...] = idx_out
    wgt_ref[...] = e / jnp.sum(e, axis=-1, keepdims=True)


def _out_proj_router(o_nsa, o_rwkv, x2, w_out, ln_g, ln_b, router_w, router_b, alpha, tm=512):
    n_tok, d = x2.shape
    half = o_nsa.shape[1]
    rw = jnp.pad(router_w, ((0, 0), (0, LANE - N_EXPERTS)))
    rw_hi = rw.astype(BF16)
    rw_mid = (rw - rw_hi.astype(F32)).astype(BF16)
    rb = jnp.pad(router_b, (0, LANE - N_EXPERTS)).reshape(1, LANE)
    row_blk = lambda n: pl.BlockSpec((tm, n), lambda i: (i, 0))
    full = lambda m, n: pl.BlockSpec((m, n), lambda i: (0, 0))
    return pl.pallas_call(
        functools.partial(_out_proj_kernel, alpha=alpha),
        out_shape=(jax.ShapeDtypeStruct((n_tok, d), F32), jax.ShapeDtypeStruct((n_tok, LANE), jnp.int32),
                   jax.ShapeDtypeStruct((n_tok, LANE), F32)),
        grid=(n_tok // tm,),
        in_specs=[row_blk(half), row_blk(half), row_blk(d), full(2 * half, d), full(1, d), full(1, d),
                  full(d, LANE), full(d, LANE), full(1, LANE)],
        out_specs=(row_blk(d), row_blk(LANE), row_blk(LANE)),
        compiler_params=pltpu.CompilerParams(dimension_semantics=("parallel",), vmem_limit_bytes=VMEM_LIMIT),
        name="out_proj_router",
    )(o_nsa, o_rwkv, x2, w_out.astype(BF16), ln_g.reshape(1, d), ln_b.reshape(1, d), rw_hi, rw_mid, rb)


MOE_ITEM_ROWS = 1280
MOE_SUB_ROWS = 256
MOE_F_TILE = 256
MOE_TOKEN_TILE = 256


MOE_RANK_BLOCK = 512


def _moe_rank_kernel(e_ref, rank_ref, cnt_ref, carry_ref):
    @pl.when(pl.program_id(0) == 0)
    def _():
        carry_ref[...] = jnp.zeros_like(carry_ref)

    blk = e_ref.shape[0]
    lane = lax.broadcasted_iota(jnp.int32, (blk, LANE), 1)
    onehot = jnp.where(e_ref[...] == lane, 1.0, 0.0)
    tri = jnp.where(lax.broadcasted_iota(jnp.int32, (blk, blk), 0) >= lax.broadcasted_iota(jnp.int32, (blk, blk), 1),
                    1.0, 0.0).astype(BF16)
    csum = jnp.dot(tri, onehot.astype(BF16), preferred_element_type=F32)
    carry = carry_ref[...]
    rank_ref[...] = (jnp.sum((csum + carry) * onehot, axis=1, keepdims=True) - 1.0).astype(jnp.int32)
    carry_ref[...] = carry + csum[blk - 1:blk, :]
    cnt_ref[...] = carry_ref[...].astype(jnp.int32)


def _moe_rank(flat_e):
    n_assign = flat_e.shape[0]
    return pl.pallas_call(
        _moe_rank_kernel,
        out_shape=(jax.ShapeDtypeStruct((n_assign, 1), jnp.int32), jax.ShapeDtypeStruct((1, LANE), jnp.int32)),
        grid=(n_assign // MOE_RANK_BLOCK,),
        in_specs=[pl.BlockSpec((MOE_RANK_BLOCK, 1), lambda i: (i, 0))],
        out_specs=(pl.BlockSpec((MOE_RANK_BLOCK, 1), lambda i: (i, 0)), pl.BlockSpec((1, LANE), lambda i: (0, 0))),
        scratch_shapes=[pltpu.VMEM((1, LANE), F32)],
        compiler_params=pltpu.CompilerParams(dimension_semantics=("arbitrary",), vmem_limit_bytes=VMEM_LIMIT),
        name="moe_rank",
    )(flat_e.reshape(n_assign, 1))


def _moe_tables(top_idx, n_items):
    n_tok = top_idx.shape[0]
    flat_e = top_idx.reshape(-1)
    rank, counts = _moe_rank(flat_e)
    counts = counts[0, :N_EXPERTS]
    items_e = (counts + MOE_ITEM_ROWS - 1) // MOE_ITEM_ROWS
    items_end = jnp.cumsum(items_e)
    item_start_e = items_end - items_e
    dest = jnp.take(item_start_e * MOE_ITEM_ROWS, flat_e) + rank[:, 0]
    item = jnp.arange(n_items, dtype=jnp.int32)
    valid = item < items_end[-1]
    last_e = jnp.max(jnp.where(counts > 0, jnp.arange(N_EXPERTS, dtype=jnp.int32), 0))
    item_e = jnp.minimum(jnp.sum(items_end[None, :] <= item[:, None], axis=1).astype(jnp.int32), N_EXPERTS - 1)
    item_e = jnp.where(valid, item_e, last_e)
    item_nv = jnp.where(valid, jnp.clip(counts[item_e] - (item - item_start_e[item_e]) * MOE_ITEM_ROWS,
                                        0, MOE_ITEM_ROWS), 0).astype(jnp.int32)
    return dest.reshape(n_tok, TOP_K), item_e, item_nv


DISPATCH_ISSUE_UNROLL = 8


def _dispatch_kernel(nv_ref, dest_ref, h_ref, xg_ref, stage_ref, zero_ref, sem_ref, zsem_ref):
    tm = h_ref.shape[0]
    i = pl.program_id(0)
    slot = i % 2
    subs_per_item = MOE_ITEM_ROWS // MOE_SUB_ROWS

    n_subs = xg_ref.shape[0] // MOE_SUB_ROWS

    def fill(sub, kind):
        return pltpu.make_async_copy(
            zero_ref, xg_ref.at[pl.ds(pl.multiple_of(sub * MOE_SUB_ROWS, MOE_SUB_ROWS), MOE_SUB_ROWS)],
            zsem_ref.at[kind])

    def is_kind(sub, kind):
        used = nv_ref[sub // subs_per_item] - (sub % subs_per_item) * MOE_SUB_ROWS
        return (used <= 0) if kind == 1 else ((used > 0) & (used < MOE_SUB_ROWS))

    def for_fills(kind, action):
        def body(sub, carry):
            @pl.when(is_kind(sub, kind))
            def _():
                action(fill(sub, kind))
            return carry

        lax.fori_loop(0, n_subs, body, 0)

    @pl.when(i == 0)
    def _():
        zero_ref[...] = jnp.zeros_like(zero_ref)
        for_fills(0, lambda copy: copy.start())
        for_fills(1, lambda copy: copy.start())
        for_fills(0, lambda copy: copy.wait())

    def wait_slot(s):
        for _ in range(TOP_K):
            pltpu.make_async_copy(stage_ref.at[s], xg_ref.at[pl.ds(0, tm)], sem_ref.at[s]).wait()

    @pl.when(i >= 2)
    def _():
        wait_slot(slot)

    stage_ref[slot] = h_ref[...]

    def body(j, carry):
        for u in range(DISPATCH_ISSUE_UNROLL):
            t = j * DISPATCH_ISSUE_UNROLL + u
            for k in range(TOP_K):
                pltpu.make_async_copy(stage_ref.at[slot, pl.ds(t, 1)], xg_ref.at[pl.ds(dest_ref[0, 0, k * tm + t], 1)],
                                      sem_ref.at[slot]).start(priority=k % 2)
        return carry

    lax.fori_loop(0, tm // DISPATCH_ISSUE_UNROLL, body, 0)

    @pl.when(i == pl.num_programs(0) - 1)
    def _():
        wait_slot(slot)

        @pl.when(i >= 1)
        def _():
            wait_slot(1 - slot)

        for_fills(1, lambda copy: copy.wait())


def _dispatch(h1, dest_tiles, item_nv, n_rows, tm):
    n_tok, d = h1.shape
    return pl.pallas_call(
        _dispatch_kernel,
        out_shape=jax.ShapeDtypeStruct((n_rows, d), F32),
        grid=(n_tok // tm,),
        in_specs=[pl.BlockSpec(memory_space=pltpu.SMEM),
                  pl.BlockSpec((1, 1, TOP_K * tm), lambda i: (i, 0, 0), memory_space=pltpu.SMEM),
                  pl.BlockSpec((tm, d), lambda i: (i, 0))],
        out_specs=pl.BlockSpec(memory_space=pl.ANY),
        scratch_shapes=[pltpu.VMEM((2, tm, d), F32), pltpu.VMEM((MOE_SUB_ROWS, d), F32),
                        pltpu.SemaphoreType.DMA((2,)), pltpu.SemaphoreType.DMA((2,))],
        compiler_params=pltpu.CompilerParams(dimension_semantics=("arbitrary",), vmem_limit_bytes=VMEM_LIMIT),
        name="moe_dispatch",
    )(item_nv, dest_tiles, h1)


def _moe_kernel(item_e_ref, item_nv_ref, xg_ref, wg_ref, bg_ref, wu_ref, bu_ref, wd_ref, bd_ref, o_ref,
                stage_ref, xb_ref, sem_ref):
    i = pl.program_id(0)
    f = pl.program_id(1)
    nv = item_nv_ref[i]
    n_sub_max = MOE_ITEM_ROWS // MOE_SUB_ROWS
    sub_count = lambda rows_used: (rows_used + MOE_SUB_ROWS - 1) // MOE_SUB_ROWS
    n_sub = sub_count(nv)
    rows = [pl.ds(sb * MOE_SUB_ROWS, MOE_SUB_ROWS) for sb in range(n_sub_max)]
    cur = i % 2

    def fetch(item, sb):
        first = pl.multiple_of(item * MOE_ITEM_ROWS + sb * MOE_SUB_ROWS, MOE_SUB_ROWS)
        return pltpu.make_async_copy(xg_ref.at[pl.ds(first, MOE_SUB_ROWS)], stage_ref.at[sb % 2], sem_ref.at[sb % 2])

    @pl.when((i == 0) & (f == 0))
    def _():
        for sb in range(n_sub_max):
            @pl.when(sb < n_sub)
            def _(sb=sb):
                copy = fetch(0, sb)
                copy.start()
                copy.wait()
                xb_ref[0, rows[sb], :] = stage_ref[sb % 2].astype(BF16)

    has_next = i + 1 < pl.num_programs(0)
    n_sub_next = jnp.where(has_next, sub_count(item_nv_ref[jnp.minimum(i + 1, pl.num_programs(0) - 1)]), 0)
    for sb in range(n_sub_max):
        @pl.when((f == sb + 1) & (sb < n_sub_next))
        def _(sb=sb):
            fetch(i + 1, sb).wait()
            xb_ref[1 - cur, rows[sb], :] = stage_ref[sb % 2].astype(BF16)

        @pl.when((f == sb) & (sb < n_sub_next))
        def _(sb=sb):
            fetch(i + 1, sb).start()

    @pl.when(f == 0)
    def _():
        for sb in range(n_sub_max):
            bias = jnp.broadcast_to(bd_ref[0], (MOE_SUB_ROWS, o_ref.shape[1]))
            o_ref[rows[sb], :] = jnp.where(sb * MOE_SUB_ROWS < nv, bias, 0.0)

    for n in range(1, n_sub_max + 1):
        @pl.when(n_sub == n)
        def _(n=n):
            wg = wg_ref[0].astype(BF16)
            wu = wu_ref[0].astype(BF16)
            wd = wd_ref[0].astype(BF16)
            xs = [xb_ref[cur, rows[sb], :] for sb in range(n)]
            gates = [jnp.dot(xs[sb], wg, preferred_element_type=F32) for sb in range(n)]
            ups = [jnp.dot(xs[sb], wu, preferred_element_type=F32) for sb in range(n)]
            for sb in range(n):
                gate = jnp.minimum(gates[sb] + bg_ref[0], SWIGLU_LIMIT)
                up = jnp.clip(ups[sb] + bu_ref[0], -SWIGLU_LIMIT, SWIGLU_LIMIT)
                h = gate * jax.nn.sigmoid(SWIGLU_ALPHA * gate) * (up + 1.0)
                o_ref[rows[sb], :] += jnp.dot(h.astype(BF16), wd, preferred_element_type=F32)


def _moe_experts(xg, item_e, item_nv, w_gate, b_gate, w_up, b_up, w_down, b_down):
    n_rows = xg.shape[0]
    d = w_gate.shape[1]
    n_items = n_rows // MOE_ITEM_ROWS
    n_e, _, d_ff = w_gate.shape
    n_f = d_ff // MOE_F_TILE

    def f_idx(i, f, nv):
        return jnp.where(nv[i] > 0, f, n_f - 1)

    grid_spec = pltpu.PrefetchScalarGridSpec(
        num_scalar_prefetch=2,
        grid=(n_items, n_f),
        in_specs=[
            pl.BlockSpec(memory_space=pl.ANY),
            pl.BlockSpec((1, d, MOE_F_TILE), lambda i, f, e, nv: (e[i], 0, f_idx(i, f, nv))),
            pl.BlockSpec((1, 1, MOE_F_TILE), lambda i, f, e, nv: (e[i], 0, f_idx(i, f, nv))),
            pl.BlockSpec((1, d, MOE_F_TILE), lambda i, f, e, nv: (e[i], 0, f_idx(i, f, nv))),
            pl.BlockSpec((1, 1, MOE_F_TILE), lambda i, f, e, nv: (e[i], 0, f_idx(i, f, nv))),
            pl.BlockSpec((1, MOE_F_TILE, d), lambda i, f, e, nv: (e[i], f_idx(i, f, nv), 0)),
            pl.BlockSpec((1, 1, d), lambda i, f, e, nv: (e[i], 0, 0)),
        ],
        out_specs=pl.BlockSpec((MOE_ITEM_ROWS, d), lambda i, f, e, nv: (i, 0)),
        scratch_shapes=[pltpu.VMEM((2, MOE_SUB_ROWS, d), F32), pltpu.VMEM((2, MOE_ITEM_ROWS, d), BF16),
                        pltpu.SemaphoreType.DMA((2,))],
    )
    assert MOE_ITEM_ROWS // MOE_SUB_ROWS < n_f
    return pl.pallas_call(
        _moe_kernel,
        out_shape=jax.ShapeDtypeStruct((n_rows, d), F32),
        grid_spec=grid_spec,
        compiler_params=pltpu.CompilerParams(dimension_semantics=("arbitrary", "arbitrary"),
                                             vmem_limit_bytes=VMEM_LIMIT),
        name="moe_experts",
    )(item_e, item_nv, xg, w_gate, b_gate.reshape(n_e, 1, d_ff), w_up, b_up.reshape(n_e, 1, d_ff), w_down,
      b_down.reshape(n_e, 1, d))


COMBINE_ISSUE_UNROLL = 16


def _final_kernel(dest_ref, dest_next_ref, h_ref, eo_ref, wgt_ref, p_ref, g_ref, b_ref, wgate_ref, wple_ref,
                  o_ref, rows_ref, sem_ref, *, alpha):
    tm, d = h_ref.shape
    n_rows = TOP_K * tm
    i = pl.program_id(0)
    slot = i % 2

    def row_copy(idx_ref, r, s):
        return pltpu.make_async_copy(eo_ref.at[pl.ds(idx_ref[0, 0, r], 1)], rows_ref.at[s, pl.ds(r, 1)],
                                     sem_ref.at[s])

    def issue(idx_ref, s):
        def body(j, carry):
            for u in range(COMBINE_ISSUE_UNROLL):
                row_copy(idx_ref, j * COMBINE_ISSUE_UNROLL + u, s).start(priority=u % 2)
            return carry

        lax.fori_loop(0, n_rows // COMBINE_ISSUE_UNROLL, body, 0)

    @pl.when(i == 0)
    def _():
        issue(dest_ref, 0)

    @pl.when(i + 1 < pl.num_programs(0))
    def _():
        issue(dest_next_ref, 1 - slot)

    pltpu.make_async_copy(eo_ref.at[pl.ds(0, n_rows)], rows_ref.at[slot], sem_ref.at[slot]).wait()
    wgt = wgt_ref[...]
    ffn = wgt[:, 0:1] * rows_ref[slot, 0:tm, :]
    for k in range(1, TOP_K):
        ffn = ffn + wgt[:, k:k + 1] * rows_ref[slot, k * tm:(k + 1) * tm, :]
    h = _layer_norm(alpha * h_ref[...] + ffn, g_ref[...], b_ref[...])
    gate = jax.nn.sigmoid(jnp.dot(h.astype(BF16), wgate_ref[...], preferred_element_type=F32))
    ple = jnp.dot(p_ref[...].astype(BF16), wple_ref[...], preferred_element_type=F32)
    o_ref[...] = h + gate * ple


def _dest_tiles(dest, tm):
    n_tiles = dest.shape[0] // tm
    return dest.reshape(n_tiles, tm, TOP_K).transpose(0, 2, 1).reshape(n_tiles, 1, TOP_K * tm)


def _final(h1, eo, dest_tiles, top_w, p2, ln_g, ln_b, ple_gate_w, ple_w, alpha, tm):
    n_tok, d = h1.shape
    n_tiles = n_tok // tm
    row_blk = lambda n: pl.BlockSpec((tm, n), lambda i: (i, 0))
    full = lambda m, n: pl.BlockSpec((m, n), lambda i: (0, 0))
    idx_blk = lambda fn: pl.BlockSpec((1, 1, TOP_K * tm), fn, memory_space=pltpu.SMEM)
    return pl.pallas_call(
        functools.partial(_final_kernel, alpha=alpha),
        out_shape=jax.ShapeDtypeStruct((n_tok, d), F32),
        grid=(n_tiles,),
        in_specs=[idx_blk(lambda i: (i, 0, 0)), idx_blk(lambda i: (jnp.minimum(i + 1, n_tiles - 1), 0, 0)),
                  row_blk(d), pl.BlockSpec(memory_space=pl.ANY), row_blk(LANE), row_blk(p2.shape[1]),
                  full(1, d), full(1, d), full(d, d), full(p2.shape[1], d)],
        out_specs=row_blk(d),
        scratch_shapes=[pltpu.VMEM((2, TOP_K * tm, d), F32), pltpu.SemaphoreType.DMA((2,))],
        compiler_params=pltpu.CompilerParams(dimension_semantics=("arbitrary",), vmem_limit_bytes=VMEM_LIMIT),
        name="combine_ln_ple",
    )(dest_tiles, dest_tiles, h1, eo, top_w, p2, ln_g.reshape(1, d), ln_b.reshape(1, d),
      ple_gate_w.astype(BF16), ple_w.astype(BF16))


def kernel(x, p, w_in, cmp_pe_k, cmp_w1_k, cmp_w2_k, cmp_pe_v, cmp_w1_v, cmp_w2_v, rwkv_mu, rwkv_w0, rwkv_w_up, rwkv_a0, rwkv_a_up, rwkv_g_up, rwkv_k_k, rwkv_k_a, rwkv_r_k, rwkv_ln_g, rwkv_ln_b, w_out, ln1_g, ln1_b, router_w, router_b, exp_w_gate, exp_b_gate, exp_w_up, exp_b_up, exp_w_down, exp_b_down, ln2_g, ln2_b, ple_w, ple_gate_w):
    b, t_len, d = x.shape
    depth = w_in.shape[0]
    alpha = float((2 * depth) ** 0.25)
    n_tok = b * t_len
    n_items = (n_tok * TOP_K) // MOE_ITEM_ROWS + N_EXPERTS
    h = x.reshape(n_tok, d)
    for i in range(depth):
        w_bf = _take_columns(w_in[i].astype(BF16), _z_source_columns())
        z3 = _in_proj(h, w_bf).reshape(b, t_len, Z_COLS)
        pe, w1, w2 = _nsa_prepare(cmp_pe_k[i], cmp_w1_k[i], cmp_w2_k[i], cmp_pe_v[i], cmp_w1_v[i], cmp_w2_v[i])
        kv_cmp = _compress(z3, pe, w1, w2)
        o_nsa = _nsa_attention(z3, kv_cmp)
        o_rwkv = _rwkv_time_mix(z3, rwkv_mu[i], rwkv_w0[i], rwkv_w_up[i], rwkv_a0[i], rwkv_a_up[i], rwkv_g_up[i],
                                rwkv_k_k[i], rwkv_k_a[i], rwkv_r_k[i].reshape(-1), rwkv_ln_g[i], rwkv_ln_b[i])
        h1, top_idx, top_w = _out_proj_router(o_nsa.reshape(n_tok, -1), o_rwkv.reshape(n_tok, -1), h, w_out[i],
                                              ln1_g[i], ln1_b[i], router_w[i], router_b[i], alpha)
        dest, item_e, item_nv = _moe_tables(top_idx[:, :TOP_K], n_items)
        dest_tiles = _dest_tiles(dest, MOE_TOKEN_TILE)
        xg = _dispatch(h1, dest_tiles, item_nv, n_items * MOE_ITEM_ROWS, MOE_TOKEN_TILE)
        eo = _moe_experts(xg, item_e, item_nv, exp_w_gate[i], exp_b_gate[i], exp_w_up[i], exp_b_up[i],
                          exp_w_down[i], exp_b_down[i])
        h = _final(h1, eo, dest_tiles, top_w, p[i].reshape(n_tok, -1), ln2_g[i], ln2_b[i], ple_gate_w[i],
                   ple_w[i], alpha, MOE_TOKEN_TILE)
    return h.reshape(b, t_len, d)
```

```python
import functools

import numpy as np
import jax
import jax.numpy as jnp
from jax import lax
from jax.experimental import pallas as pl
from jax.experimental.pallas import tpu as pltpu

F32 = jnp.float32
BF16 = jnp.bfloat16

LANE = 128
D_MODEL = 2048
HEAD_DIM = 64
NSA_HEADS = 16
NSA_KV_HEADS = 4
NSA_GROUP = NSA_HEADS // NSA_KV_HEADS
NSA_WIDTH = NSA_HEADS * HEAD_DIM
NSA_KV_WIDTH = NSA_KV_HEADS * HEAD_DIM
CMP_BLOCK = 32
CMP_STRIDE = 16
SEL_BLOCK = 64
N_SELECT = 16
WINDOW = 512
N_GATES = 3
Q_BLOCK = 64
RWKV_HEADS = 16
RWKV_WIDTH = RWKV_HEADS * HEAD_DIM
DECAY_LORA = 96
ICLR_LORA = 96
GATE_LORA = 256
GN_EPS = 64e-5
N_EXPERTS = 32
TOP_K = 4
SWIGLU_ALPHA = 1.702
SWIGLU_LIMIT = 7.0
PLE_DIM = 256
LN_EPS = 1e-5
NEG_INF = -1e30

NSA_COLS = NSA_WIDTH + 6 * NSA_KV_WIDTH + NSA_HEADS * N_GATES
RWKV_COLS = 3 * RWKV_WIDTH + DECAY_LORA + ICLR_LORA + GATE_LORA

RB_R = 0
RB_K = RB_R + RWKV_WIDTH // LANE
RB_V = RB_K + RWKV_WIDTH // LANE
RB_WLO = RB_V + RWKV_WIDTH // LANE
RB_ALO = RB_WLO + 1
RB_GLO = RB_ALO + 1
RWKV_BLOCKS = RB_GLO + GATE_LORA // LANE
ZB_RWKV = 0
ZB_Q = ZB_RWKV + RWKV_BLOCKS
ZB_KVC = ZB_Q + NSA_WIDTH // LANE
ZB_KVS = ZB_KVC + NSA_KV_HEADS
ZB_KVW = ZB_KVS + NSA_KV_HEADS
ZB_GATE = ZB_KVW + NSA_KV_HEADS
Z_BLOCKS = ZB_GATE + NSA_KV_HEADS
Z_COLS = Z_BLOCKS * LANE
assert (ZB_Q * LANE) % (NSA_GROUP * HEAD_DIM) == 0

VMEM_LIMIT = 56 * 1024 * 1024


def _z_source_columns():
    src = np.full((Z_COLS,), -1, np.int64)
    src[ZB_Q * LANE:ZB_Q * LANE + NSA_WIDTH] = np.arange(NSA_WIDTH)
    for branch in range(3):
        k0 = NSA_WIDTH + 2 * branch * NSA_KV_WIDTH
        v0 = k0 + NSA_KV_WIDTH
        for g in range(NSA_KV_HEADS):
            base = (ZB_KVC + branch * NSA_KV_HEADS + g) * LANE
            src[base:base + HEAD_DIM] = k0 + g * HEAD_DIM + np.arange(HEAD_DIM)
            src[base + HEAD_DIM:base + 2 * HEAD_DIM] = v0 + g * HEAD_DIM + np.arange(HEAD_DIM)
    g0 = NSA_WIDTH + 6 * NSA_KV_WIDTH
    for g in range(NSA_KV_HEADS):
        base = (ZB_GATE + g) * LANE
        for i in range(N_GATES):
            for r in range(NSA_GROUP):
                src[base + i * NSA_GROUP + r] = g0 + (g * NSA_GROUP + r) * N_GATES + i
    rwkv = _rwkv_source_columns()
    src[ZB_RWKV * LANE:(ZB_RWKV + RWKV_BLOCKS) * LANE] = np.where(rwkv >= 0, NSA_COLS + rwkv, -1)
    return src


def _rwkv_source_columns():
    src = np.full((RWKV_BLOCKS * LANE,), -1, np.int64)
    src[:3 * RWKV_WIDTH] = np.arange(3 * RWKV_WIDTH)
    src[RB_WLO * LANE:RB_WLO * LANE + DECAY_LORA] = 3 * RWKV_WIDTH + np.arange(DECAY_LORA)
    src[RB_ALO * LANE:RB_ALO * LANE + ICLR_LORA] = 3 * RWKV_WIDTH + DECAY_LORA + np.arange(ICLR_LORA)
    src[RB_GLO * LANE:] = 3 * RWKV_WIDTH + DECAY_LORA + ICLR_LORA + np.arange(GATE_LORA)
    return src


def _take_columns(w, src):
    pieces, i, n = [], 0, len(src)
    while i < n:
        j = i + 1
        if src[i] < 0:
            while j < n and src[j] < 0:
                j += 1
            pieces.append(jnp.zeros(w.shape[:-1] + (j - i,), w.dtype))
        else:
            while j < n and src[j] == src[i] + (j - i):
                j += 1
            pieces.append(w[..., int(src[i]):int(src[i]) + j - i])
        i = j
    return jnp.concatenate(pieces, axis=-1)


def _pad_rows(w, rows):
    return jnp.pad(w, ((0, rows - w.shape[0]), (0, 0)))


def _in_proj_kernel(x_ref, w_ref, z_ref, xb_ref):
    @pl.when(pl.program_id(1) == 0)
    def _():
        xb_ref[...] = x_ref[...].astype(BF16)

    z_ref[...] = jnp.dot(xb_ref[...], w_ref[...], preferred_element_type=F32).astype(z_ref.dtype)


def _in_proj(x2, w_bf, tm=1024, tn=1664):
    n_tok, d = x2.shape
    n_cols = w_bf.shape[1]
    return pl.pallas_call(
        _in_proj_kernel,
        out_shape=jax.ShapeDtypeStruct((n_tok, n_cols), BF16),
        grid=(n_tok // tm, n_cols // tn),
        in_specs=[pl.BlockSpec((tm, d), lambda i, j: (i, 0)),
                  pl.BlockSpec((d, tn), lambda i, j: (0, j))],
        out_specs=pl.BlockSpec((tm, tn), lambda i, j: (i, j)),
        scratch_shapes=[pltpu.VMEM((tm, d), BF16)],
        compiler_params=pltpu.CompilerParams(dimension_semantics=("parallel", "arbitrary"),
                                             vmem_limit_bytes=VMEM_LIMIT),
        name="in_proj",
    )(x2, w_bf)


def _compress_kernel(kv_ref, pe_ref, w1_ref, w2_ref, out_ref, kv32_ref):
    n_chunks = kv_ref.shape[1] // CMP_STRIDE
    kv32_ref[...] = kv_ref[0].astype(F32)
    acc_lo = jnp.zeros((n_chunks, LANE), F32)
    acc_hi = jnp.zeros((n_chunks, LANE), F32)
    for i in range(CMP_STRIDE):
        rows = kv32_ref[pl.ds(i, n_chunks, stride=CMP_STRIDE), :]
        lo = (rows + pe_ref[i:i + 1, :]).astype(BF16)
        hi = (rows + pe_ref[CMP_STRIDE + i:CMP_STRIDE + i + 1, :]).astype(BF16)
        acc_lo += jnp.dot(lo, w1_ref[i], preferred_element_type=F32)
        acc_hi += jnp.dot(hi, w1_ref[CMP_STRIDE + i], preferred_element_type=F32)
    shifted = jnp.concatenate([acc_hi[1:], jnp.zeros((1, LANE), F32)], axis=0)
    hid = jax.nn.gelu(acc_lo + shifted)
    out = jnp.dot(hid.astype(BF16), w2_ref[...], preferred_element_type=F32)
    row = lax.broadcasted_iota(jnp.int32, out.shape, 0)
    out_ref[0, 0] = jnp.where(row < n_chunks - 1, out, 0.0).astype(out_ref.dtype)


def _compress(z3, pe, w1, w2):
    b, t_len, _ = z3.shape
    n_chunks = t_len // CMP_STRIDE
    return pl.pallas_call(
        _compress_kernel,
        out_shape=jax.ShapeDtypeStruct((b, NSA_KV_HEADS, n_chunks, LANE), BF16),
        grid=(b, NSA_KV_HEADS),
        in_specs=[pl.BlockSpec((1, t_len, LANE), lambda bi, g: (bi, 0, ZB_KVC + g)),
                  pl.BlockSpec((CMP_BLOCK, LANE), lambda bi, g: (0, 0)),
                  pl.BlockSpec((CMP_BLOCK, LANE, LANE), lambda bi, g: (0, 0, 0)),
                  pl.BlockSpec((LANE, LANE), lambda bi, g: (0, 0))],
        out_specs=pl.BlockSpec((1, 1, n_chunks, LANE), lambda bi, g: (bi, g, 0, 0)),
        scratch_shapes=[pltpu.VMEM((t_len, LANE), F32)],
        compiler_params=pltpu.CompilerParams(dimension_semantics=("parallel", "parallel"),
                                             vmem_limit_bytes=VMEM_LIMIT),
        name="kv_compress",
    )(z3, pe, w1, w2)


def _block_diag2(a, b):
    za = jnp.zeros(a.shape[:-1] + (b.shape[-1],), a.dtype)
    zb = jnp.zeros(b.shape[:-1] + (a.shape[-1],), b.dtype)
    return jnp.concatenate([jnp.concatenate([a, za], axis=-1), jnp.concatenate([zb, b], axis=-1)], axis=-2)


def _stack_heads(x):
    return jnp.concatenate([x[:, r * HEAD_DIM:(r + 1) * HEAD_DIM] for r in range(NSA_GROUP)], axis=0)


def _dot_nt(a, b):
    return lax.dot_general(a, b, (((1,), (1,)), ((), ())), preferred_element_type=F32)


def _eye(n, dtype):
    return jnp.where(lax.broadcasted_iota(jnp.int32, (n, n), 0) == lax.broadcasted_iota(jnp.int32, (n, n), 1),
                     1.0, 0.0).astype(dtype)


NSA_KEY_CHUNK = 256
NSA_V_ROWS = HEAD_DIM + 16


def _nsa_kernel(slopes_ref, *refs, n_sel, n_g):
    q_refs = refs[:n_g]
    (kvc_ref, kvs_ref, kvw_ref, gate_ref, ovl_ref, o_ref, vst_ref, vwt_ref, vct_ref, sel_ref,
     sc0_ref, sc1_ref, p0_ref, p1_ref, kas_ref, kaw_ref) = refs[n_g:]
    g_base = pl.program_id(1) * n_g
    qi = pl.program_id(2)
    q0 = qi * Q_BLOCK
    cols = NSA_GROUP * Q_BLOCK
    kc_len = NSA_KEY_CHUNK
    blocks_per_chunk = kc_len // SEL_BLOCK
    n_chunks = kvs_ref.shape[1] // kc_len
    n_cmp_pad = kvc_ref.shape[2]
    n_slc = ovl_ref.shape[0]
    v_rows = vst_ref.shape[2]
    eye_dh = _eye(HEAD_DIM, BF16)
    k_lanes = lambda gg: slice(gg * LANE, gg * LANE + HEAD_DIM)
    v_lanes = lambda gg: slice(gg * LANE + HEAD_DIM, (gg + 1) * LANE)

    @pl.when(qi == 0)
    def _():
        ones_row = jnp.where(lax.broadcasted_iota(jnp.int32, (v_rows - HEAD_DIM, kc_len), 0) == 0,
                             1.0, 0.0).astype(BF16)

        e_lane = lax.broadcasted_iota(jnp.int32, (kc_len, HEAD_DIM), 1)
        e_row = lax.broadcasted_iota(jnp.int32, (kc_len, HEAD_DIM), 0)

        def body(c, carry):
            rows = pl.ds(pl.multiple_of(c * kc_len, kc_len), kc_len)
            pos = c * kc_len + e_row
            extras = jnp.where(e_lane < 3, pos // SEL_BLOCK * SEL_BLOCK,
                               jnp.where(e_lane < 6, pos % SEL_BLOCK, 0)).astype(F32).astype(BF16)
            for gg in range(n_g):
                vst_ref[gg, c, 0:HEAD_DIM] = _dot_nt(eye_dh, kvs_ref[0, rows, v_lanes(gg)]).astype(BF16)
                vwt_ref[gg, c, 0:HEAD_DIM] = _dot_nt(eye_dh, kvw_ref[0, rows, v_lanes(gg)]).astype(BF16)
                vst_ref[gg, c, HEAD_DIM:v_rows] = ones_row
                vwt_ref[gg, c, HEAD_DIM:v_rows] = ones_row
                kas_ref[gg, rows, :] = jnp.concatenate([kvs_ref[0, rows, k_lanes(gg)], extras], axis=1)
                kaw_ref[gg, rows, :] = jnp.concatenate([kvw_ref[0, rows, k_lanes(gg)], extras], axis=1)
            return carry

        lax.fori_loop(0, n_chunks, body, 0)
        for gg in range(n_g):
            vct_ref[gg] = _dot_nt(eye_dh, kvc_ref[0, gg, :, HEAD_DIM:2 * HEAD_DIM]).astype(BF16)

    log2e = float(np.log2(np.e))
    lane = lax.broadcasted_iota(jnp.int32, (1, cols), 1)
    head = lane // Q_BLOCK
    tq = q0 + lane % Q_BLOCK
    sub = lax.broadcasted_iota(jnp.int32, (kc_len, cols), 0)
    half = lax.broadcasted_iota(jnp.int32, (1, LANE), 1) // Q_BLOCK
    blk = lax.broadcasted_iota(jnp.int32, (n_slc, LANE), 0)
    blk8 = lax.broadcasted_iota(jnp.int32, (8, LANE), 0)
    cmp_end = lax.broadcasted_iota(jnp.int32, (n_cmp_pad, 1), 0) * CMP_STRIDE + (CMP_BLOCK - 1)
    d_cmp = tq - cmp_end
    m_cmp = d_cmp >= 0
    any_cmp = tq >= CMP_BLOCK - 1
    d_cmp_f = d_cmp.astype(F32)
    forced = (blk == 0) | (blk == qi) | (blk == qi - 1)
    ovl = ovl_ref[...]

    groups = range(n_g)
    qs, slopes, q_aug = [], [], []
    x_lane = lax.broadcasted_iota(jnp.int32, (cols, HEAD_DIM), 1)
    x_head = lax.broadcasted_iota(jnp.int32, (cols, HEAD_DIM), 0) // Q_BLOCK
    for gg in groups:
        q = _stack_heads(q_refs[gg][0]).astype(F32)
        qs.append((q * (HEAD_DIM ** -0.5 * log2e)).astype(BF16))
        slope = jnp.zeros((1, cols), F32)
        slope_rows = jnp.zeros((cols, HEAD_DIM), F32)
        for r in range(NSA_GROUP):
            s_r = slopes_ref[(g_base + gg) * NSA_GROUP + r] * log2e
            slope = jnp.where(head == r, s_r, slope)
            slope_rows = jnp.where(x_head == r, s_r, slope_rows)
        slopes.append(slope)
        hi, mid, lo = (t.astype(F32) for t in _split3(slope_rows))
        extras = jnp.where(x_lane < 6, jnp.where(x_lane % 3 == 0, hi, jnp.where(x_lane % 3 == 1, mid, lo)), 0.0)
        q_aug.append(jnp.concatenate([qs[gg], extras.astype(BF16)], axis=1))
    st = [_dot_nt(kvc_ref[0, gg, :, 0:HEAD_DIM], qs[gg]) for gg in groups]
    p_cmp = []
    for gg in groups:
        s_m = jnp.where(m_cmp, st[gg] - slopes[gg] * d_cmp_f, NEG_INF)
        e = jnp.exp2(s_m - jnp.max(s_m, axis=0, keepdims=True))
        inv = jnp.where(any_cmp, 1.0 / jnp.maximum(jnp.sum(e, axis=0, keepdims=True), 1e-30), 0.0)
        p_cmp.append(e * inv)
    o_cmp = [jnp.dot(vct_ref[gg], p_cmp[gg].astype(BF16), preferred_element_type=F32) for gg in groups]
    parts = []
    for gg in groups:
        y = p_cmp[gg][:, 0:LANE] + p_cmp[gg][:, LANE:2 * LANE]
        parts.append(_split3(y + pltpu.roll(y, Q_BLOCK, 1)))
    pooled = [[jnp.dot(ovl, parts[gg][i], preferred_element_type=F32) for gg in groups] for i in range(3)]
    for gg in groups:
        imp = pooled[0][gg] + pooled[1][gg] + pooled[2][gg]
        imp = jnp.where(forced, jnp.inf, jnp.where(blk > qi, -jnp.inf, imp))
        tiles = [imp[t:t + 8] for t in range(0, n_slc, 8)]
        ranks = [jnp.zeros((8, LANE), F32) for _ in tiles]
        for jp in range(0, n_slc, 2):
            row = jnp.where(half == 0, imp[jp:jp + 1, :], imp[jp + 1:jp + 2, :])
            for ti, tile in enumerate(tiles):
                t0 = ti * 8
                if t0 > jp + 1:
                    hit = jnp.where(row >= tile, 1.0, 0.0)
                elif t0 + 7 <= jp:
                    hit = jnp.where(row > tile, 1.0, 0.0)
                else:
                    hit = jnp.where(blk8 + t0 > jp + half, jnp.where(row >= tile, 1.0, 0.0),
                                    jnp.where(row > tile, 1.0, 0.0))
                ranks[ti] = ranks[ti] + hit
        rank = jnp.concatenate(ranks, axis=0)
        rank = rank + pltpu.roll(rank, Q_BLOCK, 1)
        neg = jnp.where((rank < n_sel) & (blk <= qi), 0.0, NEG_INF)
        neg2 = jnp.concatenate([neg, neg], axis=1)
        for j in range(n_slc):
            sel_ref[gg, j] = jnp.broadcast_to(neg2[j:j + 1, :], (8, cols))
    fronts = [(q_aug[gg], slopes[gg], None, o_cmp[gg]) for gg in groups]

    init1 = (jnp.full((1, cols), NEG_INF, F32), jnp.zeros((v_rows, cols), F32))

    def block_mask(gg, c):
        tiles = [sel_ref[gg, c * blocks_per_chunk + i] for i in range(blocks_per_chunk)]
        return jnp.concatenate([t for t in tiles for _ in range(SEL_BLOCK // 8)], axis=0)

    streams = {"slc": (kas_ref, vst_ref), "win": (kaw_ref, vwt_ref)}
    sc_slots = (sc0_ref, sc1_ref)
    p_slots = (p0_ref, p1_ref)

    def score(gg, pos, slot):
        stream, c, _ = pos
        rows = pl.ds(pl.multiple_of(c * kc_len, kc_len), kc_len)
        sc_slots[slot][gg] = _dot_nt(streams[stream][0][gg, rows, :], fronts[gg][0])

    def softmax(gg, pos, slot, m_prev):
        sc = sc_slots[slot][gg] + pos[2](gg)
        m_new = jnp.maximum(m_prev, jnp.max(sc, axis=0, keepdims=True))
        p_slots[slot][gg] = jnp.exp2(sc - m_new).astype(BF16)
        return m_new, jnp.exp2(m_prev - m_new)

    def weigh(gg, pos, slot, alpha, acc):
        stream, c, _ = pos
        return alpha * acc + jnp.dot(streams[stream][1][gg, c], p_slots[slot][gg], preferred_element_type=F32)

    def step(pos, slot, nxt, prev, state, alphas):
        if nxt is not None:
            for gg in range(n_g):
                score(gg, nxt, 1 - slot)
        state = {s: list(v) for s, v in state.items()}
        new_alphas = []
        for gg in range(n_g):
            m_prev, acc = state[pos[0]][gg]
            m_new, alpha = softmax(gg, pos, slot, m_prev)
            state[pos[0]][gg] = (m_new, acc)
            new_alphas.append(alpha)
            if prev is not None:
                m_p, acc_p = state[prev[0]][gg]
                state[prev[0]][gg] = (m_p, weigh(gg, prev, 1 - slot, alphas[gg], acc_p))
        return state, new_alphas

    c_cur = qi // blocks_per_chunk
    n_pairs = c_cur // 2
    plain = lambda c: ("slc", c, lambda gg: block_mask(gg, c))

    for gg in range(n_g):
        p1_ref[gg] = jnp.zeros((kc_len, cols), BF16)
        score(gg, plain(0), 0)

    def pair_body(i, carry):
        slc_state, alphas = carry
        k0 = 2 * i
        state, alphas = step(plain(k0), 0, plain(k0 + 1), plain(jnp.maximum(k0 - 1, 0)),
                             {"slc": slc_state}, alphas)
        state, alphas = step(plain(k0 + 1), 1, plain(k0 + 2), plain(k0), state, alphas)
        return tuple(state["slc"]), tuple(alphas)

    slc_state, alphas = lax.fori_loop(0, n_pairs, pair_body,
                                      (tuple(init1 for _ in range(n_g)),
                                       tuple(jnp.ones((1, cols), F32) for _ in range(n_g))))
    k_t = 2 * n_pairs
    odd_neg = jnp.where(c_cur % 2 == 1, 0.0, NEG_INF)
    causal_neg = jnp.where((c_cur * kc_len + sub) <= tq, 0.0, NEG_INF)
    tail = [("slc", k_t, lambda gg: block_mask(gg, k_t) + odd_neg),
            ("slc", c_cur, lambda gg: block_mask(gg, c_cur) + causal_neg)]
    for i in range(WINDOW // kc_len, -1, -1):
        c_raw = c_cur - i
        c_win = jnp.maximum(c_raw, 0)
        dist = tq - (c_win * kc_len + sub)
        band_neg = jnp.where((dist >= 0) & (dist < WINDOW) & (c_raw >= 0), 0.0, NEG_INF)
        tail.append(("win", c_win, lambda gg, band_neg=band_neg: band_neg))
    state = {"slc": list(slc_state), "win": [init1 for _ in range(n_g)]}
    prev = plain(jnp.maximum(k_t - 1, 0))
    for idx, pos in enumerate(tail):
        nxt = tail[idx + 1] if idx + 1 < len(tail) else None
        state, alphas = step(pos, idx % 2, nxt, prev, state, alphas)
        prev = pos
    last_slot = (len(tail) - 1) % 2
    for gg in range(n_g):
        m_p, acc_p = state[prev[0]][gg]
        state[prev[0]][gg] = (m_p, weigh(gg, prev, last_slot, alphas[gg], acc_p))

    eye_q = _eye(Q_BLOCK, BF16)
    eye_lane = _eye(LANE, BF16)
    gates = [jax.nn.sigmoid(_dot_nt(eye_lane, gate_ref[0, :, gg * LANE:(gg + 1) * LANE])) for gg in groups]

    def gate_row(gg, i):
        return jnp.concatenate([gates[gg][i * NSA_GROUP + r:i * NSA_GROUP + r + 1, :] for r in range(NSA_GROUP)],
                               axis=1)

    def normalized(acc):
        return acc[0:HEAD_DIM] / jnp.maximum(acc[HEAD_DIM:HEAD_DIM + 1], 1e-30)

    o_rows = []
    for gg in groups:
        o = (gate_row(gg, 0) * fronts[gg][3] + gate_row(gg, 1) * normalized(state["slc"][gg][1])
             + gate_row(gg, 2) * normalized(state["win"][gg][1])).astype(BF16)
        o_rows.append(jnp.concatenate([o[:, r * Q_BLOCK:(r + 1) * Q_BLOCK] for r in range(NSA_GROUP)], axis=0))
    outs = [_dot_nt(eye_q, o_rows[gg]) for gg in groups]
    for gg in groups:
        o_ref[0, :, gg * cols:(gg + 1) * cols] = outs[gg].astype(o_ref.dtype)


NSA_GROUPS_PER_STEP = 4


def _nsa_attention(z3, kv_cmp):
    b, t_len, _ = z3.shape
    n_q = t_len // Q_BLOCK
    n_slc = t_len // SEL_BLOCK
    n_cmp_pad = kv_cmp.shape[2]
    n_sel = min(N_SELECT, n_slc)
    n_g = NSA_GROUPS_PER_STEP
    cols = NSA_GROUP * Q_BLOCK
    slopes = jnp.exp2(-8.0 * jnp.arange(1, NSA_HEADS + 1, dtype=F32) / NSA_HEADS)
    c_idx = np.arange(n_cmp_pad)
    cmp_start = c_idx * CMP_STRIDE
    cmp_end = cmp_start + CMP_BLOCK - 1
    slc_start = np.arange(n_slc) * SEL_BLOCK
    overlap = np.clip(np.minimum(cmp_end[None, :], slc_start[:, None] + SEL_BLOCK - 1)
                      - np.maximum(cmp_start[None, :], slc_start[:, None]) + 1, 0, None).astype(np.float32)
    overlap[:, c_idx >= t_len // CMP_STRIDE - CMP_BLOCK // CMP_STRIDE + 1] = 0.0
    kernel = functools.partial(_nsa_kernel, n_sel=n_sel, n_g=n_g)
    n_kc = t_len // NSA_KEY_CHUNK
    q_spec = lambda gg: pl.BlockSpec((1, Q_BLOCK, cols),
                                     lambda bi, g, qi: (bi, qi, ZB_Q * LANE // cols + g * n_g + gg))
    slab = lambda zb: pl.BlockSpec((1, t_len, n_g * LANE), lambda bi, g, qi: (bi, 0, zb // n_g + g))
    assert ZB_KVS % n_g == 0 and ZB_KVW % n_g == 0 and ZB_GATE % n_g == 0
    return pl.pallas_call(
        kernel,
        out_shape=jax.ShapeDtypeStruct((b, t_len, NSA_WIDTH), BF16),
        grid=(b, NSA_KV_HEADS // n_g, n_q),
        in_specs=[pl.BlockSpec(memory_space=pltpu.SMEM)] + [q_spec(gg) for gg in range(n_g)] + [
            pl.BlockSpec((1, n_g, n_cmp_pad, LANE), lambda bi, g, qi: (bi, g, 0, 0)),
            slab(ZB_KVS), slab(ZB_KVW),
            pl.BlockSpec((1, Q_BLOCK, n_g * LANE), lambda bi, g, qi: (bi, qi, ZB_GATE // n_g + g)),
            pl.BlockSpec((n_slc, n_cmp_pad), lambda bi, g, qi: (0, 0)),
        ],
        out_specs=pl.BlockSpec((1, Q_BLOCK, n_g * cols), lambda bi, g, qi: (bi, qi, g)),
        scratch_shapes=[pltpu.VMEM((n_g, n_kc, NSA_V_ROWS, NSA_KEY_CHUNK), BF16),
                        pltpu.VMEM((n_g, n_kc, NSA_V_ROWS, NSA_KEY_CHUNK), BF16),
                        pltpu.VMEM((n_g, HEAD_DIM, n_cmp_pad), BF16),
                        pltpu.VMEM((n_g, n_slc, 8, cols), F32),
                        pltpu.VMEM((n_g, NSA_KEY_CHUNK, cols), F32),
                        pltpu.VMEM((n_g, NSA_KEY_CHUNK, cols), F32),
                        pltpu.VMEM((n_g, NSA_KEY_CHUNK, cols), BF16),
                        pltpu.VMEM((n_g, NSA_KEY_CHUNK, cols), BF16),
                        pltpu.VMEM((n_g, t_len, LANE), BF16),
                        pltpu.VMEM((n_g, t_len, LANE), BF16)],
        compiler_params=pltpu.CompilerParams(dimension_semantics=("parallel", "parallel", "arbitrary"),
                                             vmem_limit_bytes=VMEM_LIMIT),
        name="nsa_attention",
    )(slopes, *([z3] * n_g), kv_cmp, z3, z3, z3, jnp.asarray(overlap, BF16))


def _nsa_prepare(cmp_pe_k, cmp_w1_k, cmp_w2_k, cmp_pe_v, cmp_w1_v, cmp_w2_v):
    pe = jnp.concatenate([cmp_pe_k, cmp_pe_v], axis=-1)
    w1 = _block_diag2(cmp_w1_k.reshape(CMP_BLOCK, HEAD_DIM, HEAD_DIM),
                      cmp_w1_v.reshape(CMP_BLOCK, HEAD_DIM, HEAD_DIM)).astype(BF16)
    w2 = _block_diag2(cmp_w2_k, cmp_w2_v).astype(BF16)
    return pe, w1, w2


RWKV_CHUNK = 64


def _rwkv_prep_kernel(z_ref, mu_ref, w0_ref, wup_ref, a0_ref, aup_ref, gup_ref, kk_ref, ka_ref,
                      r_ref, k_ref, v_ref, kkr_ref, a_ref, lw_ref, g_ref, carry_ref):
    w = RWKV_WIDTH

    @pl.when(pl.program_id(1) == 0)
    def _():
        carry_ref[...] = jnp.zeros_like(carry_ref)

    z = z_ref[0].astype(F32)
    tc = z.shape[0]
    row = lax.broadcasted_iota(jnp.int32, z.shape, 0)
    prev = jnp.where(row == 0, carry_ref[0:1, :], pltpu.roll(z, 1, 0))
    carry_ref[0:1, :] = z[tc - 1:tc, :]
    zs = z + (prev - z) * mu_ref[...]
    r = zs[:, RB_R * LANE:RB_R * LANE + w]
    k = zs[:, RB_K * LANE:RB_K * LANE + w]
    v = zs[:, RB_V * LANE:RB_V * LANE + w]
    w_lo = zs[:, RB_WLO * LANE:(RB_WLO + 1) * LANE]
    a_lo = zs[:, RB_ALO * LANE:(RB_ALO + 1) * LANE]
    g_lo = zs[:, RB_GLO * LANE:RB_GLO * LANE + GATE_LORA]
    d = w0_ref[...] + jnp.dot(jnp.tanh(w_lo).astype(BF16), wup_ref[...], preferred_element_type=F32)
    w_raw = -jax.nn.softplus(-d) - 0.5
    lw_ref[0] = -jnp.exp(w_raw)
    a = jax.nn.sigmoid(a0_ref[...] + jnp.dot(a_lo.astype(BF16), aup_ref[...], preferred_element_type=F32))
    g = jnp.dot(jax.nn.sigmoid(g_lo).astype(BF16), gup_ref[...], preferred_element_type=F32)
    r_ref[0] = r.astype(r_ref.dtype)
    v_ref[0] = v.astype(v_ref.dtype)
    kkr_ref[0] = (k * kk_ref[...]).astype(kkr_ref.dtype)
    k_ref[0] = (k * (1.0 + (a - 1.0) * ka_ref[...])).astype(k_ref.dtype)
    a_ref[0] = a.astype(a_ref.dtype)
    g_ref[0] = g.astype(g_ref.dtype)


def _rwkv_prep(z3, mu, w0, w_up, a0, a_up, g_up, k_k, k_a, tc=256):
    b, t_len, _ = z3.shape
    w = RWKV_WIDTH
    ncol = RWKV_BLOCKS * LANE
    vec = lambda n: pl.BlockSpec((1, n), lambda bi, ti: (0, 0))
    mat = lambda m, n: pl.BlockSpec((m, n), lambda bi, ti: (0, 0))
    out_bf = jax.ShapeDtypeStruct((b, t_len, w), BF16)
    out_f32 = jax.ShapeDtypeStruct((b, t_len, w), F32)
    out_spec = pl.BlockSpec((1, tc, w), lambda bi, ti: (bi, ti, 0))
    return pl.pallas_call(
        _rwkv_prep_kernel,
        out_shape=(out_bf, out_bf, out_bf, out_bf, out_bf, out_f32, out_bf),
        grid=(b, t_len // tc),
        in_specs=[pl.BlockSpec((1, tc, ncol), lambda bi, ti: (bi, ti, ZB_RWKV)),
                  vec(ncol), vec(w), mat(LANE, w), vec(w), mat(LANE, w), mat(GATE_LORA, w), vec(w), vec(w)],
        out_specs=(out_spec,) * 7,
        scratch_shapes=[pltpu.VMEM((8, ncol), F32)],
        compiler_params=pltpu.CompilerParams(dimension_semantics=("parallel", "arbitrary"),
                                             vmem_limit_bytes=VMEM_LIMIT),
        name="rwkv_prep",
    )(z3, mu, w0, w_up, a0, a_up, g_up, k_k, k_a)


def _pair_blocks(x):
    lane = lax.broadcasted_iota(jnp.int32, x.shape, 1)
    zero = jnp.zeros((), x.dtype)
    return jnp.concatenate([jnp.where(lane < HEAD_DIM, x, zero), jnp.where(lane >= HEAD_DIM, x, zero)], axis=0)


def _fold_pair(x):
    n = x.shape[0] // 2
    return x[:n] + x[n:]


def _split3(x):
    hi = x.astype(BF16)
    r1 = x - hi.astype(F32)
    mid = r1.astype(BF16)
    return hi, mid, (r1 - mid.astype(F32)).astype(BF16)


def _dot_split_rhs(a_bf, x):
    hi, mid, lo = _split3(x)
    return (jnp.dot(a_bf, hi, preferred_element_type=F32) + jnp.dot(a_bf, mid, preferred_element_type=F32)
            + jnp.dot(a_bf, lo, preferred_element_type=F32))


RWKV_PAIRS_PER_STEP = 8


def _rwkv_scan_kernel(r_ref, k_ref, v_ref, kkr_ref, a_ref, lw_ref, g_ref, rk_ref, lng_ref, lnb_ref,
                      o_ref, s_ref, *, n_chunks, n_pairs):
    L = RWKV_CHUNK
    L2 = 2 * L

    @pl.when(pl.program_id(2) == 0)
    def _():
        s_ref[...] = jnp.zeros_like(s_ref)

    ri = lax.broadcasted_iota(jnp.int32, (L, L), 0)
    ci = lax.broadcasted_iota(jnp.int32, (L, L), 1)
    tri_incl = jnp.where(ri >= ci, 1.0, 0.0).astype(BF16)
    r2 = lax.broadcasted_iota(jnp.int32, (L2, L2), 0)
    c2 = lax.broadcasted_iota(jnp.int32, (L2, L2), 1)
    same_head = (r2 // L) == (c2 // L)
    strict2 = same_head & (r2 > c2)
    incl2 = same_head & (r2 >= c2)
    eye2 = jnp.where(r2 == c2, 1.0, 0.0).astype(F32)
    h_r = lax.broadcasted_iota(jnp.int32, (LANE, LANE), 0) // HEAD_DIM
    h_c = lax.broadcasted_iota(jnp.int32, (LANE, LANE), 1) // HEAD_DIM
    head_mask = h_r == h_c
    head_ones = jnp.where(head_mask, 1.0, 0.0).astype(BF16)

    def head_sum(x):
        hi, mid, _ = _split3(x)
        return (jnp.dot(hi, head_ones, preferred_element_type=F32)
                + jnp.dot(mid, head_ones, preferred_element_type=F32))

    def mm(a, b):
        return jnp.dot(a.astype(BF16), b.astype(BF16), preferred_element_type=F32)

    chains = [(hp, c) for hp in range(n_pairs) for c in range(n_chunks)]
    lanes_of = lambda hp: slice(hp * LANE, (hp + 1) * LANE)
    rows_of = lambda c: pl.ds(c * L, L)
    each = lambda fn: {ch: fn(ch) for ch in chains}

    load = lambda ref: each(lambda ch: ref[0, rows_of(ch[1]), lanes_of(ch[0])])
    r, k, v, a = (each(lambda ch, d=d: d[ch].astype(F32)) for d in (load(r_ref), load(k_ref), load(v_ref), load(a_ref)))
    kkr = each(lambda ch, d=load(kkr_ref): d[ch].astype(F32))
    lw = load(lw_ref)
    kk_sq = each(lambda ch: head_sum(kkr[ch] * kkr[ch]))
    cl = each(lambda ch: _dot_split_rhs(tri_incl, lw[ch]))
    kk = each(lambda ch: kkr[ch] / jnp.maximum(jnp.sqrt(kk_sq[ch]), 1e-12))
    p_incl = each(lambda ch: jnp.exp(cl[ch]))
    p_inv = each(lambda ch: jnp.exp(-cl[ch]))
    rt = each(lambda ch: (r[ch] * p_incl[ch]).astype(BF16))
    at = each(lambda ch: (-kk[ch] * jnp.exp(cl[ch] - lw[ch])).astype(BF16))
    kt = each(lambda ch: k[ch] * p_inv[ch])
    bt = each(lambda ch: kk[ch] * a[ch] * p_inv[ch])
    gram = each(lambda ch: _dot_nt(
        jnp.concatenate([_pair_blocks(at[ch]), _pair_blocks(rt[ch])], axis=0),
        jnp.concatenate([_pair_blocks(bt[ch]), _pair_blocks(kt[ch])], axis=0).astype(BF16)))
    a_ab = each(lambda ch: jnp.where(strict2, gram[ch][:L2, :L2], 0.0))
    a_ak = each(lambda ch: jnp.where(strict2, gram[ch][:L2, L2:], 0.0).astype(BF16))
    a_rb = each(lambda ch: jnp.where(incl2, gram[ch][L2:, :L2], 0.0).astype(BF16))
    a_rk = each(lambda ch: jnp.where(incl2, gram[ch][L2:, L2:], 0.0).astype(BF16))
    tinv = each(lambda ch: eye2 + a_ab[ch])
    apow = a_ab
    for _ in range(int(np.log2(L)) - 1):
        apow = each(lambda ch: mm(apow[ch], apow[ch]))
        tinv = each(lambda ch: tinv[ch] + mm(tinv[ch], apow[ch]))
    tinv = each(lambda ch: tinv[ch].astype(BF16))
    v_bf = each(lambda ch: v[ch].astype(BF16))
    v2 = each(lambda ch: _pair_blocks(v_bf[ch]))
    akv = each(lambda ch: _fold_pair(jnp.dot(a_ak[ch], v2[ch], preferred_element_type=F32)))
    rkv = each(lambda ch: _fold_pair(jnp.dot(a_rk[ch], v2[ch], preferred_element_type=F32)))
    p_last = each(lambda ch: p_incl[ch][L - 1:L, :])
    wts = each(lambda ch: jnp.concatenate([bt[ch] * p_last[ch], kt[ch] * p_last[ch]], axis=0).astype(BF16))
    bonus = each(lambda ch: head_sum(r[ch] * k[ch] * rk_ref[:, lanes_of(ch[0])]) * v[ch])

    pairs = range(n_pairs)
    states = [s_ref[hp] for hp in pairs]
    for c in range(n_chunks):
        s_bf = [states[hp].astype(BF16) for hp in pairs]
        m = [_dot_nt(at[hp, c], s_bf[hp]) + akv[hp, c] for hp in pairs]
        y = [_dot_nt(rt[hp, c], s_bf[hp]) + rkv[hp, c] for hp in pairs]
        u = [_fold_pair(jnp.dot(tinv[hp, c], _pair_blocks(m[hp].astype(BF16)), preferred_element_type=F32))
             for hp in pairs]
        u_bf = [u[hp].astype(BF16) for hp in pairs]
        upd = [lax.dot_general(jnp.concatenate([u_bf[hp], v_bf[hp, c]], axis=0), wts[hp, c],
                               (((0,), (0,)), ((), ())), preferred_element_type=F32) for hp in pairs]
        states = [states[hp] * p_last[hp, c] + jnp.where(head_mask, upd[hp], 0.0) for hp in pairs]
        y = [y[hp] + _fold_pair(jnp.dot(a_rb[hp, c], _pair_blocks(u_bf[hp]), preferred_element_type=F32))
             for hp in pairs]
        mean = [head_sum(y[hp]) * (1.0 / HEAD_DIM) for hp in pairs]
        yc = [y[hp] - mean[hp] for hp in pairs]
        var = [head_sum(yc[hp] * yc[hp]) * (1.0 / HEAD_DIM) for hp in pairs]
        for hp in pairs:
            yn = yc[hp] * lax.rsqrt(var[hp] + GN_EPS) * lng_ref[:, lanes_of(hp)] + lnb_ref[:, lanes_of(hp)]
            o_ref[0, rows_of(c), lanes_of(hp)] = (
                (yn + bonus[hp, c]) * g_ref[0, rows_of(c), lanes_of(hp)].astype(F32)).astype(o_ref.dtype)
    for hp in pairs:
        s_ref[hp] = states[hp]


def _rwkv_scan(r, k, v, kkr, a, lw, g, r_k, ln_g, ln_b, tt=256):
    b, t_len, w = r.shape
    n_p = RWKV_PAIRS_PER_STEP
    width = n_p * LANE
    tile = pl.BlockSpec((1, tt, width), lambda bi, hp, ti: (bi, ti, hp))
    vec = pl.BlockSpec((1, width), lambda bi, hp, ti: (0, hp))
    kernel = functools.partial(_rwkv_scan_kernel, n_chunks=tt // RWKV_CHUNK, n_pairs=n_p)
    return pl.pallas_call(
        kernel,
        out_shape=jax.ShapeDtypeStruct((b, t_len, w), BF16),
        grid=(b, w // width, t_len // tt),
        in_specs=[tile] * 7 + [vec] * 3,
        out_specs=tile,
        scratch_shapes=[pltpu.VMEM((n_p, LANE, LANE), F32)],
        compiler_params=pltpu.CompilerParams(dimension_semantics=("parallel", "parallel", "arbitrary"),
                                             vmem_limit_bytes=VMEM_LIMIT),
        name="rwkv_scan",
    )(r, k, v, kkr, a, lw, g, r_k, ln_g, ln_b)


def _rwkv_time_mix(z3, mu, w0, w_up, a0, a_up, g_up, k_k, k_a, r_k, ln_g, ln_b):
    row = lambda u: u.reshape(1, -1)
    mu_p = row(_take_columns(mu, _rwkv_source_columns()))
    r, k, v, kkr, a, lw, g = _rwkv_prep(z3, mu_p, row(w0), _pad_rows(w_up, LANE).astype(BF16), row(a0),
                                        _pad_rows(a_up, LANE).astype(BF16), g_up.astype(BF16), row(k_k), row(k_a))
    return _rwkv_scan(r, k, v, kkr, a, lw, g, row(r_k), row(ln_g), row(ln_b))


def _layer_norm(x, g, b):
    mean = jnp.mean(x, axis=-1, keepdims=True)
    xc = x - mean
    var = jnp.mean(xc * xc, axis=-1, keepdims=True)
    return xc * lax.rsqrt(var + LN_EPS) * g + b


def _out_proj_kernel(on_ref, or_ref, x_ref, w_ref, g_ref, b_ref, rwh_ref, rwm_ref, rb_ref,
                     h_ref, idx_ref, wgt_ref, *, alpha):
    o = jnp.concatenate([on_ref[...], or_ref[...]], axis=1)
    mix = jnp.dot(o, w_ref[...], preferred_element_type=F32)
    h = _layer_norm(alpha * x_ref[...] + mix, g_ref[...], b_ref[...])
    h_ref[...] = h
    h_hi, h_mid, _ = _split3(h)
    logits = (jnp.dot(h_hi, rwh_ref[...], preferred_element_type=F32)
              + jnp.dot(h_hi, rwm_ref[...], preferred_element_type=F32)
              + jnp.dot(h_mid, rwh_ref[...], preferred_element_type=F32)) + rb_ref[...]
    lane = lax.broadcasted_iota(jnp.int32, logits.shape, 1)
    logits = jnp.where(lane < N_EXPERTS, logits, -jnp.inf)
    idx_out = jnp.zeros(logits.shape, jnp.int32)
    val_out = jnp.full(logits.shape, -jnp.inf, F32)
    for k in range(TOP_K):
        best = jnp.max(logits, axis=-1, keepdims=True)
        first = jnp.min(jnp.where(logits == best, lane, LANE), axis=-1, keepdims=True)
        idx_out = jnp.where(lane == k, first, idx_out)
        val_out = jnp.where(lane == k, best, val_out)
        logits = jnp.where(lane == first, -jnp.inf, logits)
    e = jnp.exp(val_out - jnp.max(val_out, axis=-1, keepdims=True))
    idx_ref[...] = idx_out
    wgt_ref[...] = e / jnp.sum(e, axis=-1, keepdims=True)


def _out_proj_router(o_nsa, o_rwkv, x2, w_out, ln_g, ln_b, router_w, router_b, alpha, tm=512):
    n_tok, d = x2.shape
    half = o_nsa.shape[1]
    rw = jnp.pad(router_w, ((0, 0), (0, LANE - N_EXPERTS)))
    rw_hi = rw.astype(BF16)
    rw_mid = (rw - rw_hi.astype(F32)).astype(BF16)
    rb = jnp.pad(router_b, (0, LANE - N_EXPERTS)).reshape(1, LANE)
    row_blk = lambda n: pl.BlockSpec((tm, n), lambda i: (i, 0))
    full = lambda m, n: pl.BlockSpec((m, n), lambda i: (0, 0))
    return pl.pallas_call(
        functools.partial(_out_proj_kernel, alpha=alpha),
        out_shape=(jax.ShapeDtypeStruct((n_tok, d), F32), jax.ShapeDtypeStruct((n_tok, LANE), jnp.int32),
                   jax.ShapeDtypeStruct((n_tok, LANE), F32)),
        grid=(n_tok // tm,),
        in_specs=[row_blk(half), row_blk(half), row_blk(d), full(2 * half, d), full(1, d), full(1, d),
                  full(d, LANE), full(d, LANE), full(1, LANE)],
        out_specs=(row_blk(d), row_blk(LANE), row_blk(LANE)),
        compiler_params=pltpu.CompilerParams(dimension_semantics=("parallel",), vmem_limit_bytes=VMEM_LIMIT),
        name="out_proj_router",
    )(o_nsa, o_rwkv, x2, w_out.astype(BF16), ln_g.reshape(1, d), ln_b.reshape(1, d), rw_hi, rw_mid, rb)


MOE_ITEM_ROWS = 1280
MOE_SUB_ROWS = 256
MOE_F_TILE = 256
MOE_TOKEN_TILE = 256


MOE_RANK_BLOCK = 512


def _moe_rank_kernel(e_ref, rank_ref, cnt_ref, carry_ref):
    @pl.when(pl.program_id(0) == 0)
    def _():
        carry_ref[...] = jnp.zeros_like(carry_ref)

    blk = e_ref.shape[0]
    lane = lax.broadcasted_iota(jnp.int32, (blk, LANE), 1)
    onehot = jnp.where(e_ref[...] == lane, 1.0, 0.0)
    tri = jnp.where(lax.broadcasted_iota(jnp.int32, (blk, blk), 0) >= lax.broadcasted_iota(jnp.int32, (blk, blk), 1),
                    1.0, 0.0).astype(BF16)
    csum = jnp.dot(tri, onehot.astype(BF16), preferred_element_type=F32)
    carry = carry_ref[...]
    rank_ref[...] = (jnp.sum((csum + carry) * onehot, axis=1, keepdims=True) - 1.0).astype(jnp.int32)
    carry_ref[...] = carry + csum[blk - 1:blk, :]
    cnt_ref[...] = carry_ref[...].astype(jnp.int32)


def _moe_rank(flat_e):
    n_assign = flat_e.shape[0]
    return pl.pallas_call(
        _moe_rank_kernel,
        out_shape=(jax.ShapeDtypeStruct((n_assign, 1), jnp.int32), jax.ShapeDtypeStruct((1, LANE), jnp.int32)),
        grid=(n_assign // MOE_RANK_BLOCK,),
        in_specs=[pl.BlockSpec((MOE_RANK_BLOCK, 1), lambda i: (i, 0))],
        out_specs=(pl.BlockSpec((MOE_RANK_BLOCK, 1), lambda i: (i, 0)), pl.BlockSpec((1, LANE), lambda i: (0, 0))),
        scratch_shapes=[pltpu.VMEM((1, LANE), F32)],
        compiler_params=pltpu.CompilerParams(dimension_semantics=("arbitrary",), vmem_limit_bytes=VMEM_LIMIT),
        name="moe_rank",
    )(flat_e.reshape(n_assign, 1))


def _moe_tables(top_idx, n_items):
    n_tok = top_idx.shape[0]
    flat_e = top_idx.reshape(-1)
    rank, counts = _moe_rank(flat_e)
    counts = counts[0, :N_EXPERTS]
    items_e = (counts + MOE_ITEM_ROWS - 1) // MOE_ITEM_ROWS
    items_end = jnp.cumsum(items_e)
    item_start_e = items_end - items_e
    dest = jnp.take(item_start_e * MOE_ITEM_ROWS, flat_e) + rank[:, 0]
    item = jnp.arange(n_items, dtype=jnp.int32)
    valid = item < items_end[-1]
    last_e = jnp.max(jnp.where(counts > 0, jnp.arange(N_EXPERTS, dtype=jnp.int32), 0))
    item_e = jnp.minimum(jnp.sum(items_end[None, :] <= item[:, None], axis=1).astype(jnp.int32), N_EXPERTS - 1)
    item_e = jnp.where(valid, item_e, last_e)
    item_nv = jnp.where(valid, jnp.clip(counts[item_e] - (item - item_start_e[item_e]) * MOE_ITEM_ROWS,
                                        0, MOE_ITEM_ROWS), 0).astype(jnp.int32)
    return dest.reshape(n_tok, TOP_K), item_e, item_nv


DISPATCH_ISSUE_UNROLL = 8


def _dispatch_kernel(nv_ref, dest_ref, h_ref, xg_ref, stage_ref, zero_ref, sem_ref, zsem_ref):
    tm = h_ref.shape[0]
    i = pl.program_id(0)
    slot = i % 2
    subs_per_item = MOE_ITEM_ROWS // MOE_SUB_ROWS

    @pl.when(i == 0)
    def _():
        zero_ref[...] = jnp.zeros_like(zero_ref)

        def fill(sub):
            return pltpu.make_async_copy(
                zero_ref, xg_ref.at[pl.ds(pl.multiple_of(sub * MOE_SUB_ROWS, MOE_SUB_ROWS), MOE_SUB_ROWS)], zsem_ref)

        def not_full(sub):
            return nv_ref[sub // subs_per_item] < (sub % subs_per_item + 1) * MOE_SUB_ROWS

        def start_body(sub, carry):
            @pl.when(not_full(sub))
            def _():
                fill(sub).start()
            return carry

        def wait_body(sub, carry):
            @pl.when(not_full(sub))
            def _():
                fill(sub).wait()
            return carry

        n_subs = xg_ref.shape[0] // MOE_SUB_ROWS
        lax.fori_loop(0, n_subs, start_body, 0)
        lax.fori_loop(0, n_subs, wait_body, 0)

    def wait_slot(s):
        for _ in range(TOP_K):
            pltpu.make_async_copy(stage_ref.at[s], xg_ref.at[pl.ds(0, tm)], sem_ref.at[s]).wait()

    @pl.when(i >= 2)
    def _():
        wait_slot(slot)

    stage_ref[slot] = h_ref[...]

    def body(j, carry):
        for u in range(DISPATCH_ISSUE_UNROLL):
            t = j * DISPATCH_ISSUE_UNROLL + u
            for k in range(TOP_K):
                pltpu.make_async_copy(stage_ref.at[slot, pl.ds(t, 1)], xg_ref.at[pl.ds(dest_ref[0, 0, k * tm + t], 1)],
                                      sem_ref.at[slot]).start(priority=k % 2)
        return carry

    lax.fori_loop(0, tm // DISPATCH_ISSUE_UNROLL, body, 0)

    @pl.when(i == pl.num_programs(0) - 1)
    def _():
        wait_slot(slot)

        @pl.when(i >= 1)
        def _():
            wait_slot(1 - slot)


def _dispatch(h1, dest_tiles, item_nv, n_rows, tm):
    n_tok, d = h1.shape
    return pl.pallas_call(
        _dispatch_kernel,
        out_shape=jax.ShapeDtypeStruct((n_rows, d), F32),
        grid=(n_tok // tm,),
        in_specs=[pl.BlockSpec(memory_space=pltpu.SMEM),
                  pl.BlockSpec((1, 1, TOP_K * tm), lambda i: (i, 0, 0), memory_space=pltpu.SMEM),
                  pl.BlockSpec((tm, d), lambda i: (i, 0))],
        out_specs=pl.BlockSpec(memory_space=pl.ANY),
        scratch_shapes=[pltpu.VMEM((2, tm, d), F32), pltpu.VMEM((MOE_SUB_ROWS, d), F32),
                        pltpu.SemaphoreType.DMA((2,)), pltpu.SemaphoreType.DMA(())],
        compiler_params=pltpu.CompilerParams(dimension_semantics=("arbitrary",), vmem_limit_bytes=VMEM_LIMIT),
        name="moe_dispatch",
    )(item_nv, dest_tiles, h1)


def _moe_kernel(item_e_ref, item_nv_ref, xg_ref, wg_ref, bg_ref, wu_ref, bu_ref, wd_ref, bd_ref, o_ref,
                stage_ref, xb_ref, sem_ref):
    i = pl.program_id(0)
    f = pl.program_id(1)
    nv = item_nv_ref[i]
    n_sub_max = MOE_ITEM_ROWS // MOE_SUB_ROWS
    sub_count = lambda rows_used: (rows_used + MOE_SUB_ROWS - 1) // MOE_SUB_ROWS
    n_sub = sub_count(nv)
    rows = [pl.ds(sb * MOE_SUB_ROWS, MOE_SUB_ROWS) for sb in range(n_sub_max)]
    cur = i % 2

    def fetch(item, sb):
        first = pl.multiple_of(item * MOE_ITEM_ROWS + sb * MOE_SUB_ROWS, MOE_SUB_ROWS)
        return pltpu.make_async_copy(xg_ref.at[pl.ds(first, MOE_SUB_ROWS)], stage_ref.at[sb % 2], sem_ref.at[sb % 2])

    @pl.when((i == 0) & (f == 0))
    def _():
        for sb in range(n_sub_max):
            @pl.when(sb < n_sub)
            def _(sb=sb):
                copy = fetch(0, sb)
                copy.start()
                copy.wait()
                xb_ref[0, rows[sb], :] = stage_ref[sb % 2].astype(BF16)

    has_next = i + 1 < pl.num_programs(0)
    n_sub_next = jnp.where(has_next, sub_count(item_nv_ref[jnp.minimum(i + 1, pl.num_programs(0) - 1)]), 0)
    for sb in range(n_sub_max):
        @pl.when((f == sb + 1) & (sb < n_sub_next))
        def _(sb=sb):
            fetch(i + 1, sb).wait()
            xb_ref[1 - cur, rows[sb], :] = stage_ref[sb % 2].astype(BF16)

        @pl.when((f == sb) & (sb < n_sub_next))
        def _(sb=sb):
            fetch(i + 1, sb).start()

    @pl.when(f == 0)
    def _():
        for sb in range(n_sub_max):
            bias = jnp.broadcast_to(bd_ref[0], (MOE_SUB_ROWS, o_ref.shape[1]))
            o_ref[rows[sb], :] = jnp.where(sb * MOE_SUB_ROWS < nv, bias, 0.0)

    for n in range(1, n_sub_max + 1):
        @pl.when(n_sub == n)
        def _(n=n):
            wg = wg_ref[0].astype(BF16)
            wu = wu_ref[0].astype(BF16)
            wd = wd_ref[0].astype(BF16)
            xs = [xb_ref[cur, rows[sb], :] for sb in range(n)]
            gates = [jnp.dot(xs[sb], wg, preferred_element_type=F32) for sb in range(n)]
            ups = [jnp.dot(xs[sb], wu, preferred_element_type=F32) for sb in range(n)]
            for sb in range(n):
                gate = jnp.minimum(gates[sb] + bg_ref[0], SWIGLU_LIMIT)
                up = jnp.clip(ups[sb] + bu_ref[0], -SWIGLU_LIMIT, SWIGLU_LIMIT)
                h = gate * jax.nn.sigmoid(SWIGLU_ALPHA * gate) * (up + 1.0)
                o_ref[rows[sb], :] += jnp.dot(h.astype(BF16), wd, preferred_element_type=F32)


def _moe_experts(xg, item_e, item_nv, w_gate, b_gate, w_up, b_up, w_down, b_down):
    n_rows = xg.shape[0]
    d = w_gate.shape[1]
    n_items = n_rows // MOE_ITEM_ROWS
    n_e, _, d_ff = w_gate.shape
    n_f = d_ff // MOE_F_TILE

    def f_idx(i, f, nv):
        return jnp.where(nv[i] > 0, f, n_f - 1)

    grid_spec = pltpu.PrefetchScalarGridSpec(
        num_scalar_prefetch=2,
        grid=(n_items, n_f),
        in_specs=[
            pl.BlockSpec(memory_space=pl.ANY),
            pl.BlockSpec((1, d, MOE_F_TILE), lambda i, f, e, nv: (e[i], 0, f_idx(i, f, nv))),
            pl.BlockSpec((1, 1, MOE_F_TILE), lambda i, f, e, nv: (e[i], 0, f_idx(i, f, nv))),
            pl.BlockSpec((1, d, MOE_F_TILE), lambda i, f, e, nv: (e[i], 0, f_idx(i, f, nv))),
            pl.BlockSpec((1, 1, MOE_F_TILE), lambda i, f, e, nv: (e[i], 0, f_idx(i, f, nv))),
            pl.BlockSpec((1, MOE_F_TILE, d), lambda i, f, e, nv: (e[i], f_idx(i, f, nv), 0)),
            pl.BlockSpec((1, 1, d), lambda i, f, e, nv: (e[i], 0, 0)),
        ],
        out_specs=pl.BlockSpec((MOE_ITEM_ROWS, d), lambda i, f, e, nv: (i, 0)),
        scratch_shapes=[pltpu.VMEM((2, MOE_SUB_ROWS, d), F32), pltpu.VMEM((2, MOE_ITEM_ROWS, d), BF16),
                        pltpu.SemaphoreType.DMA((2,))],
    )
    assert MOE_ITEM_ROWS // MOE_SUB_ROWS < n_f
    return pl.pallas_call(
        _moe_kernel,
        out_shape=jax.ShapeDtypeStruct((n_rows, d), F32),
        grid_spec=grid_spec,
        compiler_params=pltpu.CompilerParams(dimension_semantics=("arbitrary", "arbitrary"),
                                             vmem_limit_bytes=VMEM_LIMIT),
        name="moe_experts",
    )(item_e, item_nv, xg, w_gate, b_gate.reshape(n_e, 1, d_ff), w_up, b_up.reshape(n_e, 1, d_ff), w_down,
      b_down.reshape(n_e, 1, d))


COMBINE_ISSUE_UNROLL = 16


def _final_kernel(dest_ref, dest_next_ref, h_ref, eo_ref, wgt_ref, p_ref, g_ref, b_ref, wgate_ref, wple_ref,
                  o_ref, rows_ref, sem_ref, *, alpha):
    tm, d = h_ref.shape
    n_rows = TOP_K * tm
    i = pl.program_id(0)
    slot = i % 2

    def row_copy(idx_ref, r, s):
        return pltpu.make_async_copy(eo_ref.at[pl.ds(idx_ref[0, 0, r], 1)], rows_ref.at[s, pl.ds(r, 1)],
                                     sem_ref.at[s])

    def issue(idx_ref, s):
        def body(j, carry):
            for u in range(COMBINE_ISSUE_UNROLL):
                row_copy(idx_ref, j * COMBINE_ISSUE_UNROLL + u, s).start(priority=u % 2)
            return carry

        lax.fori_loop(0, n_rows // COMBINE_ISSUE_UNROLL, body, 0)

    @pl.when(i == 0)
    def _():
        issue(dest_ref, 0)

    @pl.when(i + 1 < pl.num_programs(0))
    def _():
        issue(dest_next_ref, 1 - slot)

    pltpu.make_async_copy(eo_ref.at[pl.ds(0, n_rows)], rows_ref.at[slot], sem_ref.at[slot]).wait()
    wgt = wgt_ref[...]
    ffn = wgt[:, 0:1] * rows_ref[slot, 0:tm, :]
    for k in range(1, TOP_K):
        ffn = ffn + wgt[:, k:k + 1] * rows_ref[slot, k * tm:(k + 1) * tm, :]
    h = _layer_norm(alpha * h_ref[...] + ffn, g_ref[...], b_ref[...])
    gate = jax.nn.sigmoid(jnp.dot(h.astype(BF16), wgate_ref[...], preferred_element_type=F32))
    ple = jnp.dot(p_ref[...].astype(BF16), wple_ref[...], preferred_element_type=F32)
    o_ref[...] = h + gate * ple


def _dest_tiles(dest, tm):
    n_tiles = dest.shape[0] // tm
    return dest.reshape(n_tiles, tm, TOP_K).transpose(0, 2, 1).reshape(n_tiles, 1, TOP_K * tm)


def _final(h1, eo, dest_tiles, top_w, p2, ln_g, ln_b, ple_gate_w, ple_w, alpha, tm):
    n_tok, d = h1.shape
    n_tiles = n_tok // tm
    row_blk = lambda n: pl.BlockSpec((tm, n), lambda i: (i, 0))
    full = lambda m, n: pl.BlockSpec((m, n), lambda i: (0, 0))
    idx_blk = lambda fn: pl.BlockSpec((1, 1, TOP_K * tm), fn, memory_space=pltpu.SMEM)
    return pl.pallas_call(
        functools.partial(_final_kernel, alpha=alpha),
        out_shape=jax.ShapeDtypeStruct((n_tok, d), F32),
        grid=(n_tiles,),
        in_specs=[idx_blk(lambda i: (i, 0, 0)), idx_blk(lambda i: (jnp.minimum(i + 1, n_tiles - 1), 0, 0)),
                  row_blk(d), pl.BlockSpec(memory_space=pl.ANY), row_blk(LANE), row_blk(p2.shape[1]),
                  full(1, d), full(1, d), full(d, d), full(p2.shape[1], d)],
        out_specs=row_blk(d),
        scratch_shapes=[pltpu.VMEM((2, TOP_K * tm, d), F32), pltpu.SemaphoreType.DMA((2,))],
        compiler_params=pltpu.CompilerParams(dimension_semantics=("arbitrary",), vmem_limit_bytes=VMEM_LIMIT),
        name="combine_ln_ple",
    )(dest_tiles, dest_tiles, h1, eo, top_w, p2, ln_g.reshape(1, d), ln_b.reshape(1, d),
      ple_gate_w.astype(BF16), ple_w.astype(BF16))


def kernel(x, p, w_in, cmp_pe_k, cmp_w1_k, cmp_w2_k, cmp_pe_v, cmp_w1_v, cmp_w2_v, rwkv_mu, rwkv_w0, rwkv_w_up, rwkv_a0, rwkv_a_up, rwkv_g_up, rwkv_k_k, rwkv_k_a, rwkv_r_k, rwkv_ln_g, rwkv_ln_b, w_out, ln1_g, ln1_b, router_w, router_b, exp_w_gate, exp_b_gate, exp_w_up, exp_b_up, exp_w_down, exp_b_down, ln2_g, ln2_b, ple_w, ple_gate_w):
    b, t_len, d = x.shape
    depth = w_in.shape[0]
    alpha = float((2 * depth) ** 0.25)
    n_tok = b * t_len
    n_items = (n_tok * TOP_K) // MOE_ITEM_ROWS + N_EXPERTS
    h = x.reshape(n_tok, d)
    for i in range(depth):
        w_bf = _take_columns(w_in[i].astype(BF16), _z_source_columns())
        z3 = _in_proj(h, w_bf).reshape(b, t_len, Z_COLS)
        pe, w1, w2 = _nsa_prepare(cmp_pe_k[i], cmp_w1_k[i], cmp_w2_k[i], cmp_pe_v[i], cmp_w1_v[i], cmp_w2_v[i])
        kv_cmp = _compress(z3, pe, w1, w2)
        o_nsa = _nsa_attention(z3, kv_cmp)
        o_rwkv = _rwkv_time_mix(z3, rwkv_mu[i], rwkv_w0[i], rwkv_w_up[i], rwkv_a0[i], rwkv_a_up[i], rwkv_g_up[i],
                                rwkv_k_k[i], rwkv_k_a[i], rwkv_r_k[i].reshape(-1), rwkv_ln_g[i], rwkv_ln_b[i])
        h1, top_idx, top_w = _out_proj_router(o_nsa.reshape(n_tok, -1), o_rwkv.reshape(n_tok, -1), h, w_out[i],
                                              ln1_g[i], ln1_b[i], router_w[i], router_b[i], alpha)
        dest, item_e, item_nv = _moe_tables(top_idx[:, :TOP_K], n_items)
        dest_tiles = _dest_tiles(dest, MOE_TOKEN_TILE)
        xg = _dispatch(h1, dest_tiles, item_nv, n_items * MOE_ITEM_ROWS, MOE_TOKEN_TILE)
        eo = _moe_experts(xg, item_e, item_nv, exp_w_gate[i], exp_b_gate[i], exp_w_up[i], exp_b_up[i],
                          exp_w_down[i], exp_b_down[i])
        h = _final(h1, eo, dest_tiles, top_w, p[i].reshape(n_tok, -1), ln2_g[i], ln2_b[i], ple_gate_w[i],
                   ple_w[i], alpha, MOE_TOKEN_TILE)
    return h.reshape(b, t_len, d)
```
